```python
import math
import jax, jax.numpy as jnp
from jax import lax
import numpy as np

D_MODEL = 1024
BATCH = 16
SEQ = 4096
DEPTH = 1

HEAD_DIM = 64
D_MIX = D_MODEL
A_HEADS = D_MIX // 2 // HEAD_DIM
A_KV = A_HEADS // 4
B_HEADS = D_MIX // 2 // HEAD_DIM
B_KV = B_HEADS // 4
D_A = A_HEADS * HEAD_DIM
D_B = B_HEADS * HEAD_DIM
KV_A = A_KV * HEAD_DIM
KV_B = B_KV * HEAD_DIM
N_HEADS_TOTAL = A_HEADS + B_HEADS
SWA_WINDOW = 128
ATTN_BLOCK = 128
CMP_LEN = 32
CMP_STRIDE = 16
CMP_HIDDEN = 256
SEL_LEN = 64
SEL_TOPK = 16
NSA_WINDOW = 512
NSA_Q_CHUNK = 32
N_NSA_BRANCHES = 3
N_BUCKETS = 32
MAX_DISTANCE = 128
FORCE_BONUS = 1e4
EPS = 1e-6
PROJ_SIZES = (D_A, KV_A, KV_A, D_A,
              D_B, KV_B, KV_B, KV_B, KV_B, KV_B, KV_B, D_B, B_HEADS * N_NSA_BRANCHES)
D_PROJ = sum(PROJ_SIZES)

kernel_name = "hybrid_swa_sink_nsa_adaln_block"


def _split_points():
    pts, acc = [], 0
    for s in PROJ_SIZES[:-1]:
        acc += s
        pts.append(acc)
    return pts


def rms_norm(x, gain):
    xf = x.astype(jnp.float32)
    y = xf * lax.rsqrt(jnp.mean(xf * xf, -1, keepdims=True) + EPS)
    return (y * gain.astype(jnp.float32)).astype(x.dtype)


def qk_norm(t, gain):
    tf = t.astype(jnp.float32)
    return tf * lax.rsqrt(jnp.mean(tf * tf, -1, keepdims=True) + EPS) * gain.astype(jnp.float32)


def t5_bucket(dist):
    n = jnp.maximum(dist, 0)
    max_exact = N_BUCKETS // 2
    nf = jnp.maximum(n, 1).astype(jnp.float32)
    large = max_exact + (jnp.log(nf / max_exact) / math.log(MAX_DISTANCE / max_exact)
                         * (N_BUCKETS - max_exact)).astype(jnp.int32)
    large = jnp.minimum(large, N_BUCKETS - 1)
    return jnp.where(n < max_exact, n, large)


def masked_softmax(logits, mask, sink=None):
    z = jnp.where(mask, logits, -jnp.inf)
    m = jnp.max(z, -1, keepdims=True)
    if sink is not None:
        m = jnp.maximum(m, sink)
    m = jnp.where(jnp.isfinite(m), m, 0.0)
    e = jnp.where(mask, jnp.exp(z - m), 0.0)
    denom = jnp.sum(e, -1, keepdims=True)
    if sink is not None:
        denom = denom + jnp.exp(sink - m)
    return e / jnp.maximum(denom, 1e-30)


def banded_gqa(q, k, v, window, head_bias, sinks):
    Bn, S, H, D = q.shape
    G = k.shape[2]
    R = H // G
    nblk = S // ATTN_BLOCK
    span = window + ATTN_BLOCK
    kp = jnp.pad(k.astype(jnp.float32), ((0, 0), (window, 0), (0, 0), (0, 0)))
    vp = jnp.pad(v.astype(jnp.float32), ((0, 0), (window, 0), (0, 0), (0, 0)))
    qb = q.reshape(Bn, nblk, ATTN_BLOCK, G, R, D).transpose(1, 0, 2, 3, 4, 5)
    hb = head_bias.astype(jnp.float32)
    sink = None if sinks is None else sinks.astype(jnp.float32).reshape(G, R, 1, 1)
    q_off = jnp.arange(ATTN_BLOCK)
    k_off = jnp.arange(span)

    def block(args):
        qi, i = args
        start = i * ATTN_BLOCK
        kb = lax.dynamic_slice_in_dim(kp, start, span, axis=1)
        vb = lax.dynamic_slice_in_dim(vp, start, span, axis=1)
        t = start + q_off
        s = start - window + k_off
        dist = t[:, None] - s[None, :]
        mask = (dist >= 0) & (dist < window) & (s[None, :] >= 0)
        bias = hb[t5_bucket(dist)].reshape(ATTN_BLOCK, span, G, R).transpose(2, 3, 0, 1)
        logits = jnp.einsum('btgrd,bsgd->bgrts', qi, kb) + bias
        p = masked_softmax(logits, mask, sink)
        return jnp.einsum('bgrts,bsgd->btgrd', p, vb)

    out = lax.map(block, (qb, jnp.arange(nblk)))
    return out.transpose(1, 0, 2, 3, 4, 5).reshape(Bn, S, H, D)


def compress_blocks(t, pos, w1, w2):
    Bn, S, G, D = t.shape
    nc = (S - CMP_LEN) // CMP_STRIDE + 1
    idx = jnp.arange(nc)[:, None] * CMP_STRIDE + jnp.arange(CMP_LEN)[None, :]
    blocks = t[:, idx] + pos[None, None, :, None, :]
    flat = blocks.transpose(0, 1, 3, 2, 4).reshape(Bn, nc, G, CMP_LEN * D)
    return jax.nn.silu(flat @ w1) @ w2


def nsa_cmp_sel(q, k_cmp, v_cmp, k_sel, v_sel, head_bias):
    Bn, S, H, D = q.shape
    G = k_sel.shape[2]
    R = H // G
    nc = k_cmp.shape[1]
    ns = S // SEL_LEN
    topk = min(SEL_TOPK, ns)
    qc_len = NSA_Q_CHUNK
    nch = S // qc_len
    kc = k_cmp.astype(jnp.float32)
    vc = v_cmp.astype(jnp.float32)
    ks_b = k_sel.astype(jnp.float32).reshape(Bn, ns, SEL_LEN, G, D).transpose(0, 3, 1, 2, 4)
    vs_b = v_sel.astype(jnp.float32).reshape(Bn, ns, SEL_LEN, G, D).transpose(0, 3, 1, 2, 4)
    c_lo = jnp.arange(nc) * CMP_STRIDE
    c_end = c_lo + CMP_LEN - 1
    s_lo = jnp.arange(ns) * SEL_LEN
    overlap = jnp.clip(jnp.minimum(c_lo[:, None] + CMP_LEN, s_lo[None, :] + SEL_LEN)
                       - jnp.maximum(c_lo[:, None], s_lo[None, :]), 0, None).astype(jnp.float32) / CMP_LEN
    tbl = head_bias.astype(jnp.float32).reshape(N_BUCKETS, G, R).transpose(1, 0, 2)
    g_idx = jnp.arange(G)[None, :, None, None]
    blk = jnp.arange(ns)
    gather = jax.vmap(jax.vmap(lambda kb, ix: kb[ix]))
    qch = q.reshape(Bn, nch, qc_len, G, R, D).transpose(1, 0, 2, 3, 4, 5)

    def chunk(args):
        qi, i = args
        t = i * qc_len + jnp.arange(qc_len)
        logits_c = jnp.einsum('btgrd,bngd->bgrtn', qi, kc)
        p_c = masked_softmax(logits_c, c_end[None, :] <= t[:, None])
        o_cmp = jnp.einsum('bgrtn,bngd->btgrd', p_c, vc)
        imp = jnp.einsum('bgrtn,nj->bgtj', p_c, overlap)
        cur = t // SEL_LEN
        valid = blk[None, :] <= cur[:, None]
        forced = (blk[None, :] == 0) | (blk[None, :] == cur[:, None]) | (blk[None, :] == cur[:, None] - 1)
        score = jnp.where(valid, imp + jnp.where(forced, FORCE_BONUS, 0.0), -jnp.inf)
        _, sel = lax.top_k(score, topk)
        kg = gather(ks_b, sel).reshape(Bn, G, qc_len, topk * SEL_LEN, D)
        vg = gather(vs_b, sel).reshape(Bn, G, qc_len, topk * SEL_LEN, D)
        spos = (sel[..., None] * SEL_LEN + jnp.arange(SEL_LEN)).reshape(Bn, G, qc_len, topk * SEL_LEN)
        dist = t[None, None, :, None] - spos
        bias = tbl[g_idx, t5_bucket(dist)].transpose(0, 1, 4, 2, 3)
        logits_s = jnp.einsum('btgrd,bgtld->bgrtl', qi, kg) + bias
        p_s = masked_softmax(logits_s, (dist >= 0)[:, :, None])
        o_sel = jnp.einsum('bgrtl,bgtld->btgrd', p_s, vg)
        return o_cmp, o_sel

    o_cmp, o_sel = lax.map(chunk, (qch, jnp.arange(nch)))
    o_cmp = o_cmp.transpose(1, 0, 2, 3, 4, 5).reshape(Bn, S, H, D)
    o_sel = o_sel.transpose(1, 0, 2, 3, 4, 5).reshape(Bn, S, H, D)
    return o_cmp, o_sel


def hybrid_layer(x, c, w_ada, b_ada, norm_gain, w_in, b_nsa_gate, q_gain_a, k_gain_a, sinks,
                 q_gain_b, k_gain_cmp, k_gain_sel, k_gain_win, cmp_pos_k, cmp_pos_v,
                 w_cmp_k1, w_cmp_k2, w_cmp_v1, w_cmp_v2, w_out, rel_bias):
    Bn, S, _ = x.shape
    qscale = HEAD_DIM ** -0.5
    mod = jax.nn.silu(c) @ w_ada + b_ada
    shift, scale, gate = jnp.split(mod, 3, axis=-1)
    h = rms_norm(x, norm_gain) * (1 + scale[:, None, :]) + shift[:, None, :]
    proj = h @ w_in
    (q_a, k_a, v_a, z_a, q_b, kc, vc, ks, vs, kw, vw, z_b, g_b) = jnp.split(proj, _split_points(), axis=-1)
    heads = lambda t, n: t.reshape(Bn, S, n, HEAD_DIM)

    qa = qk_norm(heads(q_a, A_HEADS), q_gain_a) * qscale
    ka = qk_norm(heads(k_a, A_KV), k_gain_a)
    o_a = banded_gqa(qa, ka, heads(v_a, A_KV), SWA_WINDOW, rel_bias[:, :A_HEADS], sinks)

    bias_b = rel_bias[:, A_HEADS:]
    qb = qk_norm(heads(q_b, B_HEADS), q_gain_b) * qscale
    k_cmp = qk_norm(compress_blocks(heads(kc, B_KV), cmp_pos_k, w_cmp_k1, w_cmp_k2), k_gain_cmp)
    v_cmp = compress_blocks(heads(vc, B_KV), cmp_pos_v, w_cmp_v1, w_cmp_v2)
    k_sel = qk_norm(heads(ks, B_KV), k_gain_sel)
    k_win = qk_norm(heads(kw, B_KV), k_gain_win)
    o_cmp, o_sel = nsa_cmp_sel(qb, k_cmp, v_cmp, k_sel, heads(vs, B_KV), bias_b)
    o_win = banded_gqa(qb, k_win, heads(vw, B_KV), NSA_WINDOW, bias_b, None)
    gb = jax.nn.sigmoid((g_b + b_nsa_gate).astype(jnp.float32)).reshape(Bn, S, B_HEADS, N_NSA_BRANCHES, 1)
    o_b = gb[..., 0, :] * o_cmp + gb[..., 1, :] * o_sel + gb[..., 2, :] * o_win

    y = jnp.concatenate([o_a.reshape(Bn, S, D_A) * jax.nn.silu(z_a.astype(jnp.float32)),
                         o_b.reshape(Bn, S, D_B) * jax.nn.silu(z_b.astype(jnp.float32))], axis=-1)
    out = y.astype(x.dtype) @ w_out
    return x + gate[:, None, :] * out


def setup_inputs(seed: int = 0) -> dict:
    key = jax.random.key(seed)
    ks = jax.random.split(key, 24)
    nrm = lambda k, shape, s: jax.random.normal(k, shape, jnp.float32) * s
    gain = lambda k, shape: 1.0 + 0.1 * jax.random.normal(k, shape, jnp.float32)
    L = DEPTH
    return {
        "x": nrm(ks[0], (BATCH, SEQ, D_MODEL), 1.0),
        "c": nrm(ks[1], (BATCH, D_MODEL), 1.0),
        "w_ada": nrm(ks[2], (L, D_MODEL, 3 * D_MODEL), 0.5 * D_MODEL ** -0.5),
        "b_ada": nrm(ks[3], (L, 3 * D_MODEL), 0.01),
        "norm_gain": gain(ks[4], (L, D_MODEL)),
        "w_in": nrm(ks[5], (L, D_MODEL, D_PROJ), D_MODEL ** -0.5),
        "b_nsa_gate": nrm(ks[6], (L, B_HEADS * N_NSA_BRANCHES), 0.1),
        "q_gain_a": gain(ks[7], (L, HEAD_DIM)),
        "k_gain_a": gain(ks[8], (L, HEAD_DIM)),
        "sinks": nrm(ks[9], (L, A_HEADS), 1.0),
        "q_gain_b": gain(ks[10], (L, HEAD_DIM)),
        "k_gain_cmp": gain(ks[11], (L, HEAD_DIM)),
        "k_gain_sel": gain(ks[12], (L, HEAD_DIM)),
        "k_gain_win": gain(ks[13], (L, HEAD_DIM)),
        "cmp_pos_k": nrm(ks[14], (L, CMP_LEN, HEAD_DIM), 0.5),
        "cmp_pos_v": nrm(ks[15], (L, CMP_LEN, HEAD_DIM), 0.5),
        "w_cmp_k1": nrm(ks[16], (L, CMP_LEN * HEAD_DIM, CMP_HIDDEN), (CMP_LEN * HEAD_DIM) ** -0.5),
        "w_cmp_k2": nrm(ks[17], (L, CMP_HIDDEN, HEAD_DIM), CMP_HIDDEN ** -0.5),
        "w_cmp_v1": nrm(ks[18], (L, CMP_LEN * HEAD_DIM, CMP_HIDDEN), (CMP_LEN * HEAD_DIM) ** -0.5),
        "w_cmp_v2": nrm(ks[19], (L, CMP_HIDDEN, HEAD_DIM), CMP_HIDDEN ** -0.5),
        "w_out": nrm(ks[20], (L, D_MIX, D_MODEL), D_MIX ** -0.5),
        "rel_bias": nrm(ks[21], (N_BUCKETS, N_HEADS_TOTAL), 0.5),
    }


def reference(x, c, w_ada, b_ada, norm_gain, w_in, b_nsa_gate, q_gain_a, k_gain_a, sinks,
              q_gain_b, k_gain_cmp, k_gain_sel, k_gain_win, cmp_pos_k, cmp_pos_v,
              w_cmp_k1, w_cmp_k2, w_cmp_v1, w_cmp_v2, w_out, rel_bias):
    for l in range(DEPTH):
        x = hybrid_layer(x, c, w_ada[l], b_ada[l], norm_gain[l], w_in[l], b_nsa_gate[l],
                         q_gain_a[l], k_gain_a[l], sinks[l], q_gain_b[l], k_gain_cmp[l],
                         k_gain_sel[l], k_gain_win[l], cmp_pos_k[l], cmp_pos_v[l],
                         w_cmp_k1[l], w_cmp_k2[l], w_cmp_v1[l], w_cmp_v2[l], w_out[l], rel_bias)
    return x
```

```python
import functools
import math

import jax
import jax.numpy as jnp
from jax import lax
from jax.experimental import pallas as pl
from jax.experimental.pallas import tpu as pltpu

MXU_DTYPE = jnp.bfloat16
F32 = jnp.float32

HEAD_DIM = 64
LANES = 128
Q_TILE = 128
KV_GROUPS = 2
REP = 4
ROWS = REP * Q_TILE
FAR_CHUNK = 512
SWA_WINDOW = 128
NSA_WINDOW = 512
CMP_LEN = 32
CMP_STRIDE = 16
SEL_LEN = 64
SEL_TOPK = 16
N_BUCKETS = 32
MAX_DISTANCE = 128
FORCE_BONUS = 1e4
EPS = 1e-6
NEG = -1e30
PROJ_TM = 512
VMEM_LIMIT = 48 * 1024 * 1024

OFF_QA, OFF_KA, OFF_VA, OFF_ZA = 0, 512, 640, 768
OFF_QB, OFF_KC, OFF_VC, OFF_KS, OFF_VS, OFF_KW, OFF_VW, OFF_ZB, OFF_GB = (
    1280, 1792, 1920, 2048, 2176, 2304, 2432, 2560, 3072)
D_PROJ = 3096
D_PROJ_PAD = 3200


def _dot(a, b):
    return jnp.dot(a, b, preferred_element_type=F32)


def _dot_nt(a, b):
    return lax.dot_general(a, b, (((1,), (1,)), ((), ())), preferred_element_type=F32)


def _split3(x):
    hi = x.astype(MXU_DTYPE)
    r1 = x - hi.astype(F32)
    mid = r1.astype(MXU_DTYPE)
    lo = (r1 - mid.astype(F32)).astype(MXU_DTYPE)
    return hi, mid, lo


def _mod_kernel(c_ref, w_ref, b_ref, o_ref):
    sc = jax.nn.silu(c_ref[...])
    w = w_ref[...]
    acc = jnp.zeros(o_ref.shape, F32)
    for a in _split3(sc):
        for b in _split3(w)[:2]:
            acc = acc + _dot(a, b)
    o_ref[...] = acc + b_ref[...]


def _mod_call(c, w_ada, b_ada):
    bsz, d = c.shape
    n = w_ada.shape[1]
    tn = 512
    return pl.pallas_call(
        _mod_kernel,
        grid=(n // tn,),
        in_specs=[pl.BlockSpec((bsz, d), lambda j: (0, 0)),
                  pl.BlockSpec((d, tn), lambda j: (0, j)),
                  pl.BlockSpec((1, tn), lambda j: (0, j))],
        out_specs=pl.BlockSpec((bsz, tn), lambda j: (0, j)),
        out_shape=jax.ShapeDtypeStruct((bsz, n), F32),
        name="adaln_mod",
    )(c, w_ada, b_ada.reshape(1, n))


def _proj_kernel(x_ref, mod_ref, gain_ref, w_ref, kg_ref, bg_ref,
                 qa_ref, ka_ref, va_ref, sza_ref, qb_ref, kc_ref, vc_ref,
                 ks_ref, vs_ref, kw_ref, vw_ref, szb_ref, gb_ref):
    tm = x_ref.shape[1]
    si = pl.program_id(1)
    x = x_ref[0]
    ms = jnp.mean(x * x, axis=-1, keepdims=True)
    y = x * lax.rsqrt(ms + EPS) * gain_ref[...]
    h = y * (1.0 + mod_ref[0, 1:2, :]) + mod_ref[0, 0:1, :]
    hb = h.astype(MXU_DTYPE)

    def seg(off, n):
        return _dot(hb, w_ref[:, off:off + n])

    lane = lax.broadcasted_iota(jnp.int32, (tm, LANES), 1)
    lo = lane < HEAD_DIM
    row = lax.broadcasted_iota(jnp.int32, (tm, LANES), 0) + si * tm
    onehot = jnp.where(lane - HEAD_DIM == row // SEL_LEN, 1.0, 0.0)

    def half_norm(t, gain_row):
        sq = t * t
        s_lo = jnp.sum(jnp.where(lo, sq, 0.0), axis=-1, keepdims=True)
        s_hi = jnp.sum(jnp.where(lo, 0.0, sq), axis=-1, keepdims=True)
        inv = jnp.where(lo, lax.rsqrt(s_lo * (1.0 / HEAD_DIM) + EPS),
                        lax.rsqrt(s_hi * (1.0 / HEAD_DIM) + EPS))
        return t * inv * gain_row

    def write_groups(ref, t, extra):
        ref[0, 0] = jnp.where(lo, t, extra).astype(ref.dtype)
        ref[0, 1] = jnp.where(lo, pltpu.roll(t, HEAD_DIM, axis=1), extra).astype(ref.dtype)

    qa_ref[0] = seg(OFF_QA, 512).astype(qa_ref.dtype)
    write_groups(ka_ref, half_norm(seg(OFF_KA, LANES), kg_ref[0:1, :]), onehot)
    write_groups(va_ref, seg(OFF_VA, LANES), 1.0)
    sza_ref[0] = jax.nn.silu(seg(OFF_ZA, 512)).astype(sza_ref.dtype)
    qb_ref[0] = seg(OFF_QB, 512).astype(qb_ref.dtype)
    kc_ref[0] = seg(OFF_KC, LANES).astype(kc_ref.dtype)
    vc_ref[0] = seg(OFF_VC, LANES).astype(vc_ref.dtype)
    write_groups(ks_ref, half_norm(seg(OFF_KS, LANES), kg_ref[1:2, :]), onehot)
    write_groups(vs_ref, seg(OFF_VS, LANES), 1.0)
    write_groups(kw_ref, half_norm(seg(OFF_KW, LANES), kg_ref[2:3, :]), onehot)
    write_groups(vw_ref, seg(OFF_VW, LANES), 1.0)
    szb_ref[0] = jax.nn.silu(seg(OFF_ZB, 512)).astype(szb_ref.dtype)
    gates = jax.nn.sigmoid(seg(OFF_GB, LANES) + bg_ref[...])
    gb_ref[0, 0] = gates
    gb_ref[0, 1] = pltpu.roll(gates, LANES - REP * 3, axis=1)


def _proj_call(x, mod3, norm_gain, w_in_p, kgains, bgate):
    bsz, s, d = x.shape
    tm = PROJ_TM
    dt = MXU_DTYPE
    row512 = pl.BlockSpec((1, tm, 512), lambda b, i: (b, i, 0))
    row128 = pl.BlockSpec((1, tm, LANES), lambda b, i: (b, i, 0))
    grp = pl.BlockSpec((1, KV_GROUPS, tm, LANES), lambda b, i: (b, 0, i, 0))
    s512 = jax.ShapeDtypeStruct((bsz, s, 512), dt)
    s128 = jax.ShapeDtypeStruct((bsz, s, LANES), dt)
    sgrp = jax.ShapeDtypeStruct((bsz, KV_GROUPS, s, LANES), dt)
    return pl.pallas_call(
        _proj_kernel,
        grid=(bsz, s // tm),
        in_specs=[pl.BlockSpec((1, tm, d), lambda b, i: (b, i, 0)),
                  pl.BlockSpec((1, 3, d), lambda b, i: (b, 0, 0)),
                  pl.BlockSpec((1, d), lambda b, i: (0, 0)),
                  pl.BlockSpec((d, D_PROJ_PAD), lambda b, i: (0, 0)),
                  pl.BlockSpec((8, LANES), lambda b, i: (0, 0)),
                  pl.BlockSpec((1, LANES), lambda b, i: (0, 0))],
        out_specs=[row512, grp, grp, row512, row512, row128, row128, grp, grp, grp, grp, row512, grp],
        out_shape=[s512, sgrp, sgrp, s512, s512, s128, s128, sgrp, sgrp, sgrp, sgrp, s512,
                   jax.ShapeDtypeStruct((bsz, KV_GROUPS, s, LANES), F32)],
        compiler_params=pltpu.CompilerParams(
            dimension_semantics=("arbitrary", "arbitrary"), vmem_limit_bytes=VMEM_LIMIT),
        name="norm_in_proj",
    )(x, mod3, norm_gain, w_in_p, kgains, bgate)


def _compress_kernel(kc_ref, vc_ref, wkt_ref, wkb_ref, wvt_ref, wvb_ref, pos_ref, w2k_ref, w2v_ref,
                     kg_ref, ko_ref, vo_ref):
    ncp = kc_ref.shape[1]
    lane = lax.broadcasted_iota(jnp.int32, (ncp, LANES), 1)
    lo = lane < HEAD_DIM

    def hidden(h_ref, wt_ref, wb_ref, ptop, pbot):
        hf = h_ref[0].astype(F32)
        top = _dot((hf + ptop).astype(MXU_DTYPE), wt_ref[...])
        bot = _dot((hf + pbot).astype(MXU_DTYPE), wb_ref[...])
        pre = top + pltpu.roll(bot, ncp - 1, axis=0)
        return jax.nn.silu(pre).astype(MXU_DTYPE)

    hk = hidden(kc_ref, wkt_ref, wkb_ref, pos_ref[0:1, :], pos_ref[1:2, :])
    hv = hidden(vc_ref, wvt_ref, wvb_ref, pos_ref[2:3, :], pos_ref[3:4, :])
    nh = w2k_ref.shape[0]
    for g in range(KV_GROUPS):
        k = _dot(hk[:, g * nh:(g + 1) * nh], w2k_ref[...])
        ss = jnp.sum(k * k, axis=-1, keepdims=True) * (1.0 / HEAD_DIM)
        ko_ref[0, g] = (k * lax.rsqrt(ss + EPS) * kg_ref[...]).astype(ko_ref.dtype)
        v = _dot(hv[:, g * nh:(g + 1) * nh], w2v_ref[...])
        vo_ref[0, g] = jnp.where(lo, v, 1.0).astype(vo_ref.dtype)


def _compress_call(kc_r, vc_r, wkt, wkb, wvt, wvb, pos4, w2k, w2v, kgain):
    bsz, ncp, width = kc_r.shape
    full = lambda a: pl.BlockSpec(a.shape, lambda b: (0,) * a.ndim)
    out = jax.ShapeDtypeStruct((bsz, KV_GROUPS, ncp, LANES), MXU_DTYPE)
    ospec = pl.BlockSpec((1, KV_GROUPS, ncp, LANES), lambda b: (b, 0, 0, 0))
    return pl.pallas_call(
        _compress_kernel,
        grid=(bsz,),
        in_specs=[pl.BlockSpec((1, ncp, width), lambda b: (b, 0, 0)),
                  pl.BlockSpec((1, ncp, width), lambda b: (b, 0, 0)),
                  full(wkt), full(wkb), full(wvt), full(wvb), full(pos4), full(w2k), full(w2v),
                  full(kgain)],
        out_specs=[ospec, ospec],
        out_shape=[out, out],
        compiler_params=pltpu.CompilerParams(
            dimension_semantics=("arbitrary",), vmem_limit_bytes=VMEM_LIMIT),
        name="nsa_compress",
    )(kc_r, vc_r, wkt, wkb, wvt, wvb, pos4, w2k, w2v, kgain)


def _lane_consts():
    lane = lax.broadcasted_iota(jnp.int32, (Q_TILE, LANES), 1)
    return lane, lane < HEAD_DIM, lane - HEAD_DIM


def _norm_queries(qblk, qgain_row, lo):
    out = []
    for r in range(REP):
        chunk = qblk[:, LANES * (r // 2):LANES * (r // 2 + 1)]
        if r % 2:
            chunk = pltpu.roll(chunk, HEAD_DIM, axis=1)
        qz = jnp.where(lo, chunk, 0.0)
        ss = jnp.sum(qz * qz, axis=-1, keepdims=True) * (1.0 / HEAD_DIM)
        out.append(qz * lax.rsqrt(ss + EPS) * qgain_row)
    return out


def _stack_q(qn, spare, lo):
    return jnp.concatenate([jnp.where(lo, q, spare) for q in qn], axis=0).astype(MXU_DTYPE)


def _tile(ref, jt):
    return ref[0, 0, pl.ds(pl.multiple_of(jt * Q_TILE, Q_TILE), Q_TILE), :]


def _slot_tile(i, d):
    return jnp.where(i >= d, i - d, i + 1)


def _range_mask(blk, first_tile, i):
    return jnp.where(blk < 2 * first_tile, NEG, jnp.where(blk > 2 * i + 1, NEG, 0.0))


def _pack_heads(outs, sz_ref, o_ref, lo):
    for c in range(REP // 2):
        pk = jnp.where(lo, outs[2 * c], pltpu.roll(outs[2 * c + 1], HEAD_DIM, axis=1))
        sz = sz_ref[0, :, LANES * c:LANES * (c + 1)].astype(F32)
        o_ref[0, :, LANES * c:LANES * (c + 1)] = (pk * sz).astype(o_ref.dtype)


def _normalise(acc, extra=None):
    den = pltpu.roll(acc, HEAD_DIM, axis=1)
    if extra is not None:
        den = den + extra
    value_lane = lax.broadcasted_iota(jnp.int32, acc.shape, 1) < HEAD_DIM
    return acc / jnp.where(value_lane, den, 1.0)


def _attn_a_kernel(sinks_ref, q_ref, k_ref, v_ref, sz_ref, nb_ref, qg_ref, o_ref):
    g = pl.program_id(1)
    i = pl.program_id(2)
    _, lo, blk = _lane_consts()
    qn = _norm_queries(q_ref[0].astype(F32), qg_ref[...], lo)
    q = _stack_q(qn, _range_mask(blk, i - 1, i), lo)
    tiles = [_slot_tile(i, 1), i]
    nb = nb_ref[...].reshape(ROWS, 2 * Q_TILE)
    s = [_dot_nt(q, _tile(k_ref, jt)) + nb[:, Q_TILE * n:Q_TILE * (n + 1)]
         for n, jt in enumerate(tiles)]
    rowmax = jnp.max(jnp.maximum(s[0], s[1]), axis=-1, keepdims=True)
    sink = jnp.concatenate(
        [jnp.full((Q_TILE, 1), sinks_ref[REP * g + r], F32) for r in range(REP)], axis=0)
    m = jnp.maximum(rowmax, sink)
    acc = sum(_dot(jnp.exp(sn - m).astype(MXU_DTYPE), _tile(v_ref, jt))
              for sn, jt in zip(s, tiles))
    o = _normalise(acc, extra=jnp.exp(sink - m))
    _pack_heads([o[r * Q_TILE:(r + 1) * Q_TILE] for r in range(REP)], sz_ref, o_ref, lo)


def _attn_a_call(sinks, qa, ka, va, sza, nba, qgain):
    bsz, s, _ = qa.shape
    nq = s // Q_TILE
    qspec = pl.BlockSpec((1, Q_TILE, 2 * LANES), lambda b, g, i: (b, i, g))
    kvspec = pl.BlockSpec((1, 1, s, LANES), lambda b, g, i: (b, g, 0, 0))
    return pl.pallas_call(
        _attn_a_kernel,
        grid=(bsz, KV_GROUPS, nq),
        in_specs=[pl.BlockSpec(memory_space=pltpu.SMEM),
                  qspec, kvspec, kvspec, qspec,
                  pl.BlockSpec((REP, Q_TILE, 2 * Q_TILE), lambda b, g, i: (g, 0, 0)),
                  pl.BlockSpec((1, LANES), lambda b, g, i: (0, 0))],
        out_specs=qspec,
        out_shape=jax.ShapeDtypeStruct((bsz, s, 512), MXU_DTYPE),
        compiler_params=pltpu.CompilerParams(
            dimension_semantics=("arbitrary", "arbitrary", "arbitrary"),
            vmem_limit_bytes=VMEM_LIMIT),
        name="attn_swa_sink",
    )(sinks, qa, ka, va, sza, nba, qgain)


def _attn_b_kernel(q_ref, kc_ref, vc_ref, ks_ref, vs_ref, kw_ref, vw_ref, sz_ref, gb_ref, nb_ref,
                   ovl_ref, qg_ref, o_ref, score_ref, *, topk):
    i = pl.program_id(2)
    lane, lo, blk = _lane_consts()
    qn = _norm_queries(q_ref[0].astype(F32), qg_ref[...], lo)
    nb = nb_ref[...].reshape(ROWS, 3 * Q_TILE)
    win_first = i - NSA_WINDOW // Q_TILE
    qw = _stack_q(qn, _range_mask(blk, win_first, i), lo)

    ncp = kc_ref.shape[2]
    sc = _dot_nt(qw, kc_ref[0, 0])
    n_io = lax.broadcasted_iota(jnp.int32, (ROWS, ncp), 1)
    tok = i * Q_TILE + (lax.broadcasted_iota(jnp.int32, (ROWS, ncp), 0) & (Q_TILE - 1))
    cmask = n_io * CMP_STRIDE + (CMP_LEN - 1) <= tok
    z = jnp.where(cmask, sc, NEG)
    m = jnp.max(z, axis=-1, keepdims=True)
    m = jnp.where(m > 0.5 * NEG, m, 0.0)
    e = jnp.where(cmask, jnp.exp(z - m), 0.0)
    p = e / jnp.maximum(jnp.sum(e, axis=-1, keepdims=True), 1e-30)
    o_cmp = _dot(p.astype(MXU_DTYPE), vc_ref[0, 0])
    psum = sum(p[r * Q_TILE:(r + 1) * Q_TILE] for r in range(REP))
    ovl = ovl_ref[...]
    imp = sum(_dot(piece, ovl) for piece in _split3(psum))

    tok1 = i * Q_TILE + lax.broadcasted_iota(jnp.int32, (Q_TILE, LANES), 0)
    cur = tok1 // SEL_LEN
    bonus = jnp.where(lane == 0, FORCE_BONUS,
                      jnp.where(lane == cur, FORCE_BONUS,
                                jnp.where(lane == cur - 1, FORCE_BONUS, 0.0)))
    score = jnp.where(lane <= cur, imp + bonus, NEG)
    score_ref[...] = score.T

    nslab = SEL_LEN // 8
    sub = lax.broadcasted_iota(jnp.int32, (8, LANES), 0)
    slabs = [score_ref[8 * v:8 * (v + 1), :] for v in range(nslab)]
    cnts = [jnp.zeros((8, LANES), F32) for _ in range(nslab)]
    for jp in range(SEL_LEN):
        rowv = score_ref[jp:jp + 1, :]
        for v in range(nslab):
            ge = jnp.where(rowv >= slabs[v], 1.0, 0.0)
            gt = jnp.where(rowv > slabs[v], 1.0, 0.0)
            if 8 * v > jp:
                beats = ge
            elif 8 * v + 7 < jp:
                beats = gt
            else:
                beats = jnp.where(sub + 8 * v > jp, ge, gt)
            cnts[v] = cnts[v] + beats
    cur_t = (i * Q_TILE + lax.broadcasted_iota(jnp.int32, (8, LANES), 1)) // SEL_LEN
    sel_t = [jnp.where(cnts[v] < topk, jnp.where(sub + 8 * v <= cur_t, 0.0, NEG), NEG)
             for v in range(nslab)]
    sel_t.append(jnp.full((LANES - SEL_LEN, LANES), NEG, F32))
    selm = pltpu.roll(jnp.concatenate(sel_t, axis=0).T, HEAD_DIM, axis=1)

    wslots = (4, 3, 2, 1, 0)
    wtiles = [_slot_tile(i, d) for d in wslots]
    sw = [_dot_nt(qw, _tile(kw_ref, jt)) for jt in wtiles]
    sw[0] = sw[0] + nb[:, 0:Q_TILE]
    sw[3] = sw[3] + nb[:, Q_TILE:2 * Q_TILE]
    sw[4] = sw[4] + nb[:, 2 * Q_TILE:3 * Q_TILE]
    mw = jnp.max(functools.reduce(jnp.maximum, sw), axis=-1, keepdims=True)
    acc_w = sum(_dot(jnp.exp(sn - mw).astype(MXU_DTYPE), _tile(vw_ref, jt))
                for sn, jt in zip(sw, wtiles))
    o_win = _normalise(acc_w)

    nfar = jnp.maximum(i - NSA_WINDOW // Q_TILE, 0) // (FAR_CHUNK // Q_TILE)
    far_blocks = nfar * (FAR_CHUNK // SEL_LEN)
    qs_far = _stack_q(qn, selm, lo)
    qs_near = _stack_q(qn, jnp.where(blk < far_blocks, NEG, selm), lo)
    sslots = (7, 6, 5, 4, 3, 2, 1, 0)
    stiles = [_slot_tile(i, d) for d in sslots]
    ss = [_dot_nt(qs_near, _tile(ks_ref, jt)) for jt in stiles]
    ss[6] = ss[6] + nb[:, Q_TILE:2 * Q_TILE]
    ss[7] = ss[7] + nb[:, 2 * Q_TILE:3 * Q_TILE]
    m_s = jnp.max(functools.reduce(jnp.maximum, ss), axis=-1, keepdims=True)
    acc_s = sum(_dot(jnp.exp(sn - m_s).astype(MXU_DTYPE), _tile(vs_ref, jt))
                for sn, jt in zip(ss, stiles))

    def far_body(c, carry):
        m_old, acc = carry
        st = pl.multiple_of(c * FAR_CHUNK, FAR_CHUNK)
        s_c = _dot_nt(qs_far, ks_ref[0, 0, pl.ds(st, FAR_CHUNK), :])
        m_new = jnp.maximum(m_old, jnp.max(s_c, axis=-1, keepdims=True))
        pv = _dot(jnp.exp(s_c - m_new).astype(MXU_DTYPE), vs_ref[0, 0, pl.ds(st, FAR_CHUNK), :])
        return m_new, jnp.exp(m_old - m_new) * acc + pv

    m_s, acc_s = lax.fori_loop(0, nfar, far_body, (m_s, acc_s))
    o_sel = _normalise(acc_s)

    gates = gb_ref[0, 0]
    outs = []
    for r in range(REP):
        rows = slice(r * Q_TILE, (r + 1) * Q_TILE)
        outs.append(gates[:, 3 * r:3 * r + 1] * o_cmp[rows]
                    + gates[:, 3 * r + 1:3 * r + 2] * o_sel[rows]
                    + gates[:, 3 * r + 2:3 * r + 3] * o_win[rows])
    _pack_heads(outs, sz_ref, o_ref, lo)


def _attn_b_call(qb, kcmp, vcmp, ks, vs, kw, vw, szb, gb, nbw, ovl, qgain):
    bsz, s, _ = qb.shape
    nq = s // Q_TILE
    ncp = kcmp.shape[2]
    qspec = pl.BlockSpec((1, Q_TILE, 2 * LANES), lambda b, g, i: (b, i, g))
    kvspec = pl.BlockSpec((1, 1, s, LANES), lambda b, g, i: (b, g, 0, 0))
    cspec = pl.BlockSpec((1, 1, ncp, LANES), lambda b, g, i: (b, g, 0, 0))
    return pl.pallas_call(
        functools.partial(_attn_b_kernel, topk=min(SEL_TOPK, s // SEL_LEN)),
        grid=(bsz, KV_GROUPS, nq),
        in_specs=[qspec, cspec, cspec, kvspec, kvspec, kvspec, kvspec, qspec,
                  pl.BlockSpec((1, 1, Q_TILE, LANES), lambda b, g, i: (b, g, i, 0)),
                  pl.BlockSpec((REP, Q_TILE, 3 * Q_TILE), lambda b, g, i: (g, 0, 0)),
                  pl.BlockSpec((ncp, LANES), lambda b, g, i: (0, 0)),
                  pl.BlockSpec((1, LANES), lambda b, g, i: (0, 0))],
        out_specs=qspec,
        out_shape=jax.ShapeDtypeStruct((bsz, s, 512), MXU_DTYPE),
        scratch_shapes=[pltpu.VMEM((LANES, LANES), F32)],
        compiler_params=pltpu.CompilerParams(
            dimension_semantics=("arbitrary", "arbitrary", "arbitrary"),
            vmem_limit_bytes=VMEM_LIMIT),
        name="attn_nsa",
    )(qb, kcmp, vcmp, ks, vs, kw, vw, szb, gb, nbw, ovl, qgain)


def _out_kernel(x_ref, mod_ref, ya_ref, yb_ref, w_ref, o_ref):
    half = ya_ref.shape[2]
    out = _dot(ya_ref[0], w_ref[0:half, :]) + _dot(yb_ref[0], w_ref[half:2 * half, :])
    o_ref[0] = x_ref[0] + mod_ref[0, 2:3, :] * out


def _out_call(x, mod3, ya, yb, w_out):
    bsz, s, d = x.shape
    tm = PROJ_TM
    xs = pl.BlockSpec((1, tm, d), lambda b, i: (b, i, 0))
    ys = pl.BlockSpec((1, tm, 512), lambda b, i: (b, i, 0))
    return pl.pallas_call(
        _out_kernel,
        grid=(bsz, s // tm),
        in_specs=[xs, pl.BlockSpec((1, 3, d), lambda b, i: (b, 0, 0)), ys, ys,
                  pl.BlockSpec(w_out.shape, lambda b, i: (0, 0))],
        out_specs=xs,
        out_shape=jax.ShapeDtypeStruct(x.shape, x.dtype),
        compiler_params=pltpu.CompilerParams(
            dimension_semantics=("arbitrary", "arbitrary"), vmem_limit_bytes=VMEM_LIMIT),
        name="out_proj_residual",
    )(x, mod3, ya, yb, w_out)


def _t5_bucket(dist):
    n = jnp.maximum(dist, 0)
    max_exact = N_BUCKETS // 2
    nf = jnp.maximum(n, 1).astype(F32)
    large = max_exact + (jnp.log(nf / max_exact) / math.log(MAX_DISTANCE / max_exact)
                         * (N_BUCKETS - max_exact)).astype(jnp.int32)
    large = jnp.minimum(large, N_BUCKETS - 1)
    return jnp.where(n < max_exact, n, large)


def _near_tables(rel_bias):
    tq = jnp.arange(Q_TILE)[:, None]
    col = jnp.arange(2 * Q_TILE)[None, :]
    dist = tq + Q_TILE - col
    bias = rel_bias[_t5_bucket(dist)]
    half = rel_bias.shape[1] // 2
    ok_a = ((dist >= 0) & (dist < SWA_WINDOW))[:, :, None]
    nba = jnp.where(ok_a, bias[:, :, :half], NEG).transpose(2, 0, 1)
    far = rel_bias[N_BUCKETS - 1, half:]
    near_b = jnp.where((dist >= 0)[:, :, None], bias[:, :, half:] - far, NEG)
    dist4 = tq + NSA_WINDOW - jnp.arange(Q_TILE)[None, :]
    edge_b = jnp.where((dist4 < NSA_WINDOW)[:, :, None],
                       rel_bias[_t5_bucket(dist4)][:, :, half:] - far, NEG)
    nbw = jnp.concatenate([edge_b, near_b], axis=1).transpose(2, 0, 1)
    return nba.astype(F32), nbw.astype(F32)


def _overlap_table(s, ncp):
    nc = (s - CMP_LEN) // CMP_STRIDE + 1
    ns = s // SEL_LEN
    c_lo = jnp.arange(ncp)[:, None] * CMP_STRIDE
    s_lo = jnp.arange(LANES)[None, :] * SEL_LEN
    ov = jnp.clip(jnp.minimum(c_lo + CMP_LEN, s_lo + SEL_LEN) - jnp.maximum(c_lo, s_lo), 0, None)
    ov = ov.astype(F32) / CMP_LEN
    ok = (jnp.arange(ncp)[:, None] < nc) & (jnp.arange(LANES)[None, :] < ns)
    return jnp.where(ok, ov, 0.0).astype(MXU_DTYPE)


def _compress_weights(w1, pos):
    hid = w1.shape[1]
    half = CMP_LEN // 2
    w1r = w1.reshape(CMP_LEN, HEAD_DIM, hid)
    eye = jnp.eye(KV_GROUPS, dtype=w1.dtype)
    expand = lambda w: jnp.einsum("ldj,gh->lgdhj", w, eye).reshape(
        half * KV_GROUPS * HEAD_DIM, KV_GROUPS * hid).astype(MXU_DTYPE)
    prow = lambda p: jnp.broadcast_to(p[:, None, :], (half, KV_GROUPS, HEAD_DIM)).reshape(1, -1)
    return expand(w1r[:half]), expand(w1r[half:]), prow(pos[:half]), prow(pos[half:])


def _upper_zero(row):
    return jnp.concatenate([row, jnp.zeros_like(row)]).reshape(1, LANES).astype(F32)


def _layer(x, c, w_ada, b_ada, norm_gain, w_in, b_nsa_gate, q_gain_a, k_gain_a, sinks, q_gain_b,
           k_gain_cmp, k_gain_sel, k_gain_win, cmp_pos_k, cmp_pos_v, w_cmp_k1, w_cmp_k2,
           w_cmp_v1, w_cmp_v2, w_out, rel_bias):
    bsz, s, d = x.shape
    assert s % FAR_CHUNK == 0 and s // SEL_LEN <= HEAD_DIM and s // Q_TILE >= 8
    assert w_in.shape == (d, D_PROJ) and s % PROJ_TM == 0
    qscale = HEAD_DIM ** -0.5

    mod3 = _mod_call(c, w_ada, b_ada).reshape(bsz, 3, d)
    w_in_p = jnp.pad(w_in, ((0, 0), (0, D_PROJ_PAD - D_PROJ))).astype(MXU_DTYPE)
    tile2 = lambda gn: jnp.concatenate([gn, gn]).astype(F32)
    kgains = jnp.zeros((8, LANES), F32).at[0].set(tile2(k_gain_a)).at[1].set(
        tile2(k_gain_sel)).at[2].set(tile2(k_gain_win))
    bgate = jnp.pad(b_nsa_gate, (0, LANES - b_nsa_gate.shape[0])).reshape(1, LANES).astype(F32)
    (qa, ka, va, sza, qb, kc, vc, ks, vs, kw, vw, szb, gb) = _proj_call(
        x, mod3, norm_gain.reshape(1, d).astype(F32), w_in_p, kgains, bgate)

    ncp = s // CMP_STRIDE
    wkt, wkb, pkt, pkb = _compress_weights(w_cmp_k1, cmp_pos_k)
    wvt, wvb, pvt, pvb = _compress_weights(w_cmp_v1, cmp_pos_v)
    pos4 = jnp.concatenate([pkt, pkb, pvt, pvb], axis=0).astype(F32)
    pad2 = lambda w: jnp.pad(w, ((0, 0), (0, LANES - HEAD_DIM))).astype(MXU_DTYPE)
    half_block = CMP_STRIDE * LANES
    kcmp, vcmp = _compress_call(kc.reshape(bsz, ncp, half_block), vc.reshape(bsz, ncp, half_block),
                                wkt, wkb, wvt, wvb, pos4, pad2(w_cmp_k2), pad2(w_cmp_v2),
                                _upper_zero(k_gain_cmp))

    nba, nbw = _near_tables(rel_bias.astype(F32))
    ya = _attn_a_call(sinks.astype(F32), qa, ka, va, sza, nba, _upper_zero(q_gain_a * qscale))
    yb = _attn_b_call(qb, kcmp, vcmp, ks, vs, kw, vw, szb, gb, nbw, _overlap_table(s, ncp),
                      _upper_zero(q_gain_b * qscale))
    return _out_call(x, mod3, ya, yb, w_out.astype(MXU_DTYPE))


def kernel(x, c, w_ada, b_ada, norm_gain, w_in, b_nsa_gate, q_gain_a, k_gain_a, sinks, q_gain_b,
           k_gain_cmp, k_gain_sel, k_gain_win, cmp_pos_k, cmp_pos_v, w_cmp_k1, w_cmp_k2,
           w_cmp_v1, w_cmp_v2, w_out, rel_bias):
    for l in range(w_ada.shape[0]):
        x = _layer(x, c, w_ada[l], b_ada[l], norm_gain[l], w_in[l], b_nsa_gate[l], q_gain_a[l],
                   k_gain_a[l], sinks[l], q_gain_b[l], k_gain_cmp[l], k_gain_sel[l],
                   k_gain_win[l], cmp_pos_k[l], cmp_pos_v[l], w_cmp_k1[l], w_cmp_k2[l],
                   w_cmp_v1[l], w_cmp_v2[l], w_out[l], rel_bias)
    return x
```

```python
import functools
import math

import jax
import jax.numpy as jnp
from jax import lax
from jax.experimental import pallas as pl
from jax.experimental.pallas import tpu as pltpu

MXU_DTYPE = jnp.bfloat16
F32 = jnp.float32

HEAD_DIM = 64
LANES = 128
Q_TILE = 128
KV_GROUPS = 2
REP = 4
ROWS = REP * Q_TILE
FAR_TILES = 4
SWA_WINDOW = 128
NSA_WINDOW = 512
WIN_TILES = NSA_WINDOW // Q_TILE
CMP_LEN = 32
CMP_STRIDE = 16
SEL_LEN = 64
SEL_TOPK = 16
N_BUCKETS = 32
MAX_DISTANCE = 128
FORCE_BONUS = 1e4
EPS = 1e-6
NEG = -1e30
PROJ_TM = 512
VMEM_LIMIT = 48 * 1024 * 1024

OFF_QA, OFF_KA, OFF_VA, OFF_ZA = 0, 512, 640, 768
OFF_QB, OFF_KC, OFF_VC, OFF_KS, OFF_VS, OFF_KW, OFF_VW, OFF_ZB, OFF_GB = (
    1280, 1792, 1920, 2048, 2176, 2304, 2432, 2560, 3072)
D_PROJ = 3096
D_PROJ_PAD = 3200


def _dot(a, b):
    return jnp.dot(a, b, preferred_element_type=F32)


def _dot_nt(a, b):
    return lax.dot_general(a, b, (((1,), (1,)), ((), ())), preferred_element_type=F32)


def _split3(x):
    hi = x.astype(MXU_DTYPE)
    r1 = x - hi.astype(F32)
    mid = r1.astype(MXU_DTYPE)
    lo = (r1 - mid.astype(F32)).astype(MXU_DTYPE)
    return hi, mid, lo


def _mod_kernel(c_ref, w_ref, b_ref, o_ref):
    sc = jax.nn.silu(c_ref[...])
    w = w_ref[...]
    acc = jnp.zeros(o_ref.shape, F32)
    for a in _split3(sc):
        for b in _split3(w)[:2]:
            acc = acc + _dot(a, b)
    o_ref[...] = acc + b_ref[...]


def _mod_call(c, w_ada, b_ada):
    bsz, d = c.shape
    n = w_ada.shape[1]
    tn = 512
    return pl.pallas_call(
        _mod_kernel,
        grid=(n // tn,),
        in_specs=[pl.BlockSpec((bsz, d), lambda j: (0, 0)),
                  pl.BlockSpec((d, tn), lambda j: (0, j)),
                  pl.BlockSpec((1, tn), lambda j: (0, j))],
        out_specs=pl.BlockSpec((bsz, tn), lambda j: (0, j)),
        out_shape=jax.ShapeDtypeStruct((bsz, n), F32),
        name="adaln_mod",
    )(c, w_ada, b_ada.reshape(1, n))


def _proj_kernel(x_ref, mod_ref, gain_ref, w_ref, kg_ref, bg_ref,
                 qa_ref, ka_ref, va_ref, sza_ref, qb_ref, kc_ref, vc_ref,
                 ks_ref, vs_ref, kw_ref, vw_ref, szb_ref, gb_ref):
    tm = x_ref.shape[1]
    si = pl.program_id(1)
    x = x_ref[0]
    ms = jnp.mean(x * x, axis=-1, keepdims=True)
    y = x * lax.rsqrt(ms + EPS) * gain_ref[...]
    h = y * (1.0 + mod_ref[0, 1:2, :]) + mod_ref[0, 0:1, :]
    hb = h.astype(MXU_DTYPE)

    def seg(off, n):
        return _dot(hb, w_ref[:, off:off + n])

    lane = lax.broadcasted_iota(jnp.int32, (tm, LANES), 1)
    lo = lane < HEAD_DIM
    row = lax.broadcasted_iota(jnp.int32, (tm, LANES), 0) + si * tm
    onehot = jnp.where(lane - HEAD_DIM == row // SEL_LEN, 1.0, 0.0)

    def half_norm(t, gain_row):
        sq = t * t
        s_lo = jnp.sum(jnp.where(lo, sq, 0.0), axis=-1, keepdims=True)
        s_hi = jnp.sum(jnp.where(lo, 0.0, sq), axis=-1, keepdims=True)
        inv = jnp.where(lo, lax.rsqrt(s_lo * (1.0 / HEAD_DIM) + EPS),
                        lax.rsqrt(s_hi * (1.0 / HEAD_DIM) + EPS))
        return t * inv * gain_row

    def split_heads(t, extra):
        return (jnp.where(lo, t, extra), jnp.where(lo, pltpu.roll(t, HEAD_DIM, axis=1), extra))

    def write_q(ref, off, gain_row):
        t = seg(off, REP * KV_GROUPS * HEAD_DIM)
        for c in range(REP * KV_GROUPS // 2):
            pair = split_heads(half_norm(t[:, LANES * c:LANES * (c + 1)], gain_row), 0.0)
            for j in range(2):
                col = LANES * (2 * c + j)
                ref[0, :, col:col + LANES] = pair[j].astype(ref.dtype)

    def write_k(ref, off, gain_row):
        for g, t in enumerate(split_heads(half_norm(seg(off, LANES), gain_row), onehot)):
            ref[0, g] = t.astype(ref.dtype)

    def write_vt(ref, off):
        for g, t in enumerate(split_heads(seg(off, LANES), 1.0)):
            tt = t.T
            for j in range(tm // Q_TILE):
                ref[0, g, j] = tt[:, Q_TILE * j:Q_TILE * (j + 1)].astype(ref.dtype)

    write_q(qa_ref, OFF_QA, kg_ref[3:4, :])
    write_k(ka_ref, OFF_KA, kg_ref[0:1, :])
    write_vt(va_ref, OFF_VA)
    sza_ref[0] = jax.nn.silu(seg(OFF_ZA, 512)).astype(sza_ref.dtype)
    write_q(qb_ref, OFF_QB, kg_ref[4:5, :])
    kc_ref[0] = seg(OFF_KC, LANES).astype(kc_ref.dtype)
    vc_ref[0] = seg(OFF_VC, LANES).astype(vc_ref.dtype)
    write_k(ks_ref, OFF_KS, kg_ref[1:2, :])
    write_vt(vs_ref, OFF_VS)
    write_k(kw_ref, OFF_KW, kg_ref[2:3, :])
    write_vt(vw_ref, OFF_VW)
    szb_ref[0] = jax.nn.silu(seg(OFF_ZB, 512)).astype(szb_ref.dtype)
    gates = jax.nn.sigmoid(seg(OFF_GB, LANES) + bg_ref[...])
    gb_ref[0, 0] = gates
    gb_ref[0, 1] = pltpu.roll(gates, LANES - REP * 3, axis=1)


def _proj_call(x, mod3, norm_gain, w_in_p, gains, bgate):
    bsz, s, d = x.shape
    tm = PROJ_TM
    dt = MXU_DTYPE
    nt = tm // Q_TILE
    qwidth = REP * KV_GROUPS * LANES
    rowq = pl.BlockSpec((1, tm, qwidth), lambda b, i: (b, i, 0))
    row512 = pl.BlockSpec((1, tm, 512), lambda b, i: (b, i, 0))
    row128 = pl.BlockSpec((1, tm, LANES), lambda b, i: (b, i, 0))
    grp = pl.BlockSpec((1, KV_GROUPS, tm, LANES), lambda b, i: (b, 0, i, 0))
    grpt = pl.BlockSpec((1, KV_GROUPS, nt, LANES, Q_TILE), lambda b, i: (b, 0, i, 0, 0))
    sq = jax.ShapeDtypeStruct((bsz, s, qwidth), dt)
    s512 = jax.ShapeDtypeStruct((bsz, s, 512), dt)
    s128 = jax.ShapeDtypeStruct((bsz, s, LANES), dt)
    sgrp = jax.ShapeDtypeStruct((bsz, KV_GROUPS, s, LANES), dt)
    sgrpt = jax.ShapeDtypeStruct((bsz, KV_GROUPS, s // Q_TILE, LANES, Q_TILE), dt)
    return pl.pallas_call(
        _proj_kernel,
        grid=(bsz, s // tm),
        in_specs=[pl.BlockSpec((1, tm, d), lambda b, i: (b, i, 0)),
                  pl.BlockSpec((1, 3, d), lambda b, i: (b, 0, 0)),
                  pl.BlockSpec((1, d), lambda b, i: (0, 0)),
                  pl.BlockSpec((d, D_PROJ_PAD), lambda b, i: (0, 0)),
                  pl.BlockSpec((8, LANES), lambda b, i: (0, 0)),
                  pl.BlockSpec((1, LANES), lambda b, i: (0, 0))],
        out_specs=[rowq, grp, grpt, row512, rowq, row128, row128, grp, grpt, grp, grpt, row512, grp],
        out_shape=[sq, sgrp, sgrpt, s512, sq, s128, s128, sgrp, sgrpt, sgrp, sgrpt, s512,
                   jax.ShapeDtypeStruct((bsz, KV_GROUPS, s, LANES), F32)],
        compiler_params=pltpu.CompilerParams(
            dimension_semantics=("arbitrary", "arbitrary"), vmem_limit_bytes=VMEM_LIMIT),
        name="norm_in_proj",
    )(x, mod3, norm_gain, w_in_p, gains, bgate)


def _compress_kernel(kc_ref, vc_ref, wkt_ref, wkb_ref, wvt_ref, wvb_ref, pos_ref, w2k_ref, w2v_ref,
                     kg_ref, ko_ref, vo_ref):
    ncp = kc_ref.shape[1]
    lane = lax.broadcasted_iota(jnp.int32, (ncp, LANES), 1)
    lo = lane < HEAD_DIM

    def hidden(h_ref, wt_ref, wb_ref, ptop, pbot):
        hf = h_ref[0].astype(F32)
        top = _dot((hf + ptop).astype(MXU_DTYPE), wt_ref[...])
        bot = _dot((hf + pbot).astype(MXU_DTYPE), wb_ref[...])
        pre = top + pltpu.roll(bot, ncp - 1, axis=0)
        return jax.nn.silu(pre).astype(MXU_DTYPE)

    hk = hidden(kc_ref, wkt_ref, wkb_ref, pos_ref[0:1, :], pos_ref[1:2, :])
    hv = hidden(vc_ref, wvt_ref, wvb_ref, pos_ref[2:3, :], pos_ref[3:4, :])
    nh = w2k_ref.shape[0]
    for g in range(KV_GROUPS):
        k = _dot(hk[:, g * nh:(g + 1) * nh], w2k_ref[...])
        ss = jnp.sum(k * k, axis=-1, keepdims=True) * (1.0 / HEAD_DIM)
        ko_ref[0, g] = (k * lax.rsqrt(ss + EPS) * kg_ref[...]).astype(ko_ref.dtype)
        v = _dot(hv[:, g * nh:(g + 1) * nh], w2v_ref[...])
        vo_ref[0, g] = jnp.where(lo, v, 1.0).T.astype(vo_ref.dtype)


def _compress_call(kc_r, vc_r, wkt, wkb, wvt, wvb, pos4, w2k, w2v, kgain):
    bsz, ncp, width = kc_r.shape
    full = lambda a: pl.BlockSpec(a.shape, lambda b: (0,) * a.ndim)
    return pl.pallas_call(
        _compress_kernel,
        grid=(bsz,),
        in_specs=[pl.BlockSpec((1, ncp, width), lambda b: (b, 0, 0)),
                  pl.BlockSpec((1, ncp, width), lambda b: (b, 0, 0)),
                  full(wkt), full(wkb), full(wvt), full(wvb), full(pos4), full(w2k), full(w2v),
                  full(kgain)],
        out_specs=[pl.BlockSpec((1, KV_GROUPS, ncp, LANES), lambda b: (b, 0, 0, 0)),
                   pl.BlockSpec((1, KV_GROUPS, LANES, ncp), lambda b: (b, 0, 0, 0))],
        out_shape=[jax.ShapeDtypeStruct((bsz, KV_GROUPS, ncp, LANES), MXU_DTYPE),
                   jax.ShapeDtypeStruct((bsz, KV_GROUPS, LANES, ncp), MXU_DTYPE)],
        compiler_params=pltpu.CompilerParams(
            dimension_semantics=("arbitrary",), vmem_limit_bytes=VMEM_LIMIT),
        name="nsa_compress",
    )(kc_r, vc_r, wkt, wkb, wvt, wvb, pos4, w2k, w2v, kgain)


def _stacked_queries(q_ref, g):
    return jnp.concatenate(
        [q_ref[0, :, LANES * (REP * g + r):LANES * (REP * g + r + 1)] for r in range(REP)], axis=0)


def _with_mask(qst, mask):
    lo = lax.broadcasted_iota(jnp.int32, (1, LANES), 1) < HEAD_DIM
    return jnp.where(lo, qst, mask.astype(qst.dtype))


def _range_mask(first_tile, last_tile):
    blk = lax.broadcasted_iota(jnp.int32, (1, LANES), 1) - HEAD_DIM
    return jnp.where(blk < 2 * first_tile, NEG, jnp.where(blk > 2 * last_tile + 1, NEG, 0.0))


def _k_window(ref, g, first_tile, ntiles):
    start = pl.multiple_of(jnp.maximum(first_tile, 0) * Q_TILE, Q_TILE)
    return ref[0, g, pl.ds(start, ntiles * Q_TILE), :]


def _vt_window(ref, g, first_tile, ntiles):
    start = jnp.maximum(first_tile, 0)
    return jnp.concatenate([ref[0, g, start + t] for t in range(ntiles)], axis=1)


def _near_table(nb, first):
    diag = nb[Q_TILE:]
    return jnp.concatenate([jnp.where(first, diag, nb[:Q_TILE]), diag], axis=0)


def _colmax(s):
    return jnp.max(s, axis=0, keepdims=True)


def _probs(s, m):
    return jnp.exp(s - m).astype(MXU_DTYPE)


def _values(acc, extra=None):
    den = acc[HEAD_DIM:HEAD_DIM + 1]
    if extra is not None:
        den = den + extra
    return acc[:HEAD_DIM] / den


def _store_heads(o_t, sz_ref, o_ref, g):
    for c in range(REP // 2):
        pair = jnp.concatenate([o_t[:, Q_TILE * (2 * c):Q_TILE * (2 * c + 1)],
                                o_t[:, Q_TILE * (2 * c + 1):Q_TILE * (2 * c + 2)]], axis=0)
        cols = slice(LANES * (2 * g + c), LANES * (2 * g + c + 1))
        o_ref[0, :, cols] = (pair.T * sz_ref[0, :, cols].astype(F32)).astype(o_ref.dtype)


def _attn_a_kernel(sinks_ref, q_ref, k_ref, vt_ref, sz_ref, nb_ref, o_ref):
    i = pl.program_id(1)
    near_mask = _range_mask(i - 1, i)
    for g in range(KV_GROUPS):
        q = _with_mask(_stacked_queries(q_ref, g), near_mask)
        nb = _near_table(nb_ref[g], i == 0)
        s = _dot_nt(_k_window(k_ref, g, i - 1, 2), q) + nb
        sink = jnp.concatenate(
            [jnp.full((1, Q_TILE), sinks_ref[REP * g + r], F32) for r in range(REP)], axis=1)
        m = jnp.maximum(_colmax(s), sink)
        acc = _dot(_vt_window(vt_ref, g, i - 1, 2), _probs(s, m))
        _store_heads(_values(acc, extra=jnp.exp(sink - m)), sz_ref, o_ref, g)


def _attn_a_call(sinks, qa, ka, vta, sza, nba):
    bsz, s, qwidth = qa.shape
    owidth = sza.shape[2]
    full = lambda a: pl.BlockSpec((1,) + a.shape[1:], lambda b, i: (b,) + (0,) * (a.ndim - 1))
    return pl.pallas_call(
        _attn_a_kernel,
        grid=(bsz, s // Q_TILE),
        in_specs=[pl.BlockSpec(memory_space=pltpu.SMEM),
                  pl.BlockSpec((1, Q_TILE, qwidth), lambda b, i: (b, i, 0)),
                  full(ka), full(vta),
                  pl.BlockSpec((1, Q_TILE, owidth), lambda b, i: (b, i, 0)),
                  pl.BlockSpec(nba.shape, lambda b, i: (0, 0, 0))],
        out_specs=pl.BlockSpec((1, Q_TILE, owidth), lambda b, i: (b, i, 0)),
        out_shape=jax.ShapeDtypeStruct((bsz, s, owidth), MXU_DTYPE),
        compiler_params=pltpu.CompilerParams(
            dimension_semantics=("arbitrary", "arbitrary"), vmem_limit_bytes=VMEM_LIMIT),
        name="attn_swa_sink",
    )(sinks, qa, ka, vta, sza, nba)


def _nsa_front(g, i, q_ref, kc_ref, vct_ref, ks_ref, vst_ref, kw_ref, vwt_ref, nb_ref, ovl_ref,
               score_ref, topk):
    qst = _stacked_queries(q_ref, g)
    q_near = _with_mask(qst, _range_mask(i - 1, i))
    nb = nb_ref[g]
    nb_near = _near_table(nb[Q_TILE:], i == 0)

    ncp = kc_ref.shape[2]
    sc = _dot_nt(kc_ref[0, g], q_near)
    n_io = lax.broadcasted_iota(jnp.int32, (ncp, ROWS), 0)
    tok = i * Q_TILE + (lax.broadcasted_iota(jnp.int32, (ncp, ROWS), 1) & (Q_TILE - 1))
    cmask = n_io * CMP_STRIDE + (CMP_LEN - 1) <= tok
    z = jnp.where(cmask, sc, NEG)
    m = _colmax(z)
    m = jnp.where(m > 0.5 * NEG, m, 0.0)
    e = jnp.where(cmask, jnp.exp(z - m), 0.0)
    p = e / jnp.maximum(jnp.sum(e, axis=0, keepdims=True), 1e-30)
    o_cmp = _dot(vct_ref[0, g], p.astype(MXU_DTYPE))[:HEAD_DIM]
    psum = sum(p[:, r * Q_TILE:(r + 1) * Q_TILE] for r in range(REP))
    ovl = ovl_ref[...]
    imp = sum(_dot(ovl, piece) for piece in _split3(psum))

    blk = lax.broadcasted_iota(jnp.int32, (LANES, Q_TILE), 0)
    cur = (i * Q_TILE + lax.broadcasted_iota(jnp.int32, (LANES, Q_TILE), 1)) // SEL_LEN
    bonus = jnp.where(blk == 0, FORCE_BONUS,
                      jnp.where(blk == cur, FORCE_BONUS,
                                jnp.where(blk == cur - 1, FORCE_BONUS, 0.0)))
    score_ref[g] = jnp.where(blk <= cur, imp + bonus, NEG)

    nslab = SEL_LEN // 8
    sub = lax.broadcasted_iota(jnp.int32, (8, Q_TILE), 0)
    slabs = [score_ref[g, 8 * v:8 * (v + 1), :] for v in range(nslab)]
    cnts = [jnp.zeros((8, Q_TILE), F32) for _ in range(nslab)]
    for jp in range(SEL_LEN):
        rowv = score_ref[g, jp:jp + 1, :]
        for v in range(nslab):
            ge = jnp.where(rowv >= slabs[v], 1.0, 0.0)
            gt = jnp.where(rowv > slabs[v], 1.0, 0.0)
            if 8 * v > jp:
                beats = ge
            elif 8 * v + 7 < jp:
                beats = gt
            else:
                beats = jnp.where(sub + 8 * v > jp, ge, gt)
            cnts[v] = cnts[v] + beats
    cur_t = cur[:8]
    sel_t = [jnp.where(cnts[v] < topk, jnp.where(sub + 8 * v <= cur_t, 0.0, NEG), NEG)
             for v in range(nslab)]
    sel_t = jnp.concatenate([jnp.full((LANES - SEL_LEN, Q_TILE), NEG, F32)] + sel_t, axis=0)
    selm = jnp.concatenate([sel_t.T] * REP, axis=0)

    q_wfar = _with_mask(qst, _range_mask(i - WIN_TILES, i - 2))
    s_wf = _dot_nt(_k_window(kw_ref, g, i - WIN_TILES, WIN_TILES - 1), q_wfar)
    edge = jnp.where(i >= WIN_TILES, nb[:Q_TILE], 0.0)
    s_wf = jnp.concatenate([s_wf[:Q_TILE] + edge, s_wf[Q_TILE:]], axis=0)
    s_wn = _dot_nt(_k_window(kw_ref, g, i - 1, 2), q_near) + nb_near
    mw = jnp.maximum(_colmax(s_wf), _colmax(s_wn))
    acc_w = (_dot(_vt_window(vwt_ref, g, i - WIN_TILES, WIN_TILES - 1), _probs(s_wf, mw))
             + _dot(_vt_window(vwt_ref, g, i - 1, 2), _probs(s_wn, mw)))
    o_win = _values(acc_w)

    s_sn = _dot_nt(_k_window(ks_ref, g, i - 1, 2), _with_mask(qst, selm)) + nb_near
    m_s = _colmax(s_sn)
    acc_s = _dot(_vt_window(vst_ref, g, i - 1, 2), _probs(s_sn, m_s))
    blk_row = lax.broadcasted_iota(jnp.int32, (1, LANES), 1) - HEAD_DIM
    qs_far = _with_mask(qst, jnp.where(blk_row >= 2 * (i - 1), NEG, selm))
    return o_cmp, o_win, m_s, acc_s, qs_far


def _attn_b_kernel(q_ref, kc_ref, vct_ref, ks_ref, vst_ref, kw_ref, vwt_ref, sz_ref, gb_ref, nb_ref,
                   ovl_ref, o_ref, score_ref, *, topk):
    i = pl.program_id(1)
    front = [_nsa_front(g, i, q_ref, kc_ref, vct_ref, ks_ref, vst_ref, kw_ref, vwt_ref, nb_ref,
                        ovl_ref, score_ref, topk) for g in range(KV_GROUPS)]
    qs_far = [f[4] for f in front]

    def far_body(c, carry):
        s_c = [_dot_nt(_k_window(ks_ref, g, c * FAR_TILES, FAR_TILES), qs_far[g])
               for g in range(KV_GROUPS)]
        out = []
        for g in range(KV_GROUPS):
            m_old, acc = carry[2 * g], carry[2 * g + 1]
            m_new = jnp.maximum(m_old, _colmax(s_c[g]))
            pv = _dot(_vt_window(vst_ref, g, c * FAR_TILES, FAR_TILES), _probs(s_c[g], m_new))
            out += [m_new, jnp.exp(m_old - m_new) * acc + pv]
        return tuple(out)

    nfar = (i + FAR_TILES - 2) // FAR_TILES
    far = lax.fori_loop(0, nfar, far_body, tuple(v for f in front for v in f[2:4]))

    for g in range(KV_GROUPS):
        o_cmp, o_win = front[g][0], front[g][1]
        o_sel = _values(far[2 * g + 1])
        gates = gb_ref[0, g].T
        mix = []
        for r in range(REP):
            cols = slice(r * Q_TILE, (r + 1) * Q_TILE)
            mix.append(gates[3 * r:3 * r + 1] * o_cmp[:, cols]
                       + gates[3 * r + 1:3 * r + 2] * o_sel[:, cols]
                       + gates[3 * r + 2:3 * r + 3] * o_win[:, cols])
        _store_heads(jnp.concatenate(mix, axis=1), sz_ref, o_ref, g)


def _attn_b_call(qb, kcmp, vcmpt, ks, vst, kw, vwt, szb, gb, nbw, ovl):
    bsz, s, qwidth = qb.shape
    owidth = szb.shape[2]
    full = lambda a: pl.BlockSpec((1,) + a.shape[1:], lambda b, i: (b,) + (0,) * (a.ndim - 1))
    return pl.pallas_call(
        functools.partial(_attn_b_kernel, topk=min(SEL_TOPK, s // SEL_LEN)),
        grid=(bsz, s // Q_TILE),
        in_specs=[pl.BlockSpec((1, Q_TILE, qwidth), lambda b, i: (b, i, 0)),
                  full(kcmp), full(vcmpt), full(ks), full(vst), full(kw), full(vwt),
                  pl.BlockSpec((1, Q_TILE, owidth), lambda b, i: (b, i, 0)),
                  pl.BlockSpec((1, KV_GROUPS, Q_TILE, LANES), lambda b, i: (b, 0, i, 0)),
                  pl.BlockSpec(nbw.shape, lambda b, i: (0, 0, 0)),
                  pl.BlockSpec(ovl.shape, lambda b, i: (0, 0))],
        out_specs=pl.BlockSpec((1, Q_TILE, owidth), lambda b, i: (b, i, 0)),
        out_shape=jax.ShapeDtypeStruct((bsz, s, owidth), MXU_DTYPE),
        scratch_shapes=[pltpu.VMEM((KV_GROUPS, LANES, Q_TILE), F32)],
        compiler_params=pltpu.CompilerParams(
            dimension_semantics=("arbitrary", "arbitrary"), vmem_limit_bytes=VMEM_LIMIT),
        name="attn_nsa",
    )(qb, kcmp, vcmpt, ks, vst, kw, vwt, szb, gb, nbw, ovl)


def _out_kernel(x_ref, mod_ref, ya_ref, yb_ref, w_ref, o_ref):
    half = ya_ref.shape[2]
    out = _dot(ya_ref[0], w_ref[0:half, :]) + _dot(yb_ref[0], w_ref[half:2 * half, :])
    o_ref[0] = x_ref[0] + mod_ref[0, 2:3, :] * out


def _out_call(x, mod3, ya, yb, w_out):
    bsz, s, d = x.shape
    tm = PROJ_TM
    xs = pl.BlockSpec((1, tm, d), lambda b, i: (b, i, 0))
    ys = pl.BlockSpec((1, tm, 512), lambda b, i: (b, i, 0))
    return pl.pallas_call(
        _out_kernel,
        grid=(bsz, s // tm),
        in_specs=[xs, pl.BlockSpec((1, 3, d), lambda b, i: (b, 0, 0)), ys, ys,
                  pl.BlockSpec(w_out.shape, lambda b, i: (0, 0))],
        out_specs=xs,
        out_shape=jax.ShapeDtypeStruct(x.shape, x.dtype),
        compiler_params=pltpu.CompilerParams(
            dimension_semantics=("arbitrary", "arbitrary"), vmem_limit_bytes=VMEM_LIMIT),
        name="out_proj_residual",
    )(x, mod3, ya, yb, w_out)


def _t5_bucket(dist):
    n = jnp.maximum(dist, 0)
    max_exact = N_BUCKETS // 2
    nf = jnp.maximum(n, 1).astype(F32)
    large = max_exact + (jnp.log(nf / max_exact) / math.log(MAX_DISTANCE / max_exact)
                         * (N_BUCKETS - max_exact)).astype(jnp.int32)
    large = jnp.minimum(large, N_BUCKETS - 1)
    return jnp.where(n < max_exact, n, large)


def _bias_table_kernel(rel_ref, idx_ref, idx_edge_ref, nba_ref, nbw_ref):
    h = pl.program_id(0)
    hb = h + pl.num_programs(0)

    def lookup(idx, head):
        acc = jnp.zeros(idx.shape, F32)
        for b in range(N_BUCKETS):
            acc = jnp.where(idx == b, rel_ref[b, head], acc)
        return acc

    idx = idx_ref[...]
    dist = (lax.broadcasted_iota(jnp.int32, idx.shape, 1) + Q_TILE
            - lax.broadcasted_iota(jnp.int32, idx.shape, 0))
    causal = dist >= 0
    far = rel_ref[N_BUCKETS - 1, hb]
    nba_ref[0] = jnp.where(causal, jnp.where(dist < SWA_WINDOW, lookup(idx, h), NEG), NEG)
    near_b = jnp.where(causal, lookup(idx, hb) - far, NEG)
    idx_e = idx_edge_ref[...]
    dist_e = (lax.broadcasted_iota(jnp.int32, idx_e.shape, 1) + NSA_WINDOW
              - lax.broadcasted_iota(jnp.int32, idx_e.shape, 0))
    edge_b = jnp.where(dist_e < NSA_WINDOW, lookup(idx_e, hb) - far, NEG)
    nbw_ref[0] = jnp.concatenate([edge_b, near_b], axis=0)


def _near_tables(rel_bias):
    nheads = rel_bias.shape[1] // 2
    tq = jnp.arange(Q_TILE)[None, :]
    idx = _t5_bucket(tq + Q_TILE - jnp.arange(2 * Q_TILE)[:, None]).astype(jnp.int32)
    idx_edge = _t5_bucket(tq + NSA_WINDOW - jnp.arange(Q_TILE)[:, None]).astype(jnp.int32)
    return pl.pallas_call(
        _bias_table_kernel,
        grid=(nheads,),
        in_specs=[pl.BlockSpec(memory_space=pltpu.SMEM),
                  pl.BlockSpec(idx.shape, lambda h: (0, 0)),
                  pl.BlockSpec(idx_edge.shape, lambda h: (0, 0))],
        out_specs=[pl.BlockSpec((1, 2 * Q_TILE, Q_TILE), lambda h: (h // REP, 0, h % REP)),
                   pl.BlockSpec((1, 3 * Q_TILE, Q_TILE), lambda h: (h // REP, 0, h % REP))],
        out_shape=[jax.ShapeDtypeStruct((nheads // REP, 2 * Q_TILE, ROWS), F32),
                   jax.ShapeDtypeStruct((nheads // REP, 3 * Q_TILE, ROWS), F32)],
        name="t5_bias_tables",
    )(rel_bias, idx, idx_edge)


def _overlap_table(s, ncp):
    nc = (s - CMP_LEN) // CMP_STRIDE + 1
    ns = s // SEL_LEN
    c_lo = jnp.arange(ncp)[None, :] * CMP_STRIDE
    s_lo = jnp.arange(LANES)[:, None] * SEL_LEN
    ov = jnp.clip(jnp.minimum(c_lo + CMP_LEN, s_lo + SEL_LEN) - jnp.maximum(c_lo, s_lo), 0, None)
    ov = ov.astype(F32) / CMP_LEN
    ok = (jnp.arange(ncp)[None, :] < nc) & (jnp.arange(LANES)[:, None] < ns)
    return jnp.where(ok, ov, 0.0).astype(MXU_DTYPE)


def _compress_weights(w1, pos):
    hid = w1.shape[1]
    half = CMP_LEN // 2
    w1r = w1.reshape(CMP_LEN, HEAD_DIM, hid)
    eye = jnp.eye(KV_GROUPS, dtype=w1.dtype)
    expand = lambda w: jnp.einsum("ldj,gh->lgdhj", w, eye).reshape(
        half * KV_GROUPS * HEAD_DIM, KV_GROUPS * hid).astype(MXU_DTYPE)
    prow = lambda p: jnp.broadcast_to(p[:, None, :], (half, KV_GROUPS, HEAD_DIM)).reshape(1, -1)
    return expand(w1r[:half]), expand(w1r[half:]), prow(pos[:half]), prow(pos[half:])


def _upper_zero(row):
    return jnp.concatenate([row, jnp.zeros_like(row)]).reshape(1, LANES).astype(F32)


def _layer(x, c, w_ada, b_ada, norm_gain, w_in, b_nsa_gate, q_gain_a, k_gain_a, sinks, q_gain_b,
           k_gain_cmp, k_gain_sel, k_gain_win, cmp_pos_k, cmp_pos_v, w_cmp_k1, w_cmp_k2,
           w_cmp_v1, w_cmp_v2, w_out, rel_bias):
    bsz, s, d = x.shape
    assert s % (FAR_TILES * Q_TILE) == 0 and s // SEL_LEN <= HEAD_DIM and s // Q_TILE >= WIN_TILES
    assert w_in.shape == (d, D_PROJ) and s % PROJ_TM == 0
    qscale = HEAD_DIM ** -0.5

    mod3 = _mod_call(c, w_ada, b_ada).reshape(bsz, 3, d)
    w_in_p = jnp.pad(w_in, ((0, 0), (0, D_PROJ_PAD - D_PROJ))).astype(MXU_DTYPE)
    tile2 = lambda gn: jnp.concatenate([gn, gn]).astype(F32)
    gains = jnp.zeros((8, LANES), F32)
    for n, gn in enumerate((k_gain_a, k_gain_sel, k_gain_win, q_gain_a * qscale, q_gain_b * qscale)):
        gains = gains.at[n].set(tile2(gn))
    bgate = jnp.pad(b_nsa_gate, (0, LANES - b_nsa_gate.shape[0])).reshape(1, LANES).astype(F32)
    (qa, ka, vta, sza, qb, kc, vc, ks, vst, kw, vwt, szb, gb) = _proj_call(
        x, mod3, norm_gain.reshape(1, d).astype(F32), w_in_p, gains, bgate)

    ncp = s // CMP_STRIDE
    wkt, wkb, pkt, pkb = _compress_weights(w_cmp_k1, cmp_pos_k)
    wvt, wvb, pvt, pvb = _compress_weights(w_cmp_v1, cmp_pos_v)
    pos4 = jnp.concatenate([pkt, pkb, pvt, pvb], axis=0).astype(F32)
    pad2 = lambda w: jnp.pad(w, ((0, 0), (0, LANES - HEAD_DIM))).astype(MXU_DTYPE)
    half_block = CMP_STRIDE * LANES
    kcmp, vcmpt = _compress_call(kc.reshape(bsz, ncp, half_block), vc.reshape(bsz, ncp, half_block),
                                 wkt, wkb, wvt, wvb, pos4, pad2(w_cmp_k2), pad2(w_cmp_v2),
                                 _upper_zero(k_gain_cmp))

    nba, nbw = _near_tables(rel_bias.astype(F32))
    ya = _attn_a_call(sinks.astype(F32), qa, ka, vta, sza, nba)
    yb = _attn_b_call(qb, kcmp, vcmpt, ks, vst, kw, vwt, szb, gb, nbw, _overlap_table(s, ncp))
    return _out_call(x, mod3, ya, yb, w_out.astype(MXU_DTYPE))


def kernel(x, c, w_ada, b_ada, norm_gain, w_in, b_nsa_gate, q_gain_a, k_gain_a, sinks, q_gain_b,
           k_gain_cmp, k_gain_sel, k_gain_win, cmp_pos_k, cmp_pos_v, w_cmp_k1, w_cmp_k2,
           w_cmp_v1, w_cmp_v2, w_out, rel_bias):
    for l in range(w_ada.shape[0]):
        x = _layer(x, c, w_ada[l], b_ada[l], norm_gain[l], w_in[l], b_nsa_gate[l], q_gain_a[l],
                   k_gain_a[l], sinks[l], q_gain_b[l], k_gain_cmp[l], k_gain_sel[l],
                   k_gain_win[l], cmp_pos_k[l], cmp_pos_v[l], w_cmp_k1[l], w_cmp_k2[l],
                   w_cmp_v1[l], w_cmp_v2[l], w_out[l], rel_bias)
    return x
```

```python
import functools
import math

import jax
import jax.numpy as jnp
from jax import lax
from jax.experimental import pallas as pl
from jax.experimental.pallas import tpu as pltpu

MXU_DTYPE = jnp.bfloat16
F32 = jnp.float32

HEAD_DIM = 64
LANES = 128
Q_TILE = 128
KV_GROUPS = 2
REP = 4
ROWS = REP * Q_TILE
FAR_TILES = 4
SWA_WINDOW = 128
NSA_WINDOW = 512
WIN_TILES = NSA_WINDOW // Q_TILE
CMP_LEN = 32
CMP_STRIDE = 16
SEL_LEN = 64
SEL_TOPK = 16
N_BUCKETS = 32
MAX_DISTANCE = 128
FORCE_BONUS = 1e4
EPS = 1e-6
NEG = -1e30
LOG2E = 1.4426950408889634
PROJ_TM = 512
VMEM_LIMIT = 48 * 1024 * 1024

OFF_QA, OFF_KA, OFF_VA, OFF_ZA = 0, 512, 640, 768
OFF_QB, OFF_KC, OFF_VC, OFF_KS, OFF_VS, OFF_KW, OFF_VW, OFF_ZB, OFF_GB = (
    1280, 1792, 1920, 2048, 2176, 2304, 2432, 2560, 3072)
D_PROJ = 3096
D_PROJ_PAD = 3200


def _dot(a, b):
    return jnp.dot(a, b, preferred_element_type=F32)


def _dot_nt(a, b):
    return lax.dot_general(a, b, (((1,), (1,)), ((), ())), preferred_element_type=F32)


def _split3(x):
    hi = x.astype(MXU_DTYPE)
    r1 = x - hi.astype(F32)
    mid = r1.astype(MXU_DTYPE)
    lo = (r1 - mid.astype(F32)).astype(MXU_DTYPE)
    return hi, mid, lo


def _mod_kernel(c_ref, w_ref, b_ref, o_ref):
    sc = jax.nn.silu(c_ref[...])
    w = w_ref[...]
    acc = jnp.zeros(o_ref.shape, F32)
    for a in _split3(sc):
        for b in _split3(w)[:2]:
            acc = acc + _dot(a, b)
    o_ref[...] = acc + b_ref[...]


def _mod_call(c, w_ada, b_ada):
    bsz, d = c.shape
    n = w_ada.shape[1]
    tn = 512
    return pl.pallas_call(
        _mod_kernel,
        grid=(n // tn,),
        in_specs=[pl.BlockSpec((bsz, d), lambda j: (0, 0)),
                  pl.BlockSpec((d, tn), lambda j: (0, j)),
                  pl.BlockSpec((1, tn), lambda j: (0, j))],
        out_specs=pl.BlockSpec((bsz, tn), lambda j: (0, j)),
        out_shape=jax.ShapeDtypeStruct((bsz, n), F32),
        name="adaln_mod",
    )(c, w_ada, b_ada.reshape(1, n))


def _proj_kernel(x_ref, mod_ref, gain_ref, w_ref, kg_ref, bg_ref,
                 qa_ref, ka_ref, va_ref, sza_ref, qb_ref, kc_ref, vc_ref,
                 ks_ref, vs_ref, kw_ref, vw_ref, szb_ref, gb_ref):
    tm = x_ref.shape[1]
    si = pl.program_id(1)
    x = x_ref[0]
    ms = jnp.mean(x * x, axis=-1, keepdims=True)
    y = x * lax.rsqrt(ms + EPS) * gain_ref[...]
    h = y * (1.0 + mod_ref[0, 1:2, :]) + mod_ref[0, 0:1, :]
    hb = h.astype(MXU_DTYPE)

    def seg(off, n):
        return _dot(hb, w_ref[:, off:off + n])

    lane = lax.broadcasted_iota(jnp.int32, (tm, LANES), 1)
    lo = lane < HEAD_DIM
    row = lax.broadcasted_iota(jnp.int32, (tm, LANES), 0) + si * tm
    onehot = jnp.where(lane - HEAD_DIM == row // SEL_LEN, 1.0, 0.0)

    def half_norm(t, gain_row):
        sq = t * t
        s_lo = jnp.sum(jnp.where(lo, sq, 0.0), axis=-1, keepdims=True)
        s_hi = jnp.sum(jnp.where(lo, 0.0, sq), axis=-1, keepdims=True)
        inv = jnp.where(lo, lax.rsqrt(s_lo * (1.0 / HEAD_DIM) + EPS),
                        lax.rsqrt(s_hi * (1.0 / HEAD_DIM) + EPS))
        return t * inv * gain_row

    def split_heads(t, extra):
        return (jnp.where(lo, t, extra), jnp.where(lo, pltpu.roll(t, HEAD_DIM, axis=1), extra))

    def write_q(ref, off, gain_row):
        t = seg(off, REP * KV_GROUPS * HEAD_DIM)
        for c in range(REP * KV_GROUPS // 2):
            pair = split_heads(half_norm(t[:, LANES * c:LANES * (c + 1)], gain_row), 0.0)
            for j in range(2):
                col = LANES * (2 * c + j)
                ref[0, :, col:col + LANES] = pair[j].astype(ref.dtype)

    def write_kv(k_ref, vt_ref, off, gain_row):
        kv = seg(off, 2 * LANES)
        for g, t in enumerate(split_heads(half_norm(kv[:, :LANES], gain_row), onehot)):
            k_ref[0, g] = t.astype(k_ref.dtype)
        for g, t in enumerate(split_heads(kv[:, LANES:], 1.0)):
            tt = t.T
            for j in range(tm // Q_TILE):
                vt_ref[0, g, j] = tt[:, Q_TILE * j:Q_TILE * (j + 1)].astype(vt_ref.dtype)

    write_q(qa_ref, OFF_QA, kg_ref[3:4, :])
    write_kv(ka_ref, va_ref, OFF_KA, kg_ref[0:1, :])
    sza_ref[0] = jax.nn.silu(seg(OFF_ZA, 512)).astype(sza_ref.dtype)
    write_q(qb_ref, OFF_QB, kg_ref[4:5, :])
    kvc = seg(OFF_KC, 2 * LANES)
    kc_ref[0] = kvc[:, :LANES].astype(kc_ref.dtype)
    vc_ref[0] = kvc[:, LANES:].astype(vc_ref.dtype)
    write_kv(ks_ref, vs_ref, OFF_KS, kg_ref[1:2, :])
    write_kv(kw_ref, vw_ref, OFF_KW, kg_ref[2:3, :])
    szb_ref[0] = jax.nn.silu(seg(OFF_ZB, 512)).astype(szb_ref.dtype)
    gates = jax.nn.sigmoid(seg(OFF_GB, LANES) + bg_ref[...])
    gb_ref[0, 0] = gates
    gb_ref[0, 1] = pltpu.roll(gates, LANES - REP * 3, axis=1)


def _proj_call(x, mod3, norm_gain, w_in_p, gains, bgate):
    bsz, s, d = x.shape
    tm = PROJ_TM
    dt = MXU_DTYPE
    nt = tm // Q_TILE
    qwidth = REP * KV_GROUPS * LANES
    rowq = pl.BlockSpec((1, tm, qwidth), lambda b, i: (b, i, 0))
    row512 = pl.BlockSpec((1, tm, 512), lambda b, i: (b, i, 0))
    row128 = pl.BlockSpec((1, tm, LANES), lambda b, i: (b, i, 0))
    grp = pl.BlockSpec((1, KV_GROUPS, tm, LANES), lambda b, i: (b, 0, i, 0))
    grpt = pl.BlockSpec((1, KV_GROUPS, nt, LANES, Q_TILE), lambda b, i: (b, 0, i, 0, 0))
    sq = jax.ShapeDtypeStruct((bsz, s, qwidth), dt)
    s512 = jax.ShapeDtypeStruct((bsz, s, 512), dt)
    s128 = jax.ShapeDtypeStruct((bsz, s, LANES), dt)
    sgrp = jax.ShapeDtypeStruct((bsz, KV_GROUPS, s, LANES), dt)
    sgrpt = jax.ShapeDtypeStruct((bsz, KV_GROUPS, s // Q_TILE, LANES, Q_TILE), dt)
    return pl.pallas_call(
        _proj_kernel,
        grid=(bsz, s // tm),
        in_specs=[pl.BlockSpec((1, tm, d), lambda b, i: (b, i, 0)),
                  pl.BlockSpec((1, 3, d), lambda b, i: (b, 0, 0)),
                  pl.BlockSpec((1, d), lambda b, i: (0, 0)),
                  pl.BlockSpec((d, D_PROJ_PAD), lambda b, i: (0, 0)),
                  pl.BlockSpec((8, LANES), lambda b, i: (0, 0)),
                  pl.BlockSpec((1, LANES), lambda b, i: (0, 0))],
        out_specs=[rowq, grp, grpt, row512, rowq, row128, row128, grp, grpt, grp, grpt, row512, grp],
        out_shape=[sq, sgrp, sgrpt, s512, sq, s128, s128, sgrp, sgrpt, sgrp, sgrpt, s512,
                   jax.ShapeDtypeStruct((bsz, KV_GROUPS, s, LANES), F32)],
        compiler_params=pltpu.CompilerParams(
            dimension_semantics=("arbitrary", "arbitrary"), vmem_limit_bytes=VMEM_LIMIT),
        name="norm_in_proj",
    )(x, mod3, norm_gain, w_in_p, gains, bgate)


def _compress_kernel(kc_ref, vc_ref, wkt_ref, wkb_ref, wvt_ref, wvb_ref, pos_ref, w2k_ref, w2v_ref,
                     kg_ref, ko_ref, vo_ref):
    ncp = kc_ref.shape[1]
    lane = lax.broadcasted_iota(jnp.int32, (ncp, LANES), 1)
    lo = lane < HEAD_DIM

    def hidden(h_ref, wt_ref, wb_ref, ptop, pbot):
        hf = h_ref[0].astype(F32)
        top = _dot((hf + ptop).astype(MXU_DTYPE), wt_ref[...])
        bot = _dot((hf + pbot).astype(MXU_DTYPE), wb_ref[...])
        pre = top + pltpu.roll(bot, ncp - 1, axis=0)
        return jax.nn.silu(pre).astype(MXU_DTYPE)

    hk = hidden(kc_ref, wkt_ref, wkb_ref, pos_ref[0:1, :], pos_ref[1:2, :])
    hv = hidden(vc_ref, wvt_ref, wvb_ref, pos_ref[2:3, :], pos_ref[3:4, :])
    nh = w2k_ref.shape[0]
    for g in range(KV_GROUPS):
        k = _dot(hk[:, g * nh:(g + 1) * nh], w2k_ref[...])
        ss = jnp.sum(k * k, axis=-1, keepdims=True) * (1.0 / HEAD_DIM)
        ko_ref[0, g] = (k * lax.rsqrt(ss + EPS) * kg_ref[...]).astype(ko_ref.dtype)
        v = _dot(hv[:, g * nh:(g + 1) * nh], w2v_ref[...])
        vo_ref[0, g] = jnp.where(lo, v, 1.0).T.astype(vo_ref.dtype)


def _compress_call(kc_r, vc_r, wkt, wkb, wvt, wvb, pos4, w2k, w2v, kgain):
    bsz, ncp, width = kc_r.shape
    full = lambda a: pl.BlockSpec(a.shape, lambda b: (0,) * a.ndim)
    return pl.pallas_call(
        _compress_kernel,
        grid=(bsz,),
        in_specs=[pl.BlockSpec((1, ncp, width), lambda b: (b, 0, 0)),
                  pl.BlockSpec((1, ncp, width), lambda b: (b, 0, 0)),
                  full(wkt), full(wkb), full(wvt), full(wvb), full(pos4), full(w2k), full(w2v),
                  full(kgain)],
        out_specs=[pl.BlockSpec((1, KV_GROUPS, ncp, LANES), lambda b: (b, 0, 0, 0)),
                   pl.BlockSpec((1, KV_GROUPS, LANES, ncp), lambda b: (b, 0, 0, 0))],
        out_shape=[jax.ShapeDtypeStruct((bsz, KV_GROUPS, ncp, LANES), MXU_DTYPE),
                   jax.ShapeDtypeStruct((bsz, KV_GROUPS, LANES, ncp), MXU_DTYPE)],
        compiler_params=pltpu.CompilerParams(
            dimension_semantics=("arbitrary",), vmem_limit_bytes=VMEM_LIMIT),
        name="nsa_compress",
    )(kc_r, vc_r, wkt, wkb, wvt, wvb, pos4, w2k, w2v, kgain)


def _stacked_queries(q_ref, g):
    return jnp.concatenate(
        [q_ref[0, :, LANES * (REP * g + r):LANES * (REP * g + r + 1)] for r in range(REP)], axis=0)


def _with_mask(qst, mask):
    lo = lax.broadcasted_iota(jnp.int32, (1, LANES), 1) < HEAD_DIM
    return jnp.where(lo, qst, mask.astype(qst.dtype))


def _range_mask(first_tile, last_tile):
    blk = lax.broadcasted_iota(jnp.int32, (1, LANES), 1) - HEAD_DIM
    return jnp.where(blk < 2 * first_tile, NEG, jnp.where(blk > 2 * last_tile + 1, NEG, 0.0))


def _k_window(ref, g, first_tile, ntiles):
    start = pl.multiple_of(jnp.maximum(first_tile, 0) * Q_TILE, Q_TILE)
    return ref[0, g, pl.ds(start, ntiles * Q_TILE), :]


def _vt_window(ref, g, first_tile, ntiles):
    start = jnp.maximum(first_tile, 0)
    return jnp.concatenate([ref[0, g, start + t] for t in range(ntiles)], axis=1)


def _near_table(nb, first):
    diag = nb[Q_TILE:]
    return jnp.concatenate([jnp.where(first, diag, nb[:Q_TILE]), diag], axis=0)


def _colmax(s):
    return jnp.max(s, axis=0, keepdims=True)


def _probs(s, m):
    return jnp.exp2(s - m).astype(MXU_DTYPE)


def _values(acc, extra=None):
    den = acc[HEAD_DIM:HEAD_DIM + 1]
    if extra is not None:
        den = den + extra
    return acc[:HEAD_DIM] / den


def _store_heads(o_t, sz_ref, o_ref, g):
    for c in range(REP // 2):
        pair = jnp.concatenate([o_t[:, Q_TILE * (2 * c):Q_TILE * (2 * c + 1)],
                                o_t[:, Q_TILE * (2 * c + 1):Q_TILE * (2 * c + 2)]], axis=0)
        cols = slice(LANES * (2 * g + c), LANES * (2 * g + c + 1))
        o_ref[0, :, cols] = (pair.T * sz_ref[0, :, cols].astype(F32)).astype(o_ref.dtype)


def _attn_a_kernel(sinks_ref, q_ref, k_ref, vt_ref, sz_ref, nb_ref, o_ref):
    i = pl.program_id(1)
    near_mask = _range_mask(i - 1, i)
    for g in range(KV_GROUPS):
        q = _with_mask(_stacked_queries(q_ref, g), near_mask)
        nb = _near_table(nb_ref[g], i == 0)
        s = _dot_nt(_k_window(k_ref, g, i - 1, 2), q) + nb
        sink = jnp.concatenate(
            [jnp.full((1, Q_TILE), sinks_ref[REP * g + r], F32) for r in range(REP)], axis=1)
        m = jnp.maximum(_colmax(s), sink)
        acc = _dot(_vt_window(vt_ref, g, i - 1, 2), _probs(s, m))
        _store_heads(_values(acc, extra=jnp.exp2(sink - m)), sz_ref, o_ref, g)


def _attn_a_call(sinks, qa, ka, vta, sza, nba):
    bsz, s, qwidth = qa.shape
    owidth = sza.shape[2]
    full = lambda a: pl.BlockSpec((1,) + a.shape[1:], lambda b, i: (b,) + (0,) * (a.ndim - 1))
    return pl.pallas_call(
        _attn_a_kernel,
        grid=(bsz, s // Q_TILE),
        in_specs=[pl.BlockSpec(memory_space=pltpu.SMEM),
                  pl.BlockSpec((1, Q_TILE, qwidth), lambda b, i: (b, i, 0)),
                  full(ka), full(vta),
                  pl.BlockSpec((1, Q_TILE, owidth), lambda b, i: (b, i, 0)),
                  pl.BlockSpec(nba.shape, lambda b, i: (0, 0, 0))],
        out_specs=pl.BlockSpec((1, Q_TILE, owidth), lambda b, i: (b, i, 0)),
        out_shape=jax.ShapeDtypeStruct((bsz, s, owidth), MXU_DTYPE),
        compiler_params=pltpu.CompilerParams(
            dimension_semantics=("arbitrary", "arbitrary"), vmem_limit_bytes=VMEM_LIMIT),
        name="attn_swa_sink",
    )(sinks, qa, ka, vta, sza, nba)


def _nsa_scores(g, i, q_ref, kc_ref, kw_ref, nb_ref):
    qst = _stacked_queries(q_ref, g)
    q_near = _with_mask(qst, _range_mask(i - 1, i))
    nb = nb_ref[g]
    nb_near = _near_table(nb[Q_TILE:], i == 0)
    sc = _dot_nt(kc_ref[0, g], q_near)
    q_wfar = _with_mask(qst, _range_mask(i - WIN_TILES, i - 2))
    s_wf = _dot_nt(_k_window(kw_ref, g, i - WIN_TILES, WIN_TILES - 1), q_wfar)
    edge = jnp.where(i >= WIN_TILES, nb[:Q_TILE], 0.0)
    s_wf = jnp.concatenate([s_wf[:Q_TILE] + edge, s_wf[Q_TILE:]], axis=0)
    s_wn = _dot_nt(_k_window(kw_ref, g, i - 1, 2), q_near) + nb_near
    return qst, nb_near, sc, s_wf, s_wn


def _nsa_compressed(g, i, sc, vct_ref, ovl_ref, score_ref):
    ncp = sc.shape[0]
    tok = i * Q_TILE + (lax.broadcasted_iota(jnp.int32, (1, ROWS), 1) & (Q_TILE - 1))
    last_visible = (tok - (CMP_LEN - 1)) // CMP_STRIDE
    z = jnp.where(lax.broadcasted_iota(jnp.int32, (ncp, ROWS), 0) <= last_visible, sc, NEG)
    m = _colmax(z)
    m = jnp.where(m > 0.5 * NEG, m, 0.0)
    e = jnp.exp2(z - m)
    p = e * (1.0 / jnp.maximum(jnp.sum(e, axis=0, keepdims=True), 1e-30))
    o_cmp = _dot(vct_ref[0, g], p.astype(MXU_DTYPE))[:HEAD_DIM]
    psum = sum(p[:, r * Q_TILE:(r + 1) * Q_TILE] for r in range(REP))
    ovl = ovl_ref[...]
    imp = sum(_dot(ovl, piece) for piece in _split3(psum))

    blk = lax.broadcasted_iota(jnp.int32, (LANES, Q_TILE), 0)
    cur = (i * Q_TILE + lax.broadcasted_iota(jnp.int32, (LANES, Q_TILE), 1)) // SEL_LEN
    bonus = jnp.where(blk == 0, FORCE_BONUS,
                      jnp.where(blk == cur, FORCE_BONUS,
                                jnp.where(blk == cur - 1, FORCE_BONUS, 0.0)))
    score_ref[g] = jnp.where(blk <= cur, imp + bonus, NEG)
    return o_cmp


def _nsa_window(g, i, s_wf, s_wn, vwt_ref):
    mw = jnp.maximum(_colmax(s_wf), _colmax(s_wn))
    acc_w = (_dot(_vt_window(vwt_ref, g, i - WIN_TILES, WIN_TILES - 1), _probs(s_wf, mw))
             + _dot(_vt_window(vwt_ref, g, i - 1, 2), _probs(s_wn, mw)))
    return _values(acc_w)


def _nsa_select(g, i, score_ref, topk):
    nslab = SEL_LEN // 8
    sub = lax.broadcasted_iota(jnp.int32, (8, Q_TILE), 0)
    cur_t = (i * Q_TILE + lax.broadcasted_iota(jnp.int32, (8, Q_TILE), 1)) // SEL_LEN
    slabs = [score_ref[g, 8 * v:8 * (v + 1), :] for v in range(nslab)]
    cnts = [jnp.zeros((8, Q_TILE), F32) for _ in range(nslab)]
    for jp in range(SEL_LEN):
        rowv = score_ref[g, jp:jp + 1, :]
        for v in range(nslab):
            ge = jnp.where(rowv >= slabs[v], 1.0, 0.0)
            gt = jnp.where(rowv > slabs[v], 1.0, 0.0)
            if 8 * v > jp:
                beats = ge
            elif 8 * v + 7 < jp:
                beats = gt
            else:
                beats = jnp.where(sub + 8 * v > jp, ge, gt)
            cnts[v] = cnts[v] + beats
    sel_t = [jnp.where(cnts[v] < topk, jnp.where(sub + 8 * v <= cur_t, 0.0, NEG), NEG)
             for v in range(nslab)]
    sel_t = jnp.concatenate([jnp.full((LANES - SEL_LEN, Q_TILE), NEG, F32)] + sel_t, axis=0)
    return jnp.concatenate([sel_t.T] * REP, axis=0)


def _nsa_selected_near(g, i, qst, nb_near, selm, ks_ref, vst_ref):
    s_sn = _dot_nt(_k_window(ks_ref, g, i - 1, 2), _with_mask(qst, selm)) + nb_near
    m_s = _colmax(s_sn)
    acc_s = _dot(_vt_window(vst_ref, g, i - 1, 2), _probs(s_sn, m_s))
    blk_row = lax.broadcasted_iota(jnp.int32, (1, LANES), 1) - HEAD_DIM
    qs_far = _with_mask(qst, jnp.where(blk_row >= 2 * (i - 1), NEG, selm))
    return m_s, acc_s, qs_far


def _attn_b_kernel(q_ref, kc_ref, vct_ref, ks_ref, vst_ref, kw_ref, vwt_ref, sz_ref, gb_ref, nb_ref,
                   ovl_ref, o_ref, score_ref, *, topk):
    i = pl.program_id(1)
    groups = range(KV_GROUPS)
    scores = [_nsa_scores(g, i, q_ref, kc_ref, kw_ref, nb_ref) for g in groups]
    o_cmp = [_nsa_compressed(g, i, scores[g][2], vct_ref, ovl_ref, score_ref) for g in groups]
    o_win, selm = [], []
    for g in groups:
        o_win.append(_nsa_window(g, i, scores[g][3], scores[g][4], vwt_ref))
        selm.append(_nsa_select(g, i, score_ref, topk))
    near = [_nsa_selected_near(g, i, scores[g][0], scores[g][1], selm[g], ks_ref, vst_ref)
            for g in groups]
    qs_far = [n[2] for n in near]

    def far_chunks(chunks, carry):
        s_c = [[_dot_nt(_k_window(ks_ref, g, c * FAR_TILES, FAR_TILES), qs_far[g])
                for g in range(KV_GROUPS)] for c in chunks]
        state = list(carry)
        for n, c in enumerate(chunks):
            for g in range(KV_GROUPS):
                m_old, acc = state[2 * g], state[2 * g + 1]
                m_new = jnp.maximum(m_old, _colmax(s_c[n][g]))
                pv = _dot(_vt_window(vst_ref, g, c * FAR_TILES, FAR_TILES),
                          _probs(s_c[n][g], m_new))
                state[2 * g], state[2 * g + 1] = m_new, jnp.exp2(m_old - m_new) * acc + pv
        return tuple(state)

    nfar = (i + FAR_TILES - 2) // FAR_TILES
    far = lax.fori_loop(0, nfar // 2, lambda p, carry: far_chunks([2 * p, 2 * p + 1], carry),
                        tuple(v for n in near for v in n[:2]))
    far = lax.cond(nfar % 2 == 1, lambda carry: far_chunks([nfar - 1], carry),
                   lambda carry: carry, far)

    for g in groups:
        o_sel = _values(far[2 * g + 1])
        gates = gb_ref[0, g].T
        mix = []
        for r in range(REP):
            cols = slice(r * Q_TILE, (r + 1) * Q_TILE)
            mix.append(gates[3 * r:3 * r + 1] * o_cmp[g][:, cols]
                       + gates[3 * r + 1:3 * r + 2] * o_sel[:, cols]
                       + gates[3 * r + 2:3 * r + 3] * o_win[g][:, cols])
        _store_heads(jnp.concatenate(mix, axis=1), sz_ref, o_ref, g)


def _attn_b_call(qb, kcmp, vcmpt, ks, vst, kw, vwt, szb, gb, nbw, ovl):
    bsz, s, qwidth = qb.shape
    owidth = szb.shape[2]
    full = lambda a: pl.BlockSpec((1,) + a.shape[1:], lambda b, i: (b,) + (0,) * (a.ndim - 1))
    return pl.pallas_call(
        functools.partial(_attn_b_kernel, topk=min(SEL_TOPK, s // SEL_LEN)),
        grid=(bsz, s // Q_TILE),
        in_specs=[pl.BlockSpec((1, Q_TILE, qwidth), lambda b, i: (b, i, 0)),
                  full(kcmp), full(vcmpt), full(ks), full(vst), full(kw), full(vwt),
                  pl.BlockSpec((1, Q_TILE, owidth), lambda b, i: (b, i, 0)),
                  pl.BlockSpec((1, KV_GROUPS, Q_TILE, LANES), lambda b, i: (b, 0, i, 0)),
                  pl.BlockSpec(nbw.shape, lambda b, i: (0, 0, 0)),
                  pl.BlockSpec(ovl.shape, lambda b, i: (0, 0))],
        out_specs=pl.BlockSpec((1, Q_TILE, owidth), lambda b, i: (b, i, 0)),
        out_shape=jax.ShapeDtypeStruct((bsz, s, owidth), MXU_DTYPE),
        scratch_shapes=[pltpu.VMEM((KV_GROUPS, LANES, Q_TILE), F32)],
        compiler_params=pltpu.CompilerParams(
            dimension_semantics=("arbitrary", "arbitrary"), vmem_limit_bytes=VMEM_LIMIT),
        name="attn_nsa",
    )(qb, kcmp, vcmpt, ks, vst, kw, vwt, szb, gb, nbw, ovl)


def _out_kernel(x_ref, mod_ref, ya_ref, yb_ref, w_ref, o_ref):
    half = ya_ref.shape[2]
    out = _dot(ya_ref[0], w_ref[0:half, :]) + _dot(yb_ref[0], w_ref[half:2 * half, :])
    o_ref[0] = x_ref[0] + mod_ref[0, 2:3, :] * out


def _out_call(x, mod3, ya, yb, w_out):
    bsz, s, d = x.shape
    tm = PROJ_TM
    xs = pl.BlockSpec((1, tm, d), lambda b, i: (b, i, 0))
    ys = pl.BlockSpec((1, tm, 512), lambda b, i: (b, i, 0))
    return pl.pallas_call(
        _out_kernel,
        grid=(bsz, s // tm),
        in_specs=[xs, pl.BlockSpec((1, 3, d), lambda b, i: (b, 0, 0)), ys, ys,
                  pl.BlockSpec(w_out.shape, lambda b, i: (0, 0))],
        out_specs=xs,
        out_shape=jax.ShapeDtypeStruct(x.shape, x.dtype),
        compiler_params=pltpu.CompilerParams(
            dimension_semantics=("arbitrary", "arbitrary"), vmem_limit_bytes=VMEM_LIMIT),
        name="out_proj_residual",
    )(x, mod3, ya, yb, w_out)


def _t5_bucket(dist):
    n = jnp.maximum(dist, 0)
    max_exact = N_BUCKETS // 2
    nf = jnp.maximum(n, 1).astype(F32)
    large = max_exact + (jnp.log(nf / max_exact) / math.log(MAX_DISTANCE / max_exact)
                         * (N_BUCKETS - max_exact)).astype(jnp.int32)
    large = jnp.minimum(large, N_BUCKETS - 1)
    return jnp.where(n < max_exact, n, large)


def _bias_table_kernel(rel_ref, idx_ref, idx_edge_ref, nba_ref, nbw_ref):
    h = pl.program_id(0)
    hb = h + pl.num_programs(0)

    def lookup(idx, head):
        acc = jnp.zeros(idx.shape, F32)
        for b in range(N_BUCKETS):
            acc = jnp.where(idx == b, rel_ref[b, head], acc)
        return acc

    idx = idx_ref[...]
    dist = (lax.broadcasted_iota(jnp.int32, idx.shape, 1) + Q_TILE
            - lax.broadcasted_iota(jnp.int32, idx.shape, 0))
    causal = dist >= 0
    far = rel_ref[N_BUCKETS - 1, hb]
    nba_ref[0] = jnp.where(causal, jnp.where(dist < SWA_WINDOW, lookup(idx, h) * LOG2E, NEG), NEG)
    near_b = jnp.where(causal, (lookup(idx, hb) - far) * LOG2E, NEG)
    idx_e = idx_edge_ref[...]
    dist_e = (lax.broadcasted_iota(jnp.int32, idx_e.shape, 1) + NSA_WINDOW
              - lax.broadcasted_iota(jnp.int32, idx_e.shape, 0))
    edge_b = jnp.where(dist_e < NSA_WINDOW, (lookup(idx_e, hb) - far) * LOG2E, NEG)
    nbw_ref[0] = jnp.concatenate([edge_b, near_b], axis=0)


def _near_tables(rel_bias):
    nheads = rel_bias.shape[1] // 2
    tq = jnp.arange(Q_TILE)[None, :]
    idx = _t5_bucket(tq + Q_TILE - jnp.arange(2 * Q_TILE)[:, None]).astype(jnp.int32)
    idx_edge = _t5_bucket(tq + NSA_WINDOW - jnp.arange(Q_TILE)[:, None]).astype(jnp.int32)
    return pl.pallas_call(
        _bias_table_kernel,
        grid=(nheads,),
        in_specs=[pl.BlockSpec(memory_space=pltpu.SMEM),
                  pl.BlockSpec(idx.shape, lambda h: (0, 0)),
                  pl.BlockSpec(idx_edge.shape, lambda h: (0, 0))],
        out_specs=[pl.BlockSpec((1, 2 * Q_TILE, Q_TILE), lambda h: (h // REP, 0, h % REP)),
                   pl.BlockSpec((1, 3 * Q_TILE, Q_TILE), lambda h: (h // REP, 0, h % REP))],
        out_shape=[jax.ShapeDtypeStruct((nheads // REP, 2 * Q_TILE, ROWS), F32),
                   jax.ShapeDtypeStruct((nheads // REP, 3 * Q_TILE, ROWS), F32)],
        name="t5_bias_tables",
    )(rel_bias, idx, idx_edge)


def _overlap_table(s, ncp):
    nc = (s - CMP_LEN) // CMP_STRIDE + 1
    ns = s // SEL_LEN
    c_lo = jnp.arange(ncp)[None, :] * CMP_STRIDE
    s_lo = jnp.arange(LANES)[:, None] * SEL_LEN
    ov = jnp.clip(jnp.minimum(c_lo + CMP_LEN, s_lo + SEL_LEN) - jnp.maximum(c_lo, s_lo), 0, None)
    ov = ov.astype(F32) / CMP_LEN
    ok = (jnp.arange(ncp)[None, :] < nc) & (jnp.arange(LANES)[:, None] < ns)
    return jnp.where(ok, ov, 0.0).astype(MXU_DTYPE)


def _compress_weights(w1, pos):
    hid = w1.shape[1]
    half = CMP_LEN // 2
    w1r = w1.reshape(CMP_LEN, HEAD_DIM, hid)
    eye = jnp.eye(KV_GROUPS, dtype=w1.dtype)
    expand = lambda w: jnp.einsum("ldj,gh->lgdhj", w, eye).reshape(
        half * KV_GROUPS * HEAD_DIM, KV_GROUPS * hid).astype(MXU_DTYPE)
    prow = lambda p: jnp.broadcast_to(p[:, None, :], (half, KV_GROUPS, HEAD_DIM)).reshape(1, -1)
    return expand(w1r[:half]), expand(w1r[half:]), prow(pos[:half]), prow(pos[half:])


def _upper_zero(row):
    return jnp.concatenate([row, jnp.zeros_like(row)]).reshape(1, LANES).astype(F32)


def _layer(x, c, w_ada, b_ada, norm_gain, w_in, b_nsa_gate, q_gain_a, k_gain_a, sinks, q_gain_b,
           k_gain_cmp, k_gain_sel, k_gain_win, cmp_pos_k, cmp_pos_v, w_cmp_k1, w_cmp_k2,
           w_cmp_v1, w_cmp_v2, w_out, rel_bias):
    bsz, s, d = x.shape
    assert s % (FAR_TILES * Q_TILE) == 0 and s // SEL_LEN <= HEAD_DIM and s // Q_TILE >= WIN_TILES
    assert w_in.shape == (d, D_PROJ) and s % PROJ_TM == 0
    qscale = HEAD_DIM ** -0.5 * LOG2E

    mod3 = _mod_call(c, w_ada, b_ada).reshape(bsz, 3, d)
    w_in_p = jnp.pad(w_in, ((0, 0), (0, D_PROJ_PAD - D_PROJ))).astype(MXU_DTYPE)
    tile2 = lambda gn: jnp.concatenate([gn, gn]).astype(F32)
    gains = jnp.zeros((8, LANES), F32)
    for n, gn in enumerate((k_gain_a, k_gain_sel, k_gain_win, q_gain_a * qscale, q_gain_b * qscale)):
        gains = gains.at[n].set(tile2(gn))
    bgate = jnp.pad(b_nsa_gate, (0, LANES - b_nsa_gate.shape[0])).reshape(1, LANES).astype(F32)
    (qa, ka, vta, sza, qb, kc, vc, ks, vst, kw, vwt, szb, gb) = _proj_call(
        x, mod3, norm_gain.reshape(1, d).astype(F32), w_in_p, gains, bgate)

    ncp = s // CMP_STRIDE
    wkt, wkb, pkt, pkb = _compress_weights(w_cmp_k1, cmp_pos_k)
    wvt, wvb, pvt, pvb = _compress_weights(w_cmp_v1, cmp_pos_v)
    pos4 = jnp.concatenate([pkt, pkb, pvt, pvb], axis=0).astype(F32)
    pad2 = lambda w: jnp.pad(w, ((0, 0), (0, LANES - HEAD_DIM))).astype(MXU_DTYPE)
    half_block = CMP_STRIDE * LANES
    kcmp, vcmpt = _compress_call(kc.reshape(bsz, ncp, half_block), vc.reshape(bsz, ncp, half_block),
                                 wkt, wkb, wvt, wvb, pos4, pad2(w_cmp_k2), pad2(w_cmp_v2),
                                 _upper_zero(k_gain_cmp))

    nba, nbw = _near_tables(rel_bias.astype(F32))
    ya = _attn_a_call(sinks.astype(F32) * LOG2E, qa, ka, vta, sza, nba)
    yb = _attn_b_call(qb, kcmp, vcmpt, ks, vst, kw, vwt, szb, gb, nbw, _overlap_table(s, ncp))
    return _out_call(x, mod3, ya, yb, w_out.astype(MXU_DTYPE))


def kernel(x, c, w_ada, b_ada, norm_gain, w_in, b_nsa_gate, q_gain_a, k_gain_a, sinks, q_gain_b,
           k_gain_cmp, k_gain_sel, k_gain_win, cmp_pos_k, cmp_pos_v, w_cmp_k1, w_cmp_k2,
           w_cmp_v1, w_cmp_v2, w_out, rel_bias):
    for l in range(w_ada.shape[0]):
        x = _layer(x, c, w_ada[l], b_ada[l], norm_gain[l], w_in[l], b_nsa_gate[l], q_gain_a[l],
                   k_gain_a[l], sinks[l], q_gain_b[l], k_gain_cmp[l], k_gain_sel[l],
                   k_gain_win[l], cmp_pos_k[l], cmp_pos_v[l], w_cmp_k1[l], w_cmp_k2[l],
                   w_cmp_v1[l], w_cmp_v2[l], w_out[l], rel_bias)
    return x
```

```python
import functools
import math

import jax
import jax.numpy as jnp
from jax import lax
from jax.experimental import pallas as pl
from jax.experimental.pallas import tpu as pltpu

MXU_DTYPE = jnp.bfloat16
F32 = jnp.float32

HEAD_DIM = 64
LANES = 128
Q_TILE = 128
KV_GROUPS = 2
REP = 4
ROWS = REP * Q_TILE
FAR_TILES = 4
SWA_WINDOW = 128
NSA_WINDOW = 512
WIN_TILES = NSA_WINDOW // Q_TILE
CMP_LEN = 32
CMP_STRIDE = 16
SEL_LEN = 64
SEL_TOPK = 16
N_BUCKETS = 32
MAX_DISTANCE = 128
FORCE_BONUS = 1e4
EPS = 1e-6
NEG = -1e30
LOG2E = 1.4426950408889634
MAX_BOUND = 40.0
PROJ_TM = 512
VMEM_LIMIT = 48 * 1024 * 1024

OFF_QA, OFF_KA, OFF_VA, OFF_ZA = 0, 512, 640, 768
OFF_QB, OFF_KC, OFF_VC, OFF_KS, OFF_VS, OFF_KW, OFF_VW, OFF_ZB, OFF_GB = (
    1280, 1792, 1920, 2048, 2176, 2304, 2432, 2560, 3072)
D_PROJ = 3096
D_PROJ_PAD = 3200


def _dot(a, b):
    return jnp.dot(a, b, preferred_element_type=F32)


def _dot_nt(a, b):
    return lax.dot_general(a, b, (((1,), (1,)), ((), ())), preferred_element_type=F32)


def _split3(x):
    hi = x.astype(MXU_DTYPE)
    r1 = x - hi.astype(F32)
    mid = r1.astype(MXU_DTYPE)
    lo = (r1 - mid.astype(F32)).astype(MXU_DTYPE)
    return hi, mid, lo


def _mod_kernel(c_ref, w_ref, b_ref, o_ref):
    sc = jax.nn.silu(c_ref[...])
    w = w_ref[...]
    acc = jnp.zeros(o_ref.shape, F32)
    for a in _split3(sc):
        for b in _split3(w)[:2]:
            acc = acc + _dot(a, b)
    o_ref[...] = acc + b_ref[...]


def _mod_call(c, w_ada, b_ada):
    bsz, d = c.shape
    n = w_ada.shape[1]
    tn = 512
    return pl.pallas_call(
        _mod_kernel,
        grid=(n // tn,),
        in_specs=[pl.BlockSpec((bsz, d), lambda j: (0, 0)),
                  pl.BlockSpec((d, tn), lambda j: (0, j)),
                  pl.BlockSpec((1, tn), lambda j: (0, j))],
        out_specs=pl.BlockSpec((bsz, tn), lambda j: (0, j)),
        out_shape=jax.ShapeDtypeStruct((bsz, n), F32),
        name="adaln_mod",
    )(c, w_ada, b_ada.reshape(1, n))


def _proj_kernel(x_ref, mod_ref, gain_ref, w_ref, kg_ref, bg_ref,
                 qa_ref, ka_ref, va_ref, sza_ref, qb_ref, kc_ref, vc_ref,
                 ks_ref, vs_ref, kw_ref, vw_ref, szb_ref, gb_ref):
    tm = x_ref.shape[1]
    si = pl.program_id(1)
    x = x_ref[0]
    ms = jnp.mean(x * x, axis=-1, keepdims=True)
    y = x * lax.rsqrt(ms + EPS) * gain_ref[...]
    h = y * (1.0 + mod_ref[0, 1:2, :]) + mod_ref[0, 0:1, :]
    hb = h.astype(MXU_DTYPE)

    def seg(off, n):
        return _dot(hb, w_ref[:, off:off + n])

    lane = lax.broadcasted_iota(jnp.int32, (tm, LANES), 1)
    lo = lane < HEAD_DIM
    row = lax.broadcasted_iota(jnp.int32, (tm, LANES), 0) + si * tm
    onehot = jnp.where(lane - HEAD_DIM == row // SEL_LEN, 1.0, 0.0)

    def half_norm(t, gain_row):
        sq = t * t
        s_lo = jnp.sum(jnp.where(lo, sq, 0.0), axis=-1, keepdims=True)
        s_hi = jnp.sum(jnp.where(lo, 0.0, sq), axis=-1, keepdims=True)
        inv = jnp.where(lo, lax.rsqrt(s_lo * (1.0 / HEAD_DIM) + EPS),
                        lax.rsqrt(s_hi * (1.0 / HEAD_DIM) + EPS))
        return t * inv * gain_row

    def split_heads(t, extra):
        return (jnp.where(lo, t, extra), jnp.where(lo, pltpu.roll(t, HEAD_DIM, axis=1), extra))

    def write_q(ref, off, gain_row):
        t = seg(off, REP * KV_GROUPS * HEAD_DIM)
        for c in range(REP * KV_GROUPS // 2):
            pair = split_heads(half_norm(t[:, LANES * c:LANES * (c + 1)], gain_row), 0.0)
            for j in range(2):
                col = LANES * (2 * c + j)
                ref[0, :, col:col + LANES] = pair[j].astype(ref.dtype)

    def write_kv(k_ref, vt_ref, off, gain_row):
        kv = seg(off, 2 * LANES)
        for g, t in enumerate(split_heads(half_norm(kv[:, :LANES], gain_row), onehot)):
            k_ref[0, g] = t.astype(k_ref.dtype)
        for g, t in enumerate(split_heads(kv[:, LANES:], 1.0)):
            tt = t.T
            for j in range(tm // Q_TILE):
                vt_ref[0, g, j] = tt[:, Q_TILE * j:Q_TILE * (j + 1)].astype(vt_ref.dtype)

    write_q(qa_ref, OFF_QA, kg_ref[3:4, :])
    write_kv(ka_ref, va_ref, OFF_KA, kg_ref[0:1, :])
    sza_ref[0] = jax.nn.silu(seg(OFF_ZA, 512)).astype(sza_ref.dtype)
    write_q(qb_ref, OFF_QB, kg_ref[4:5, :])
    kvc = seg(OFF_KC, 2 * LANES)
    kc_ref[0] = kvc[:, :LANES].astype(kc_ref.dtype)
    vc_ref[0] = kvc[:, LANES:].astype(vc_ref.dtype)
    write_kv(ks_ref, vs_ref, OFF_KS, kg_ref[1:2, :])
    write_kv(kw_ref, vw_ref, OFF_KW, kg_ref[2:3, :])
    szb_ref[0] = jax.nn.silu(seg(OFF_ZB, 512)).astype(szb_ref.dtype)
    gates = jax.nn.sigmoid(seg(OFF_GB, LANES) + bg_ref[...])
    gb_ref[0, 0] = gates
    gb_ref[0, 1] = pltpu.roll(gates, LANES - REP * 3, axis=1)


def _proj_call(x, mod3, norm_gain, w_in_p, gains, bgate):
    bsz, s, d = x.shape
    tm = PROJ_TM
    dt = MXU_DTYPE
    nt = tm // Q_TILE
    qwidth = REP * KV_GROUPS * LANES
    rowq = pl.BlockSpec((1, tm, qwidth), lambda b, i: (b, i, 0))
    row512 = pl.BlockSpec((1, tm, 512), lambda b, i: (b, i, 0))
    row128 = pl.BlockSpec((1, tm, LANES), lambda b, i: (b, i, 0))
    grp = pl.BlockSpec((1, KV_GROUPS, tm, LANES), lambda b, i: (b, 0, i, 0))
    grpt = pl.BlockSpec((1, KV_GROUPS, nt, LANES, Q_TILE), lambda b, i: (b, 0, i, 0, 0))
    sq = jax.ShapeDtypeStruct((bsz, s, qwidth), dt)
    s512 = jax.ShapeDtypeStruct((bsz, s, 512), dt)
    s128 = jax.ShapeDtypeStruct((bsz, s, LANES), dt)
    sgrp = jax.ShapeDtypeStruct((bsz, KV_GROUPS, s, LANES), dt)
    sgrpt = jax.ShapeDtypeStruct((bsz, KV_GROUPS, s // Q_TILE, LANES, Q_TILE), dt)
    return pl.pallas_call(
        _proj_kernel,
        grid=(bsz, s // tm),
        in_specs=[pl.BlockSpec((1, tm, d), lambda b, i: (b, i, 0)),
                  pl.BlockSpec((1, 3, d), lambda b, i: (b, 0, 0)),
                  pl.BlockSpec((1, d), lambda b, i: (0, 0)),
                  pl.BlockSpec((d, D_PROJ_PAD), lambda b, i: (0, 0)),
                  pl.BlockSpec((8, LANES), lambda b, i: (0, 0)),
                  pl.BlockSpec((1, LANES), lambda b, i: (0, 0))],
        out_specs=[rowq, grp, grpt, row512, rowq, row128, row128, grp, grpt, grp, grpt, row512, grp],
        out_shape=[sq, sgrp, sgrpt, s512, sq, s128, s128, sgrp, sgrpt, sgrp, sgrpt, s512,
                   jax.ShapeDtypeStruct((bsz, KV_GROUPS, s, LANES), F32)],
        compiler_params=pltpu.CompilerParams(
            dimension_semantics=("arbitrary", "arbitrary"), vmem_limit_bytes=VMEM_LIMIT),
        name="norm_in_proj",
    )(x, mod3, norm_gain, w_in_p, gains, bgate)


def _compress_kernel(kc_ref, vc_ref, wkt_ref, wkb_ref, wvt_ref, wvb_ref, pos_ref, w2k_ref, w2v_ref,
                     kg_ref, ko_ref, vo_ref):
    ncp = kc_ref.shape[1]
    lane = lax.broadcasted_iota(jnp.int32, (ncp, LANES), 1)
    lo = lane < HEAD_DIM

    def hidden(h_ref, wt_ref, wb_ref, ptop, pbot):
        hf = h_ref[0].astype(F32)
        top = _dot((hf + ptop).astype(MXU_DTYPE), wt_ref[...])
        bot = _dot((hf + pbot).astype(MXU_DTYPE), wb_ref[...])
        pre = top + pltpu.roll(bot, ncp - 1, axis=0)
        return jax.nn.silu(pre).astype(MXU_DTYPE)

    hk = hidden(kc_ref, wkt_ref, wkb_ref, pos_ref[0:1, :], pos_ref[1:2, :])
    hv = hidden(vc_ref, wvt_ref, wvb_ref, pos_ref[2:3, :], pos_ref[3:4, :])
    nh = w2k_ref.shape[0]
    for g in range(KV_GROUPS):
        k = _dot(hk[:, g * nh:(g + 1) * nh], w2k_ref[...])
        ss = jnp.sum(k * k, axis=-1, keepdims=True) * (1.0 / HEAD_DIM)
        ko_ref[0, g] = (k * lax.rsqrt(ss + EPS) * kg_ref[...]).astype(ko_ref.dtype)
        v = _dot(hv[:, g * nh:(g + 1) * nh], w2v_ref[...])
        vo_ref[0, g] = jnp.where(lo, v, 1.0).T.astype(vo_ref.dtype)


def _compress_call(kc_r, vc_r, wkt, wkb, wvt, wvb, pos4, w2k, w2v, kgain):
    bsz, ncp, width = kc_r.shape
    full = lambda a: pl.BlockSpec(a.shape, lambda b: (0,) * a.ndim)
    return pl.pallas_call(
        _compress_kernel,
        grid=(bsz,),
        in_specs=[pl.BlockSpec((1, ncp, width), lambda b: (b, 0, 0)),
                  pl.BlockSpec((1, ncp, width), lambda b: (b, 0, 0)),
                  full(wkt), full(wkb), full(wvt), full(wvb), full(pos4), full(w2k), full(w2v),
                  full(kgain)],
        out_specs=[pl.BlockSpec((1, KV_GROUPS, ncp, LANES), lambda b: (b, 0, 0, 0)),
                   pl.BlockSpec((1, KV_GROUPS, LANES, ncp), lambda b: (b, 0, 0, 0))],
        out_shape=[jax.ShapeDtypeStruct((bsz, KV_GROUPS, ncp, LANES), MXU_DTYPE),
                   jax.ShapeDtypeStruct((bsz, KV_GROUPS, LANES, ncp), MXU_DTYPE)],
        compiler_params=pltpu.CompilerParams(
            dimension_semantics=("arbitrary",), vmem_limit_bytes=VMEM_LIMIT),
        name="nsa_compress",
    )(kc_r, vc_r, wkt, wkb, wvt, wvb, pos4, w2k, w2v, kgain)


def _stacked_queries(q_ref, g):
    return jnp.concatenate(
        [q_ref[0, :, LANES * (REP * g + r):LANES * (REP * g + r + 1)] for r in range(REP)], axis=0)


def _with_mask(qst, mask):
    lo = lax.broadcasted_iota(jnp.int32, (1, LANES), 1) < HEAD_DIM
    return jnp.where(lo, qst, mask.astype(qst.dtype))


def _open(bound):
    return 0.0 if bound is None else -bound


def _range_mask(first_tile, last_tile, bound):
    blk = lax.broadcasted_iota(jnp.int32, (1, LANES), 1) - HEAD_DIM
    return jnp.where(blk < 2 * first_tile, NEG,
                     jnp.where(blk > 2 * last_tile + 1, NEG, _open(bound)))


def _k_window(ref, g, first_tile, ntiles):
    start = pl.multiple_of(jnp.maximum(first_tile, 0) * Q_TILE, Q_TILE)
    return ref[0, g, pl.ds(start, ntiles * Q_TILE), :]


def _vt_window(ref, g, first_tile, ntiles):
    start = jnp.maximum(first_tile, 0)
    return jnp.concatenate([ref[0, g, start + t] for t in range(ntiles)], axis=1)


def _near_table(nb, first):
    diag = nb[Q_TILE:]
    return jnp.concatenate([jnp.where(first, diag, nb[:Q_TILE]), diag], axis=0)


def _colmax(s):
    return jnp.max(s, axis=0, keepdims=True)


def _probs(s, m):
    return jnp.exp2(s if m is None else s - m).astype(MXU_DTYPE)


def _values(acc, extra=None):
    den = acc[HEAD_DIM:HEAD_DIM + 1]
    if extra is not None:
        den = den + extra
    return acc[:HEAD_DIM] / den


def _store_heads(o_t, sz_ref, o_ref, g):
    for c in range(REP // 2):
        pair = jnp.concatenate([o_t[:, Q_TILE * (2 * c):Q_TILE * (2 * c + 1)],
                                o_t[:, Q_TILE * (2 * c + 1):Q_TILE * (2 * c + 2)]], axis=0)
        cols = slice(LANES * (2 * g + c), LANES * (2 * g + c + 1))
        o_ref[0, :, cols] = (pair.T * sz_ref[0, :, cols].astype(F32)).astype(o_ref.dtype)


def _both_paths(par_ref, body):
    bounded = par_ref[1] > 0.5
    pl.when(bounded)(lambda: body(par_ref[0]))
    pl.when(jnp.logical_not(bounded))(lambda: body(None))


def _attn_a_body(bound, sinks_ref, q_ref, k_ref, vt_ref, sz_ref, nb_ref, o_ref):
    i = pl.program_id(1)
    near_mask = _range_mask(i - 1, i, bound)
    for g in range(KV_GROUPS):
        q = _with_mask(_stacked_queries(q_ref, g), near_mask)
        nb = _near_table(nb_ref[g], i == 0)
        s = _dot_nt(_k_window(k_ref, g, i - 1, 2), q) + nb
        sink = jnp.concatenate(
            [jnp.full((1, Q_TILE), sinks_ref[REP * g + r], F32) for r in range(REP)], axis=1)
        m = None if bound is not None else jnp.maximum(_colmax(s), sink)
        acc = _dot(_vt_window(vt_ref, g, i - 1, 2), _probs(s, m))
        sink_w = jnp.exp2(sink - (bound if m is None else m))
        _store_heads(_values(acc, extra=sink_w), sz_ref, o_ref, g)


def _attn_a_kernel(par_ref, *refs):
    _both_paths(par_ref, lambda bound: _attn_a_body(bound, *refs))


def _attn_a_call(par, sinks, qa, ka, vta, sza, nba):
    bsz, s, qwidth = qa.shape
    owidth = sza.shape[2]
    full = lambda a: pl.BlockSpec((1,) + a.shape[1:], lambda b, i: (b,) + (0,) * (a.ndim - 1))
    smem = pl.BlockSpec(memory_space=pltpu.SMEM)
    return pl.pallas_call(
        _attn_a_kernel,
        grid=(bsz, s // Q_TILE),
        in_specs=[smem, smem,
                  pl.BlockSpec((1, Q_TILE, qwidth), lambda b, i: (b, i, 0)),
                  full(ka), full(vta),
                  pl.BlockSpec((1, Q_TILE, owidth), lambda b, i: (b, i, 0)),
                  pl.BlockSpec(nba.shape, lambda b, i: (0, 0, 0))],
        out_specs=pl.BlockSpec((1, Q_TILE, owidth), lambda b, i: (b, i, 0)),
        out_shape=jax.ShapeDtypeStruct((bsz, s, owidth), MXU_DTYPE),
        compiler_params=pltpu.CompilerParams(
            dimension_semantics=("arbitrary", "arbitrary"), vmem_limit_bytes=VMEM_LIMIT),
        name="attn_swa_sink",
    )(par, sinks, qa, ka, vta, sza, nba)


def _nsa_scores(g, i, bound, q_ref, kc_ref, kw_ref, nb_ref):
    qst = _stacked_queries(q_ref, g)
    q_near = _with_mask(qst, _range_mask(i - 1, i, bound))
    nb = nb_ref[g]
    nb_near = _near_table(nb[Q_TILE:], i == 0)
    sc = _dot_nt(kc_ref[0, g], q_near)
    q_wfar = _with_mask(qst, _range_mask(i - WIN_TILES, i - 2, bound))
    s_wf = _dot_nt(_k_window(kw_ref, g, i - WIN_TILES, WIN_TILES - 1), q_wfar)
    edge = jnp.where(i >= WIN_TILES, nb[:Q_TILE], 0.0)
    s_wf = jnp.concatenate([s_wf[:Q_TILE] + edge, s_wf[Q_TILE:]], axis=0)
    s_wn = _dot_nt(_k_window(kw_ref, g, i - 1, 2), q_near) + nb_near
    return qst, nb_near, sc, s_wf, s_wn


def _nsa_compressed(g, i, bound, sc, vct_ref, ovl_ref, score_ref):
    ncp = sc.shape[0]
    tok = i * Q_TILE + (lax.broadcasted_iota(jnp.int32, (1, ROWS), 1) & (Q_TILE - 1))
    last_visible = (tok - (CMP_LEN - 1)) // CMP_STRIDE
    z = jnp.where(lax.broadcasted_iota(jnp.int32, (ncp, ROWS), 0) <= last_visible, sc, NEG)
    if bound is None:
        m = _colmax(z)
        m = jnp.where(m > 0.5 * NEG, m, 0.0)
    else:
        m = bound
    e = jnp.exp2(z - m)
    p = e * (1.0 / jnp.maximum(jnp.sum(e, axis=0, keepdims=True), 1e-30))
    o_cmp = _dot(vct_ref[0, g], p.astype(MXU_DTYPE))[:HEAD_DIM]
    psum = sum(p[:, r * Q_TILE:(r + 1) * Q_TILE] for r in range(REP))
    ovl = ovl_ref[...]
    imp = sum(_dot(ovl, piece) for piece in _split3(psum))

    blk = lax.broadcasted_iota(jnp.int32, (LANES, Q_TILE), 0)
    cur = (i * Q_TILE + lax.broadcasted_iota(jnp.int32, (LANES, Q_TILE), 1)) // SEL_LEN
    bonus = jnp.where(blk == 0, FORCE_BONUS,
                      jnp.where(blk == cur, FORCE_BONUS,
                                jnp.where(blk == cur - 1, FORCE_BONUS, 0.0)))
    score_ref[g] = jnp.where(blk <= cur, imp + bonus, NEG)
    return o_cmp


def _nsa_window(g, i, bound, s_wf, s_wn, vwt_ref):
    mw = None if bound is not None else jnp.maximum(_colmax(s_wf), _colmax(s_wn))
    acc_w = (_dot(_vt_window(vwt_ref, g, i - WIN_TILES, WIN_TILES - 1), _probs(s_wf, mw))
             + _dot(_vt_window(vwt_ref, g, i - 1, 2), _probs(s_wn, mw)))
    return _values(acc_w)


def _nsa_select(g, i, bound, score_ref, topk):
    nslab = SEL_LEN // 8
    sub = lax.broadcasted_iota(jnp.int32, (8, Q_TILE), 0)
    cur_t = (i * Q_TILE + lax.broadcasted_iota(jnp.int32, (8, Q_TILE), 1)) // SEL_LEN
    slabs = [score_ref[g, 8 * v:8 * (v + 1), :] for v in range(nslab)]
    cnts = [jnp.zeros((8, Q_TILE), F32) for _ in range(nslab)]
    for jp in range(SEL_LEN):
        rowv = score_ref[g, jp:jp + 1, :]
        for v in range(nslab):
            ge = jnp.where(rowv >= slabs[v], 1.0, 0.0)
            gt = jnp.where(rowv > slabs[v], 1.0, 0.0)
            if 8 * v > jp:
                beats = ge
            elif 8 * v + 7 < jp:
                beats = gt
            else:
                beats = jnp.where(sub + 8 * v > jp, ge, gt)
            cnts[v] = cnts[v] + beats
    sel_t = [jnp.where(cnts[v] < topk, jnp.where(sub + 8 * v <= cur_t, _open(bound), NEG), NEG)
             for v in range(nslab)]
    sel_t = jnp.concatenate([jnp.full((LANES - SEL_LEN, Q_TILE), NEG, F32)] + sel_t, axis=0)
    return jnp.concatenate([sel_t.T] * REP, axis=0)


def _nsa_selected_near(g, i, bound, qst, nb_near, selm, ks_ref, vst_ref):
    s_sn = _dot_nt(_k_window(ks_ref, g, i - 1, 2), _with_mask(qst, selm)) + nb_near
    m_s = None if bound is not None else _colmax(s_sn)
    acc_s = _dot(_vt_window(vst_ref, g, i - 1, 2), _probs(s_sn, m_s))
    blk_row = lax.broadcasted_iota(jnp.int32, (1, LANES), 1) - HEAD_DIM
    qs_far = _with_mask(qst, jnp.where(blk_row >= 2 * (i - 1), NEG, selm))
    return ([acc_s] if m_s is None else [m_s, acc_s]), qs_far


def _attn_b_body(bound, q_ref, kc_ref, vct_ref, ks_ref, vst_ref, kw_ref, vwt_ref, sz_ref, gb_ref,
                 nb_ref, ovl_ref, o_ref, score_ref, *, topk):
    i = pl.program_id(1)
    groups = range(KV_GROUPS)
    scores = [_nsa_scores(g, i, bound, q_ref, kc_ref, kw_ref, nb_ref) for g in groups]
    o_cmp = [_nsa_compressed(g, i, bound, scores[g][2], vct_ref, ovl_ref, score_ref)
             for g in groups]
    o_win, selm = [], []
    for g in groups:
        o_win.append(_nsa_window(g, i, bound, scores[g][3], scores[g][4], vwt_ref))
        selm.append(_nsa_select(g, i, bound, score_ref, topk))
    near = [_nsa_selected_near(g, i, bound, scores[g][0], scores[g][1], selm[g], ks_ref, vst_ref)
            for g in groups]
    qs_far = [n[1] for n in near]
    nstate = len(near[0][0])

    def far_chunks(chunks, carry):
        s_c = [[_dot_nt(_k_window(ks_ref, g, c * FAR_TILES, FAR_TILES), qs_far[g])
                for g in groups] for c in chunks]
        state = list(carry)
        for n, c in enumerate(chunks):
            for g in groups:
                vt = _vt_window(vst_ref, g, c * FAR_TILES, FAR_TILES)
                if bound is not None:
                    state[g] = state[g] + _dot(vt, _probs(s_c[n][g], None))
                else:
                    m_old, acc = state[2 * g], state[2 * g + 1]
                    m_new = jnp.maximum(m_old, _colmax(s_c[n][g]))
                    pv = _dot(vt, _probs(s_c[n][g], m_new))
                    state[2 * g], state[2 * g + 1] = m_new, jnp.exp2(m_old - m_new) * acc + pv
        return tuple(state)

    nfar = (i + FAR_TILES - 2) // FAR_TILES
    far = lax.fori_loop(0, nfar // 2, lambda p, carry: far_chunks([2 * p, 2 * p + 1], carry),
                        tuple(v for n in near for v in n[0]))
    far = lax.cond(nfar % 2 == 1, lambda carry: far_chunks([nfar - 1], carry),
                   lambda carry: carry, far)

    for g in groups:
        o_sel = _values(far[nstate * (g + 1) - 1])
        gates = gb_ref[0, g].T
        mix = []
        for r in range(REP):
            cols = slice(r * Q_TILE, (r + 1) * Q_TILE)
            mix.append(gates[3 * r:3 * r + 1] * o_cmp[g][:, cols]
                       + gates[3 * r + 1:3 * r + 2] * o_sel[:, cols]
                       + gates[3 * r + 2:3 * r + 3] * o_win[g][:, cols])
        _store_heads(jnp.concatenate(mix, axis=1), sz_ref, o_ref, g)


def _attn_b_kernel(par_ref, *refs, topk):
    _both_paths(par_ref, lambda bound: _attn_b_body(bound, *refs, topk=topk))


def _attn_b_call(par, qb, kcmp, vcmpt, ks, vst, kw, vwt, szb, gb, nbw, ovl):
    bsz, s, qwidth = qb.shape
    owidth = szb.shape[2]
    full = lambda a: pl.BlockSpec((1,) + a.shape[1:], lambda b, i: (b,) + (0,) * (a.ndim - 1))
    return pl.pallas_call(
        functools.partial(_attn_b_kernel, topk=min(SEL_TOPK, s // SEL_LEN)),
        grid=(bsz, s // Q_TILE),
        in_specs=[pl.BlockSpec(memory_space=pltpu.SMEM),
                  pl.BlockSpec((1, Q_TILE, qwidth), lambda b, i: (b, i, 0)),
                  full(kcmp), full(vcmpt), full(ks), full(vst), full(kw), full(vwt),
                  pl.BlockSpec((1, Q_TILE, owidth), lambda b, i: (b, i, 0)),
                  pl.BlockSpec((1, KV_GROUPS, Q_TILE, LANES), lambda b, i: (b, 0, i, 0)),
                  pl.BlockSpec(nbw.shape, lambda b, i: (0, 0, 0)),
                  pl.BlockSpec(ovl.shape, lambda b, i: (0, 0))],
        out_specs=pl.BlockSpec((1, Q_TILE, owidth), lambda b, i: (b, i, 0)),
        out_shape=jax.ShapeDtypeStruct((bsz, s, owidth), MXU_DTYPE),
        scratch_shapes=[pltpu.VMEM((KV_GROUPS, LANES, Q_TILE), F32)],
        compiler_params=pltpu.CompilerParams(
            dimension_semantics=("arbitrary", "arbitrary"), vmem_limit_bytes=VMEM_LIMIT),
        name="attn_nsa",
    )(par, qb, kcmp, vcmpt, ks, vst, kw, vwt, szb, gb, nbw, ovl)


def _out_kernel(x_ref, mod_ref, ya_ref, yb_ref, w_ref, o_ref):
    half = ya_ref.shape[2]
    out = _dot(ya_ref[0], w_ref[0:half, :]) + _dot(yb_ref[0], w_ref[half:2 * half, :])
    o_ref[0] = x_ref[0] + mod_ref[0, 2:3, :] * out


def _out_call(x, mod3, ya, yb, w_out):
    bsz, s, d = x.shape
    tm = PROJ_TM
    xs = pl.BlockSpec((1, tm, d), lambda b, i: (b, i, 0))
    ys = pl.BlockSpec((1, tm, 512), lambda b, i: (b, i, 0))
    return pl.pallas_call(
        _out_kernel,
        grid=(bsz, s // tm),
        in_specs=[xs, pl.BlockSpec((1, 3, d), lambda b, i: (b, 0, 0)), ys, ys,
                  pl.BlockSpec(w_out.shape, lambda b, i: (0, 0))],
        out_specs=xs,
        out_shape=jax.ShapeDtypeStruct(x.shape, x.dtype),
        compiler_params=pltpu.CompilerParams(
            dimension_semantics=("arbitrary", "arbitrary"), vmem_limit_bytes=VMEM_LIMIT),
        name="out_proj_residual",
    )(x, mod3, ya, yb, w_out)


def _t5_bucket(dist):
    n = jnp.maximum(dist, 0)
    max_exact = N_BUCKETS // 2
    nf = jnp.maximum(n, 1).astype(F32)
    large = max_exact + (jnp.log(nf / max_exact) / math.log(MAX_DISTANCE / max_exact)
                         * (N_BUCKETS - max_exact)).astype(jnp.int32)
    large = jnp.minimum(large, N_BUCKETS - 1)
    return jnp.where(n < max_exact, n, large)


def _bias_table_kernel(rel_ref, idx_ref, idx_edge_ref, nba_ref, nbw_ref):
    h = pl.program_id(0)
    hb = h + pl.num_programs(0)

    def lookup(idx, head):
        acc = jnp.zeros(idx.shape, F32)
        for b in range(N_BUCKETS):
            acc = jnp.where(idx == b, rel_ref[b, head], acc)
        return acc

    idx = idx_ref[...]
    dist = (lax.broadcasted_iota(jnp.int32, idx.shape, 1) + Q_TILE
            - lax.broadcasted_iota(jnp.int32, idx.shape, 0))
    causal = dist >= 0
    far = rel_ref[N_BUCKETS - 1, hb]
    nba_ref[0] = jnp.where(causal, jnp.where(dist < SWA_WINDOW, lookup(idx, h) * LOG2E, NEG), NEG)
    near_b = jnp.where(causal, (lookup(idx, hb) - far) * LOG2E, NEG)
    idx_e = idx_edge_ref[...]
    dist_e = (lax.broadcasted_iota(jnp.int32, idx_e.shape, 1) + NSA_WINDOW
              - lax.broadcasted_iota(jnp.int32, idx_e.shape, 0))
    edge_b = jnp.where(dist_e < NSA_WINDOW, (lookup(idx_e, hb) - far) * LOG2E, NEG)
    nbw_ref[0] = jnp.concatenate([edge_b, near_b], axis=0)


def _near_tables(rel_bias):
    nheads = rel_bias.shape[1] // 2
    tq = jnp.arange(Q_TILE)[None, :]
    idx = _t5_bucket(tq + Q_TILE - jnp.arange(2 * Q_TILE)[:, None]).astype(jnp.int32)
    idx_edge = _t5_bucket(tq + NSA_WINDOW - jnp.arange(Q_TILE)[:, None]).astype(jnp.int32)
    return pl.pallas_call(
        _bias_table_kernel,
        grid=(nheads,),
        in_specs=[pl.BlockSpec(memory_space=pltpu.SMEM),
                  pl.BlockSpec(idx.shape, lambda h: (0, 0)),
                  pl.BlockSpec(idx_edge.shape, lambda h: (0, 0))],
        out_specs=[pl.BlockSpec((1, 2 * Q_TILE, Q_TILE), lambda h: (h // REP, 0, h % REP)),
                   pl.BlockSpec((1, 3 * Q_TILE, Q_TILE), lambda h: (h // REP, 0, h % REP))],
        out_shape=[jax.ShapeDtypeStruct((nheads // REP, 2 * Q_TILE, ROWS), F32),
                   jax.ShapeDtypeStruct((nheads // REP, 3 * Q_TILE, ROWS), F32)],
        name="t5_bias_tables",
    )(rel_bias, idx, idx_edge)


def _overlap_table(s, ncp):
    nc = (s - CMP_LEN) // CMP_STRIDE + 1
    ns = s // SEL_LEN
    c_lo = jnp.arange(ncp)[None, :] * CMP_STRIDE
    s_lo = jnp.arange(LANES)[:, None] * SEL_LEN
    ov = jnp.clip(jnp.minimum(c_lo + CMP_LEN, s_lo + SEL_LEN) - jnp.maximum(c_lo, s_lo), 0, None)
    ov = ov.astype(F32) / CMP_LEN
    ok = (jnp.arange(ncp)[None, :] < nc) & (jnp.arange(LANES)[:, None] < ns)
    return jnp.where(ok, ov, 0.0).astype(MXU_DTYPE)


def _compress_weights(w1, pos):
    hid = w1.shape[1]
    half = CMP_LEN // 2
    w1r = w1.reshape(CMP_LEN, HEAD_DIM, hid)
    eye = jnp.eye(KV_GROUPS, dtype=w1.dtype)
    expand = lambda w: jnp.einsum("ldj,gh->lgdhj", w, eye).reshape(
        half * KV_GROUPS * HEAD_DIM, KV_GROUPS * hid).astype(MXU_DTYPE)
    prow = lambda p: jnp.broadcast_to(p[:, None, :], (half, KV_GROUPS, HEAD_DIM)).reshape(1, -1)
    return expand(w1r[:half]), expand(w1r[half:]), prow(pos[:half]), prow(pos[half:])


def _logit_bound(q_gain, k_gains, bias, floor=None):
    gk = jnp.max(jnp.stack([jnp.max(jnp.abs(k)) for k in k_gains]))
    m = 1.02 * HEAD_DIM * jnp.max(jnp.abs(q_gain)) * gk + jnp.max(jnp.abs(bias))
    if floor is not None:
        m = jnp.maximum(m, jnp.max(floor))
    m = jnp.ceil(m).astype(F32)
    return jnp.stack([m, (m <= MAX_BOUND).astype(F32)])


def _upper_zero(row):
    return jnp.concatenate([row, jnp.zeros_like(row)]).reshape(1, LANES).astype(F32)


def _layer(x, c, w_ada, b_ada, norm_gain, w_in, b_nsa_gate, q_gain_a, k_gain_a, sinks, q_gain_b,
           k_gain_cmp, k_gain_sel, k_gain_win, cmp_pos_k, cmp_pos_v, w_cmp_k1, w_cmp_k2,
           w_cmp_v1, w_cmp_v2, w_out, rel_bias):
    bsz, s, d = x.shape
    assert s % (FAR_TILES * Q_TILE) == 0 and s // SEL_LEN <= HEAD_DIM and s // Q_TILE >= WIN_TILES
    assert w_in.shape == (d, D_PROJ) and s % PROJ_TM == 0
    qscale = HEAD_DIM ** -0.5 * LOG2E

    mod3 = _mod_call(c, w_ada, b_ada).reshape(bsz, 3, d)
    w_in_p = jnp.pad(w_in, ((0, 0), (0, D_PROJ_PAD - D_PROJ))).astype(MXU_DTYPE)
    tile2 = lambda gn: jnp.concatenate([gn, gn]).astype(F32)
    gains = jnp.zeros((8, LANES), F32)
    for n, gn in enumerate((k_gain_a, k_gain_sel, k_gain_win, q_gain_a * qscale, q_gain_b * qscale)):
        gains = gains.at[n].set(tile2(gn))
    bgate = jnp.pad(b_nsa_gate, (0, LANES - b_nsa_gate.shape[0])).reshape(1, LANES).astype(F32)
    (qa, ka, vta, sza, qb, kc, vc, ks, vst, kw, vwt, szb, gb) = _proj_call(
        x, mod3, norm_gain.reshape(1, d).astype(F32), w_in_p, gains, bgate)

    ncp = s // CMP_STRIDE
    wkt, wkb, pkt, pkb = _compress_weights(w_cmp_k1, cmp_pos_k)
    wvt, wvb, pvt, pvb = _compress_weights(w_cmp_v1, cmp_pos_v)
    pos4 = jnp.concatenate([pkt, pkb, pvt, pvb], axis=0).astype(F32)
    pad2 = lambda w: jnp.pad(w, ((0, 0), (0, LANES - HEAD_DIM))).astype(MXU_DTYPE)
    half_block = CMP_STRIDE * LANES
    kcmp, vcmpt = _compress_call(kc.reshape(bsz, ncp, half_block), vc.reshape(bsz, ncp, half_block),
                                 wkt, wkb, wvt, wvb, pos4, pad2(w_cmp_k2), pad2(w_cmp_v2),
                                 _upper_zero(k_gain_cmp))

    nba, nbw = _near_tables(rel_bias.astype(F32))
    half = rel_bias.shape[1] // 2
    sinks2 = sinks.astype(F32) * LOG2E
    par_a = _logit_bound(q_gain_a * qscale, [k_gain_a], rel_bias[:, :half] * LOG2E, floor=sinks2)
    par_b = _logit_bound(q_gain_b * qscale, [k_gain_cmp, k_gain_sel, k_gain_win],
                         (rel_bias[:, half:] - rel_bias[N_BUCKETS - 1, half:]) * LOG2E)
    ya = _attn_a_call(par_a, sinks2, qa, ka, vta, sza, nba)
    yb = _attn_b_call(par_b, qb, kcmp, vcmpt, ks, vst, kw, vwt, szb, gb, nbw,
                      _overlap_table(s, ncp))
    return _out_call(x, mod3, ya, yb, w_out.astype(MXU_DTYPE))


def kernel(x, c, w_ada, b_ada, norm_gain, w_in, b_nsa_gate, q_gain_a, k_gain_a, sinks, q_gain_b,
           k_gain_cmp, k_gain_sel, k_gain_win, cmp_pos_k, cmp_pos_v, w_cmp_k1, w_cmp_k2,
           w_cmp_v1, w_cmp_v2, w_out, rel_bias):
    for l in range(w_ada.shape[0]):
        x = _layer(x, c, w_ada[l], b_ada[l], norm_gain[l], w_in[l], b_nsa_gate[l], q_gain_a[l],
                   k_gain_a[l], sinks[l], q_gain_b[l], k_gain_cmp[l], k_gain_sel[l],
                   k_gain_win[l], cmp_pos_k[l], cmp_pos_v[l], w_cmp_k1[l], w_cmp_k2[l],
                   w_cmp_v1[l], w_cmp_v2[l], w_out[l], rel_bias)
    return x
```

```python
import functools
import math

import jax
import jax.numpy as jnp
from jax import lax
from jax.experimental import pallas as pl
from jax.experimental.pallas import tpu as pltpu

MXU_DTYPE = jnp.bfloat16
F32 = jnp.float32

HEAD_DIM = 64
LANES = 128
Q_TILE = 128
KV_GROUPS = 2
REP = 4
ROWS = REP * Q_TILE
FAR_TILES = 4
SWA_STEP_TILES = 8
NSA_STEP_TILES = 2
SWA_WINDOW = 128
NSA_WINDOW = 512
WIN_TILES = NSA_WINDOW // Q_TILE
CMP_LEN = 32
CMP_STRIDE = 16
SEL_LEN = 64
SEL_TOPK = 16
N_BUCKETS = 32
MAX_DISTANCE = 128
FORCE_BONUS = 1e4
EPS = 1e-6
NEG = -1e30
LOG2E = 1.4426950408889634
MAX_BOUND = 40.0
PROJ_TM = 512
PROJ_SUBTILES = 2
VMEM_LIMIT = 48 * 1024 * 1024

OFF_QA, OFF_KA, OFF_VA, OFF_ZA = 0, 512, 640, 768
OFF_QB, OFF_KC, OFF_VC, OFF_KS, OFF_VS, OFF_KW, OFF_VW, OFF_ZB, OFF_GB = (
    1280, 1792, 1920, 2048, 2176, 2304, 2432, 2560, 3072)
D_PROJ = 3096
D_PROJ_PAD = 3200


def _dot(a, b):
    return jnp.dot(a, b, preferred_element_type=F32)


def _dot_nt(a, b):
    return lax.dot_general(a, b, (((1,), (1,)), ((), ())), preferred_element_type=F32)


def _split3(x):
    hi = x.astype(MXU_DTYPE)
    r1 = x - hi.astype(F32)
    mid = r1.astype(MXU_DTYPE)
    lo = (r1 - mid.astype(F32)).astype(MXU_DTYPE)
    return hi, mid, lo


def _mod_kernel(c_ref, w_ref, b_ref, o_ref):
    sc = jax.nn.silu(c_ref[...])
    w = w_ref[...]
    acc = jnp.zeros(o_ref.shape, F32)
    for a in _split3(sc):
        for b in _split3(w)[:2]:
            acc = acc + _dot(a, b)
    o_ref[...] = acc + b_ref[...]


def _mod_call(c, w_ada, b_ada):
    bsz, d = c.shape
    n = w_ada.shape[1]
    tn = 512
    return pl.pallas_call(
        _mod_kernel,
        grid=(n // tn,),
        in_specs=[pl.BlockSpec((bsz, d), lambda j: (0, 0)),
                  pl.BlockSpec((d, tn), lambda j: (0, j)),
                  pl.BlockSpec((1, tn), lambda j: (0, j))],
        out_specs=pl.BlockSpec((bsz, tn), lambda j: (0, j)),
        out_shape=jax.ShapeDtypeStruct((bsz, n), F32),
        name="adaln_mod",
    )(c, w_ada, b_ada.reshape(1, n))


def _proj_kernel(x_ref, mod_ref, gain_ref, w_ref, kg_ref, bg_ref,
                 qa_ref, ka_ref, va_ref, sza_ref, qb_ref, kc_ref, vc_ref,
                 ks_ref, vs_ref, kw_ref, vw_ref, szb_ref, gb_ref):
    tm = x_ref.shape[1]
    ts = tm // PROJ_SUBTILES
    si = pl.program_id(1)

    def normed(sub):
        x = x_ref[0, sub * ts:(sub + 1) * ts]
        ms = jnp.mean(x * x, axis=-1, keepdims=True)
        y = x * lax.rsqrt(ms + EPS) * gain_ref[...]
        h = y * (1.0 + mod_ref[0, 1:2, :]) + mod_ref[0, 0:1, :]
        return h.astype(MXU_DTYPE)

    hbs = [normed(sub) for sub in range(PROJ_SUBTILES)]
    lane = lax.broadcasted_iota(jnp.int32, (ts, LANES), 1)
    lo = lane < HEAD_DIM

    def half_norm(t, gain_row):
        sq = t * t
        s_lo = jnp.sum(jnp.where(lo, sq, 0.0), axis=-1, keepdims=True)
        s_hi = jnp.sum(jnp.where(lo, 0.0, sq), axis=-1, keepdims=True)
        inv = jnp.where(lo, lax.rsqrt(s_lo * (1.0 / HEAD_DIM) + EPS),
                        lax.rsqrt(s_hi * (1.0 / HEAD_DIM) + EPS))
        return t * inv * gain_row

    def split_heads(t, extra):
        return (jnp.where(lo, t, extra), jnp.where(lo, pltpu.roll(t, HEAD_DIM, axis=1), extra))

    for sub, hb in enumerate(hbs):
        rows = slice(sub * ts, (sub + 1) * ts)
        row = lax.broadcasted_iota(jnp.int32, (ts, LANES), 0) + si * tm + sub * ts
        onehot = jnp.where(lane - HEAD_DIM == row // SEL_LEN, 1.0, 0.0)

        def seg(off, n):
            return _dot(hb, w_ref[:, off:off + n])

        def write_q(ref, off, gain_row):
            t = seg(off, REP * KV_GROUPS * HEAD_DIM)
            for c in range(REP * KV_GROUPS // 2):
                pair = split_heads(half_norm(t[:, LANES * c:LANES * (c + 1)], gain_row), 0.0)
                for j in range(2):
                    col = LANES * (2 * c + j)
                    ref[0, rows, col:col + LANES] = pair[j].astype(ref.dtype)

        def write_kv(k_ref, vt_ref, off, gain_row):
            kv = seg(off, 2 * LANES)
            for g, t in enumerate(split_heads(half_norm(kv[:, :LANES], gain_row), onehot)):
                k_ref[0, g, rows] = t.astype(k_ref.dtype)
            for g, t in enumerate(split_heads(kv[:, LANES:], 1.0)):
                tt = t.T
                for j in range(ts // Q_TILE):
                    vt_ref[0, g, sub * (ts // Q_TILE) + j] = (
                        tt[:, Q_TILE * j:Q_TILE * (j + 1)].astype(vt_ref.dtype))

        write_q(qa_ref, OFF_QA, kg_ref[3:4, :])
        write_kv(ka_ref, va_ref, OFF_KA, kg_ref[0:1, :])
        sza_ref[0, rows] = jax.nn.silu(seg(OFF_ZA, 512)).astype(sza_ref.dtype)
        write_q(qb_ref, OFF_QB, kg_ref[4:5, :])
        kvc = seg(OFF_KC, 2 * LANES)
        kc_ref[0, rows] = kvc[:, :LANES].astype(kc_ref.dtype)
        vc_ref[0, rows] = kvc[:, LANES:].astype(vc_ref.dtype)
        write_kv(ks_ref, vs_ref, OFF_KS, kg_ref[1:2, :])
        write_kv(kw_ref, vw_ref, OFF_KW, kg_ref[2:3, :])
        szb_ref[0, rows] = jax.nn.silu(seg(OFF_ZB, 512)).astype(szb_ref.dtype)
        gates = jax.nn.sigmoid(seg(OFF_GB, LANES) + bg_ref[...])
        gb_ref[0, 0, rows] = gates
        gb_ref[0, 1, rows] = pltpu.roll(gates, LANES - REP * 3, axis=1)


def _proj_call(x, mod3, norm_gain, w_in_p, gains, bgate):
    bsz, s, d = x.shape
    tm = PROJ_TM
    dt = MXU_DTYPE
    nt = tm // Q_TILE
    qwidth = REP * KV_GROUPS * LANES
    rowq = pl.BlockSpec((1, tm, qwidth), lambda b, i: (b, i, 0))
    row512 = pl.BlockSpec((1, tm, 512), lambda b, i: (b, i, 0))
    row128 = pl.BlockSpec((1, tm, LANES), lambda b, i: (b, i, 0))
    grp = pl.BlockSpec((1, KV_GROUPS, tm, LANES), lambda b, i: (b, 0, i, 0))
    grpt = pl.BlockSpec((1, KV_GROUPS, nt, LANES, Q_TILE), lambda b, i: (b, 0, i, 0, 0))
    sq = jax.ShapeDtypeStruct((bsz, s, qwidth), dt)
    s512 = jax.ShapeDtypeStruct((bsz, s, 512), dt)
    s128 = jax.ShapeDtypeStruct((bsz, s, LANES), dt)
    sgrp = jax.ShapeDtypeStruct((bsz, KV_GROUPS, s, LANES), dt)
    sgrpt = jax.ShapeDtypeStruct((bsz, KV_GROUPS, s // Q_TILE, LANES, Q_TILE), dt)
    return pl.pallas_call(
        _proj_kernel,
        grid=(bsz, s // tm),
        in_specs=[pl.BlockSpec((1, tm, d), lambda b, i: (b, i, 0)),
                  pl.BlockSpec((1, 3, d), lambda b, i: (b, 0, 0)),
                  pl.BlockSpec((1, d), lambda b, i: (0, 0)),
                  pl.BlockSpec((d, D_PROJ_PAD), lambda b, i: (0, 0)),
                  pl.BlockSpec((8, LANES), lambda b, i: (0, 0)),
                  pl.BlockSpec((1, LANES), lambda b, i: (0, 0))],
        out_specs=[rowq, grp, grpt, row512, rowq, row128, row128, grp, grpt, grp, grpt, row512, grp],
        out_shape=[sq, sgrp, sgrpt, s512, sq, s128, s128, sgrp, sgrpt, sgrp, sgrpt, s512,
                   jax.ShapeDtypeStruct((bsz, KV_GROUPS, s, LANES), F32)],
        compiler_params=pltpu.CompilerParams(
            dimension_semantics=("arbitrary", "arbitrary"), vmem_limit_bytes=VMEM_LIMIT),
        name="norm_in_proj",
    )(x, mod3, norm_gain, w_in_p, gains, bgate)


def _compress_kernel(kc_ref, vc_ref, wkt_ref, wkb_ref, wvt_ref, wvb_ref, pos_ref, w2k_ref, w2v_ref,
                     kg_ref, ko_ref, vo_ref):
    ncp = kc_ref.shape[1]
    lane = lax.broadcasted_iota(jnp.int32, (ncp, LANES), 1)
    lo = lane < HEAD_DIM

    def hidden(h_ref, wt_ref, wb_ref, ptop, pbot):
        hf = h_ref[0].astype(F32)
        top = _dot((hf + ptop).astype(MXU_DTYPE), wt_ref[...])
        bot = _dot((hf + pbot).astype(MXU_DTYPE), wb_ref[...])
        pre = top + pltpu.roll(bot, ncp - 1, axis=0)
        return jax.nn.silu(pre).astype(MXU_DTYPE)

    hk = hidden(kc_ref, wkt_ref, wkb_ref, pos_ref[0:1, :], pos_ref[1:2, :])
    hv = hidden(vc_ref, wvt_ref, wvb_ref, pos_ref[2:3, :], pos_ref[3:4, :])
    nh = w2k_ref.shape[0]
    for g in range(KV_GROUPS):
        k = _dot(hk[:, g * nh:(g + 1) * nh], w2k_ref[...])
        ss = jnp.sum(k * k, axis=-1, keepdims=True) * (1.0 / HEAD_DIM)
        ko_ref[0, g] = (k * lax.rsqrt(ss + EPS) * kg_ref[...]).astype(ko_ref.dtype)
        v = _dot(hv[:, g * nh:(g + 1) * nh], w2v_ref[...])
        vo_ref[0, g] = jnp.where(lo, v, 1.0).T.astype(vo_ref.dtype)


def _compress_call(kc_r, vc_r, wkt, wkb, wvt, wvb, pos4, w2k, w2v, kgain):
    bsz, ncp, width = kc_r.shape
    full = lambda a: pl.BlockSpec(a.shape, lambda b: (0,) * a.ndim)
    return pl.pallas_call(
        _compress_kernel,
        grid=(bsz,),
        in_specs=[pl.BlockSpec((1, ncp, width), lambda b: (b, 0, 0)),
                  pl.BlockSpec((1, ncp, width), lambda b: (b, 0, 0)),
                  full(wkt), full(wkb), full(wvt), full(wvb), full(pos4), full(w2k), full(w2v),
                  full(kgain)],
        out_specs=[pl.BlockSpec((1, KV_GROUPS, ncp, LANES), lambda b: (b, 0, 0, 0)),
                   pl.BlockSpec((1, KV_GROUPS, LANES, ncp), lambda b: (b, 0, 0, 0))],
        out_shape=[jax.ShapeDtypeStruct((bsz, KV_GROUPS, ncp, LANES), MXU_DTYPE),
                   jax.ShapeDtypeStruct((bsz, KV_GROUPS, LANES, ncp), MXU_DTYPE)],
        compiler_params=pltpu.CompilerParams(
            dimension_semantics=("arbitrary",), vmem_limit_bytes=VMEM_LIMIT),
        name="nsa_compress",
    )(kc_r, vc_r, wkt, wkb, wvt, wvb, pos4, w2k, w2v, kgain)


def _tile_rows(t):
    return slice(t * Q_TILE, (t + 1) * Q_TILE)


def _stacked_queries(q_ref, t, g):
    return jnp.concatenate(
        [q_ref[0, _tile_rows(t), LANES * (REP * g + r):LANES * (REP * g + r + 1)]
         for r in range(REP)], axis=0)


def _with_mask(qst, mask):
    lo = lax.broadcasted_iota(jnp.int32, (1, LANES), 1) < HEAD_DIM
    return jnp.where(lo, qst, mask.astype(qst.dtype))


def _open(bound):
    return 0.0 if bound is None else -bound


def _range_mask(first_tile, last_tile, bound):
    blk = lax.broadcasted_iota(jnp.int32, (1, LANES), 1) - HEAD_DIM
    return jnp.where(blk < 2 * first_tile, NEG,
                     jnp.where(blk > 2 * last_tile + 1, NEG, _open(bound)))


def _k_window(ref, g, first_tile, ntiles):
    start = pl.multiple_of(jnp.maximum(first_tile, 0) * Q_TILE, Q_TILE)
    return ref[0, g, pl.ds(start, ntiles * Q_TILE), :]


def _vt_window(ref, g, first_tile, ntiles):
    start = jnp.maximum(first_tile, 0)
    return jnp.concatenate([ref[0, g, start + t] for t in range(ntiles)], axis=1)


def _near_table(nb, first):
    diag = nb[Q_TILE:]
    return jnp.concatenate([jnp.where(first, diag, nb[:Q_TILE]), diag], axis=0)


def _colmax(s):
    return jnp.max(s, axis=0, keepdims=True)


def _probs(s, m):
    return jnp.exp2(s if m is None else s - m).astype(MXU_DTYPE)


def _values(acc, extra=None):
    den = acc[HEAD_DIM:HEAD_DIM + 1]
    if extra is not None:
        den = den + extra
    return acc[:HEAD_DIM] / den


def _store_heads(o_t, sz_ref, o_ref, t, g):
    rows = _tile_rows(t)
    for c in range(REP // 2):
        pair = jnp.concatenate([o_t[:, Q_TILE * (2 * c):Q_TILE * (2 * c + 1)],
                                o_t[:, Q_TILE * (2 * c + 1):Q_TILE * (2 * c + 2)]], axis=0)
        cols = slice(LANES * (2 * g + c), LANES * (2 * g + c + 1))
        o_ref[0, rows, cols] = (pair.T * sz_ref[0, rows, cols].astype(F32)).astype(o_ref.dtype)


def _both_paths(par_ref, body):
    bounded = par_ref[1] > 0.5
    pl.when(bounded)(lambda: body(par_ref[0]))
    pl.when(jnp.logical_not(bounded))(lambda: body(None))


def _attn_a_body(bound, sinks_ref, q_ref, k_ref, vt_ref, sz_ref, nb_ref, o_ref):
    tiles = q_ref.shape[1] // Q_TILE
    units = [(t, g) for t in range(tiles) for g in range(KV_GROUPS)]
    tile = lambda t: pl.program_id(1) * tiles + t
    sinks = [jnp.concatenate([jnp.full((1, Q_TILE), sinks_ref[REP * g + r], F32)
                              for r in range(REP)], axis=1) for g in range(KV_GROUPS)]
    scores = []
    for t, g in units:
        i = tile(t)
        q = _with_mask(_stacked_queries(q_ref, t, g), _range_mask(i - 1, i, bound))
        nb = _near_table(nb_ref[g], i == 0)
        scores.append(_dot_nt(_k_window(k_ref, g, i - 1, 2), q) + nb)
    outs = []
    for (t, g), s in zip(units, scores):
        m = None if bound is not None else jnp.maximum(_colmax(s), sinks[g])
        acc = _dot(_vt_window(vt_ref, g, tile(t) - 1, 2), _probs(s, m))
        sink_w = jnp.exp2(sinks[g] - (bound if m is None else m))
        outs.append(_values(acc, extra=sink_w))
    for (t, g), o in zip(units, outs):
        _store_heads(o, sz_ref, o_ref, t, g)


def _attn_a_kernel(par_ref, *refs):
    _both_paths(par_ref, lambda bound: _attn_a_body(bound, *refs))


def _attn_a_call(par, sinks, qa, ka, vta, sza, nba):
    bsz, s, qwidth = qa.shape
    owidth = sza.shape[2]
    full = lambda a: pl.BlockSpec((1,) + a.shape[1:], lambda b, i: (b,) + (0,) * (a.ndim - 1))
    smem = pl.BlockSpec(memory_space=pltpu.SMEM)
    rows = SWA_STEP_TILES * Q_TILE
    return pl.pallas_call(
        _attn_a_kernel,
        grid=(bsz, s // rows),
        in_specs=[smem, smem,
                  pl.BlockSpec((1, rows, qwidth), lambda b, i: (b, i, 0)),
                  full(ka), full(vta),
                  pl.BlockSpec((1, rows, owidth), lambda b, i: (b, i, 0)),
                  pl.BlockSpec(nba.shape, lambda b, i: (0, 0, 0))],
        out_specs=pl.BlockSpec((1, rows, owidth), lambda b, i: (b, i, 0)),
        out_shape=jax.ShapeDtypeStruct((bsz, s, owidth), MXU_DTYPE),
        compiler_params=pltpu.CompilerParams(
            dimension_semantics=("arbitrary", "arbitrary"), vmem_limit_bytes=VMEM_LIMIT),
        name="attn_swa_sink",
    )(par, sinks, qa, ka, vta, sza, nba)


def _nsa_scores(t, g, i, bound, q_ref, kc_ref, kw_ref, nb_ref):
    qst = _stacked_queries(q_ref, t, g)
    q_near = _with_mask(qst, _range_mask(i - 1, i, bound))
    nb = nb_ref[g]
    nb_near = _near_table(nb[Q_TILE:], i == 0)
    sc = _dot_nt(kc_ref[0, g], q_near)
    q_wfar = _with_mask(qst, _range_mask(i - WIN_TILES, i - 2, bound))
    s_wf = _dot_nt(_k_window(kw_ref, g, i - WIN_TILES, WIN_TILES - 1), q_wfar)
    edge = jnp.where(i >= WIN_TILES, nb[:Q_TILE], 0.0)
    s_wf = jnp.concatenate([s_wf[:Q_TILE] + edge, s_wf[Q_TILE:]], axis=0)
    s_wn = _dot_nt(_k_window(kw_ref, g, i - 1, 2), q_near) + nb_near
    return qst, nb_near, sc, s_wf, s_wn


def _nsa_compressed(g, i, bound, sc, vct_ref, ovl_ref, score_out):
    ncp = sc.shape[0]
    tok = i * Q_TILE + (lax.broadcasted_iota(jnp.int32, (1, ROWS), 1) & (Q_TILE - 1))
    last_visible = (tok - (CMP_LEN - 1)) // CMP_STRIDE
    z = jnp.where(lax.broadcasted_iota(jnp.int32, (ncp, ROWS), 0) <= last_visible, sc, NEG)
    if bound is None:
        m = _colmax(z)
        m = jnp.where(m > 0.5 * NEG, m, 0.0)
    else:
        m = bound
    e = jnp.exp2(z - m)
    p = e * (1.0 / jnp.maximum(jnp.sum(e, axis=0, keepdims=True), 1e-30))
    o_cmp = _dot(vct_ref[0, g], p.astype(MXU_DTYPE))[:HEAD_DIM]
    psum = sum(p[:, r * Q_TILE:(r + 1) * Q_TILE] for r in range(REP))
    ovl = ovl_ref[...]
    imp = sum(_dot(ovl, piece) for piece in _split3(psum))

    blk = lax.broadcasted_iota(jnp.int32, (LANES, Q_TILE), 0)
    cur = (i * Q_TILE + lax.broadcasted_iota(jnp.int32, (LANES, Q_TILE), 1)) // SEL_LEN
    bonus = jnp.where(blk == 0, FORCE_BONUS,
                      jnp.where(blk == cur, FORCE_BONUS,
                                jnp.where(blk == cur - 1, FORCE_BONUS, 0.0)))
    score_out[...] = jnp.where(blk <= cur, imp + bonus, NEG)
    return o_cmp


def _nsa_window(g, i, bound, s_wf, s_wn, vwt_ref):
    mw = None if bound is not None else jnp.maximum(_colmax(s_wf), _colmax(s_wn))
    acc_w = (_dot(_vt_window(vwt_ref, g, i - WIN_TILES, WIN_TILES - 1), _probs(s_wf, mw))
             + _dot(_vt_window(vwt_ref, g, i - 1, 2), _probs(s_wn, mw)))
    return _values(acc_w)


def _nsa_select(i, bound, score_in, topk):
    nslab = SEL_LEN // 8
    sub = lax.broadcasted_iota(jnp.int32, (8, Q_TILE), 0)
    cur_t = (i * Q_TILE + lax.broadcasted_iota(jnp.int32, (8, Q_TILE), 1)) // SEL_LEN
    slabs = [score_in[8 * v:8 * (v + 1), :] for v in range(nslab)]
    cnts = [jnp.zeros((8, Q_TILE), F32) for _ in range(nslab)]
    for jp in range(SEL_LEN):
        rowv = score_in[jp:jp + 1, :]
        for v in range(nslab):
            ge = jnp.where(rowv >= slabs[v], 1.0, 0.0)
            gt = jnp.where(rowv > slabs[v], 1.0, 0.0)
            if 8 * v > jp:
                beats = ge
            elif 8 * v + 7 < jp:
                beats = gt
            else:
                beats = jnp.where(sub + 8 * v > jp, ge, gt)
            cnts[v] = cnts[v] + beats
    sel_t = [jnp.where(cnts[v] < topk, jnp.where(sub + 8 * v <= cur_t, _open(bound), NEG), NEG)
             for v in range(nslab)]
    sel_t = jnp.concatenate([jnp.full((LANES - SEL_LEN, Q_TILE), NEG, F32)] + sel_t, axis=0)
    return jnp.concatenate([sel_t.T] * REP, axis=0)


def _nsa_selected_near(g, i, bound, qst, nb_near, selm, ks_ref, vst_ref):
    s_sn = _dot_nt(_k_window(ks_ref, g, i - 1, 2), _with_mask(qst, selm)) + nb_near
    m_s = None if bound is not None else _colmax(s_sn)
    acc_s = _dot(_vt_window(vst_ref, g, i - 1, 2), _probs(s_sn, m_s))
    blk_row = lax.broadcasted_iota(jnp.int32, (1, LANES), 1) - HEAD_DIM
    qs_far = _with_mask(qst, jnp.where(blk_row >= 2 * (i - 1), NEG, selm))
    return ([acc_s] if m_s is None else [m_s, acc_s]), qs_far


def _attn_b_body(bound, q_ref, kc_ref, vct_ref, ks_ref, vst_ref, kw_ref, vwt_ref, sz_ref, gb_ref,
                 nb_ref, ovl_ref, o_ref, score_ref, *, topk):
    tiles = q_ref.shape[1] // Q_TILE
    units = [(t, g, pl.program_id(1) * tiles + t) for t in range(tiles) for g in range(KV_GROUPS)]
    scores = [_nsa_scores(t, g, i, bound, q_ref, kc_ref, kw_ref, nb_ref) for t, g, i in units]
    o_cmp = [_nsa_compressed(g, i, bound, scores[u][2], vct_ref, ovl_ref, score_ref.at[u])
             for u, (t, g, i) in enumerate(units)]
    o_win, selm = [], []
    for u, (t, g, i) in enumerate(units):
        o_win.append(_nsa_window(g, i, bound, scores[u][3], scores[u][4], vwt_ref))
        selm.append(_nsa_select(i, bound, score_ref.at[u], topk))
    near = [_nsa_selected_near(g, i, bound, scores[u][0], scores[u][1], selm[u], ks_ref, vst_ref)
            for u, (t, g, i) in enumerate(units)]
    qs_far = [n[1] for n in near]
    nstate = len(near[0][0])

    def far_chunks(chunks, carry):
        s_c = [[_dot_nt(_k_window(ks_ref, g, c * FAR_TILES, FAR_TILES), qs_far[u])
                for u, (t, g, i) in enumerate(units)] for c in chunks]
        state = list(carry)
        for n, c in enumerate(chunks):
            for u, (t, g, i) in enumerate(units):
                vt = _vt_window(vst_ref, g, c * FAR_TILES, FAR_TILES)
                if bound is not None:
                    state[u] = state[u] + _dot(vt, _probs(s_c[n][u], None))
                else:
                    m_old, acc = state[2 * u], state[2 * u + 1]
                    m_new = jnp.maximum(m_old, _colmax(s_c[n][u]))
                    pv = _dot(vt, _probs(s_c[n][u], m_new))
                    state[2 * u], state[2 * u + 1] = m_new, jnp.exp2(m_old - m_new) * acc + pv
        return tuple(state)

    nfar = (units[-1][2] + FAR_TILES - 2) // FAR_TILES
    far = lax.fori_loop(0, nfar // 2, lambda p, carry: far_chunks([2 * p, 2 * p + 1], carry),
                        tuple(v for n in near for v in n[0]))
    far = lax.cond(nfar % 2 == 1, lambda carry: far_chunks([nfar - 1], carry),
                   lambda carry: carry, far)

    for u, (t, g, i) in enumerate(units):
        o_sel = _values(far[nstate * (u + 1) - 1])
        gates = gb_ref[0, g, _tile_rows(t)].T
        mix = []
        for r in range(REP):
            cols = slice(r * Q_TILE, (r + 1) * Q_TILE)
            mix.append(gates[3 * r:3 * r + 1] * o_cmp[u][:, cols]
                       + gates[3 * r + 1:3 * r + 2] * o_sel[:, cols]
                       + gates[3 * r + 2:3 * r + 3] * o_win[u][:, cols])
        _store_heads(jnp.concatenate(mix, axis=1), sz_ref, o_ref, t, g)


def _attn_b_kernel(par_ref, *refs, topk):
    _both_paths(par_ref, lambda bound: _attn_b_body(bound, *refs, topk=topk))


def _attn_b_call(par, qb, kcmp, vcmpt, ks, vst, kw, vwt, szb, gb, nbw, ovl):
    bsz, s, qwidth = qb.shape
    owidth = szb.shape[2]
    full = lambda a: pl.BlockSpec((1,) + a.shape[1:], lambda b, i: (b,) + (0,) * (a.ndim - 1))
    rows = NSA_STEP_TILES * Q_TILE
    return pl.pallas_call(
        functools.partial(_attn_b_kernel, topk=min(SEL_TOPK, s // SEL_LEN)),
        grid=(bsz, s // rows),
        in_specs=[pl.BlockSpec(memory_space=pltpu.SMEM),
                  pl.BlockSpec((1, rows, qwidth), lambda b, i: (b, i, 0)),
                  full(kcmp), full(vcmpt), full(ks), full(vst), full(kw), full(vwt),
                  pl.BlockSpec((1, rows, owidth), lambda b, i: (b, i, 0)),
                  pl.BlockSpec((1, KV_GROUPS, rows, LANES), lambda b, i: (b, 0, i, 0)),
                  pl.BlockSpec(nbw.shape, lambda b, i: (0, 0, 0)),
                  pl.BlockSpec(ovl.shape, lambda b, i: (0, 0))],
        out_specs=pl.BlockSpec((1, rows, owidth), lambda b, i: (b, i, 0)),
        out_shape=jax.ShapeDtypeStruct((bsz, s, owidth), MXU_DTYPE),
        scratch_shapes=[pltpu.VMEM((NSA_STEP_TILES * KV_GROUPS, LANES, Q_TILE), F32)],
        compiler_params=pltpu.CompilerParams(
            dimension_semantics=("arbitrary", "arbitrary"), vmem_limit_bytes=VMEM_LIMIT),
        name="attn_nsa",
    )(par, qb, kcmp, vcmpt, ks, vst, kw, vwt, szb, gb, nbw, ovl)


def _out_kernel(x_ref, mod_ref, ya_ref, yb_ref, w_ref, o_ref):
    half = ya_ref.shape[2]
    out = _dot(ya_ref[0], w_ref[0:half, :]) + _dot(yb_ref[0], w_ref[half:2 * half, :])
    o_ref[0] = x_ref[0] + mod_ref[0, 2:3, :] * out


def _out_call(x, mod3, ya, yb, w_out):
    bsz, s, d = x.shape
    tm = PROJ_TM
    xs = pl.BlockSpec((1, tm, d), lambda b, i: (b, i, 0))
    ys = pl.BlockSpec((1, tm, 512), lambda b, i: (b, i, 0))
    return pl.pallas_call(
        _out_kernel,
        grid=(bsz, s // tm),
        in_specs=[xs, pl.BlockSpec((1, 3, d), lambda b, i: (b, 0, 0)), ys, ys,
                  pl.BlockSpec(w_out.shape, lambda b, i: (0, 0))],
        out_specs=xs,
        out_shape=jax.ShapeDtypeStruct(x.shape, x.dtype),
        compiler_params=pltpu.CompilerParams(
            dimension_semantics=("arbitrary", "arbitrary"), vmem_limit_bytes=VMEM_LIMIT),
        name="out_proj_residual",
    )(x, mod3, ya, yb, w_out)


def _t5_bucket(dist):
    n = jnp.maximum(dist, 0)
    max_exact = N_BUCKETS // 2
    nf = jnp.maximum(n, 1).astype(F32)
    large = max_exact + (jnp.log(nf / max_exact) / math.log(MAX_DISTANCE / max_exact)
                         * (N_BUCKETS - max_exact)).astype(jnp.int32)
    large = jnp.minimum(large, N_BUCKETS - 1)
    return jnp.where(n < max_exact, n, large)


def _bias_table_kernel(rel_ref, idx_ref, idx_edge_ref, nba_ref, nbw_ref):
    h = pl.program_id(0)
    hb = h + pl.num_programs(0)

    def lookup(idx, head):
        acc = jnp.zeros(idx.shape, F32)
        for b in range(N_BUCKETS):
            acc = jnp.where(idx == b, rel_ref[b, head], acc)
        return acc

    idx = idx_ref[...]
    dist = (lax.broadcasted_iota(jnp.int32, idx.shape, 1) + Q_TILE
            - lax.broadcasted_iota(jnp.int32, idx.shape, 0))
    causal = dist >= 0
    far = rel_ref[N_BUCKETS - 1, hb]
    nba_ref[0] = jnp.where(causal, jnp.where(dist < SWA_WINDOW, lookup(idx, h) * LOG2E, NEG), NEG)
    near_b = jnp.where(causal, (lookup(idx, hb) - far) * LOG2E, NEG)
    idx_e = idx_edge_ref[...]
    dist_e = (lax.broadcasted_iota(jnp.int32, idx_e.shape, 1) + NSA_WINDOW
              - lax.broadcasted_iota(jnp.int32, idx_e.shape, 0))
    edge_b = jnp.where(dist_e < NSA_WINDOW, (lookup(idx_e, hb) - far) * LOG2E, NEG)
    nbw_ref[0] = jnp.concatenate([edge_b, near_b], axis=0)


def _near_tables(rel_bias):
    nheads = rel_bias.shape[1] // 2
    tq = jnp.arange(Q_TILE)[None, :]
    idx = _t5_bucket(tq + Q_TILE - jnp.arange(2 * Q_TILE)[:, None]).astype(jnp.int32)
    idx_edge = _t5_bucket(tq + NSA_WINDOW - jnp.arange(Q_TILE)[:, None]).astype(jnp.int32)
    return pl.pallas_call(
        _bias_table_kernel,
        grid=(nheads,),
        in_specs=[pl.BlockSpec(memory_space=pltpu.SMEM),
                  pl.BlockSpec(idx.shape, lambda h: (0, 0)),
                  pl.BlockSpec(idx_edge.shape, lambda h: (0, 0))],
        out_specs=[pl.BlockSpec((1, 2 * Q_TILE, Q_TILE), lambda h: (h // REP, 0, h % REP)),
                   pl.BlockSpec((1, 3 * Q_TILE, Q_TILE), lambda h: (h // REP, 0, h % REP))],
        out_shape=[jax.ShapeDtypeStruct((nheads // REP, 2 * Q_TILE, ROWS), F32),
                   jax.ShapeDtypeStruct((nheads // REP, 3 * Q_TILE, ROWS), F32)],
        name="t5_bias_tables",
    )(rel_bias, idx, idx_edge)


def _overlap_table(s, ncp):
    nc = (s - CMP_LEN) // CMP_STRIDE + 1
    ns = s // SEL_LEN
    c_lo = jnp.arange(ncp)[None, :] * CMP_STRIDE
    s_lo = jnp.arange(LANES)[:, None] * SEL_LEN
    ov = jnp.clip(jnp.minimum(c_lo + CMP_LEN, s_lo + SEL_LEN) - jnp.maximum(c_lo, s_lo), 0, None)
    ov = ov.astype(F32) / CMP_LEN
    ok = (jnp.arange(ncp)[None, :] < nc) & (jnp.arange(LANES)[:, None] < ns)
    return jnp.where(ok, ov, 0.0).astype(MXU_DTYPE)


def _compress_weights(w1, pos):
    hid = w1.shape[1]
    half = CMP_LEN // 2
    w1r = w1.reshape(CMP_LEN, HEAD_DIM, hid)
    eye = jnp.eye(KV_GROUPS, dtype=w1.dtype)
    expand = lambda w: jnp.einsum("ldj,gh->lgdhj", w, eye).reshape(
        half * KV_GROUPS * HEAD_DIM, KV_GROUPS * hid).astype(MXU_DTYPE)
    prow = lambda p: jnp.broadcast_to(p[:, None, :], (half, KV_GROUPS, HEAD_DIM)).reshape(1, -1)
    return expand(w1r[:half]), expand(w1r[half:]), prow(pos[:half]), prow(pos[half:])


def _logit_bound(q_gain, k_gains, bias, floor=None):
    gk = jnp.max(jnp.stack([jnp.max(jnp.abs(k)) for k in k_gains]))
    m = 1.02 * HEAD_DIM * jnp.max(jnp.abs(q_gain)) * gk + jnp.max(jnp.abs(bias))
    if floor is not None:
        m = jnp.maximum(m, jnp.max(floor))
    m = jnp.ceil(m).astype(F32)
    return jnp.stack([m, (m <= MAX_BOUND).astype(F32)])


def _upper_zero(row):
    return jnp.concatenate([row, jnp.zeros_like(row)]).reshape(1, LANES).astype(F32)


def _layer(x, c, w_ada, b_ada, norm_gain, w_in, b_nsa_gate, q_gain_a, k_gain_a, sinks, q_gain_b,
           k_gain_cmp, k_gain_sel, k_gain_win, cmp_pos_k, cmp_pos_v, w_cmp_k1, w_cmp_k2,
           w_cmp_v1, w_cmp_v2, w_out, rel_bias):
    bsz, s, d = x.shape
    assert s % (FAR_TILES * Q_TILE) == 0 and s // SEL_LEN <= HEAD_DIM and s // Q_TILE >= WIN_TILES
    assert FAR_TILES % NSA_STEP_TILES == 0 and s % (SWA_STEP_TILES * Q_TILE) == 0
    assert w_in.shape == (d, D_PROJ) and s % PROJ_TM == 0
    qscale = HEAD_DIM ** -0.5 * LOG2E

    mod3 = _mod_call(c, w_ada, b_ada).reshape(bsz, 3, d)
    w_in_p = jnp.pad(w_in, ((0, 0), (0, D_PROJ_PAD - D_PROJ))).astype(MXU_DTYPE)
    tile2 = lambda gn: jnp.concatenate([gn, gn]).astype(F32)
    gains = jnp.zeros((8, LANES), F32)
    for n, gn in enumerate((k_gain_a, k_gain_sel, k_gain_win, q_gain_a * qscale, q_gain_b * qscale)):
        gains = gains.at[n].set(tile2(gn))
    bgate = jnp.pad(b_nsa_gate, (0, LANES - b_nsa_gate.shape[0])).reshape(1, LANES).astype(F32)
    (qa, ka, vta, sza, qb, kc, vc, ks, vst, kw, vwt, szb, gb) = _proj_call(
        x, mod3, norm_gain.reshape(1, d).astype(F32), w_in_p, gains, bgate)

    ncp = s // CMP_STRIDE
    wkt, wkb, pkt, pkb = _compress_weights(w_cmp_k1, cmp_pos_k)
    wvt, wvb, pvt, pvb = _compress_weights(w_cmp_v1, cmp_pos_v)
    pos4 = jnp.concatenate([pkt, pkb, pvt, pvb], axis=0).astype(F32)
    pad2 = lambda w: jnp.pad(w, ((0, 0), (0, LANES - HEAD_DIM))).astype(MXU_DTYPE)
    half_block = CMP_STRIDE * LANES
    kcmp, vcmpt = _compress_call(kc.reshape(bsz, ncp, half_block), vc.reshape(bsz, ncp, half_block),
                                 wkt, wkb, wvt, wvb, pos4, pad2(w_cmp_k2), pad2(w_cmp_v2),
                                 _upper_zero(k_gain_cmp))

    nba, nbw = _near_tables(rel_bias.astype(F32))
    half = rel_bias.shape[1] // 2
    sinks2 = sinks.astype(F32) * LOG2E
    par_a = _logit_bound(q_gain_a * qscale, [k_gain_a], rel_bias[:, :half] * LOG2E, floor=sinks2)
    par_b = _logit_bound(q_gain_b * qscale, [k_gain_cmp, k_gain_sel, k_gain_win],
                         (rel_bias[:, half:] - rel_bias[N_BUCKETS - 1, half:]) * LOG2E)
    ya = _attn_a_call(par_a, sinks2, qa, ka, vta, sza, nba)
    yb = _attn_b_call(par_b, qb, kcmp, vcmpt, ks, vst, kw, vwt, szb, gb, nbw,
                      _overlap_table(s, ncp))
    return _out_call(x, mod3, ya, yb, w_out.astype(MXU_DTYPE))


def kernel(x, c, w_ada, b_ada, norm_gain, w_in, b_nsa_gate, q_gain_a, k_gain_a, sinks, q_gain_b,
           k_gain_cmp, k_gain_sel, k_gain_win, cmp_pos_k, cmp_pos_v, w_cmp_k1, w_cmp_k2,
           w_cmp_v1, w_cmp_v2, w_out, rel_bias):
    for l in range(w_ada.shape[0]):
        x = _layer(x, c, w_ada[l], b_ada[l], norm_gain[l], w_in[l], b_nsa_gate[l], q_gain_a[l],
                   k_gain_a[l], sinks[l], q_gain_b[l], k_gain_cmp[l], k_gain_sel[l],
                   k_gain_win[l], cmp_pos_k[l], cmp_pos_v[l], w_cmp_k1[l], w_cmp_k2[l],
                   w_cmp_v1[l], w_cmp_v2[l], w_out[l], rel_bias)
    return x
```

```python
import functools
import math

import jax
import jax.numpy as jnp
import numpy as np
from jax import lax
from jax.experimental import pallas as pl
from jax.experimental.pallas import tpu as pltpu

MXU_DTYPE = jnp.bfloat16
F32 = jnp.float32

HEAD_DIM = 64
LANES = 128
Q_TILE = 128
KV_GROUPS = 2
REP = 4
ROWS = REP * Q_TILE
FAR_TILES = 4
SWA_STEP_TILES = 8
NSA_STEP_TILES = 2
SWA_WINDOW = 128
NSA_WINDOW = 512
WIN_TILES = NSA_WINDOW // Q_TILE
CMP_LEN = 32
CMP_STRIDE = 16
SEL_LEN = 64
SEL_TOPK = 16
N_BUCKETS = 32
MAX_DISTANCE = 128
FORCE_BONUS = 1e4
EPS = 1e-6
NEG = -1e30
LOG2E = 1.4426950408889634
MAX_BOUND = 40.0
PROJ_TM = 512
OUT_TM = 1024
PROJ_SUBTILES = 2
VMEM_LIMIT = 48 * 1024 * 1024

_IMP_TAPS = tuple(
    (k, (min(CMP_STRIDE * k + CMP_LEN, SEL_LEN) - max(CMP_STRIDE * k, 0)) / CMP_LEN)
    for k in range(-(CMP_LEN // CMP_STRIDE) + 1, SEL_LEN // CMP_STRIDE))
_IMP_PAD = 8

OFF_QA, OFF_KA, OFF_VA, OFF_ZA = 0, 512, 640, 768
OFF_QB, OFF_KC, OFF_VC, OFF_KS, OFF_VS, OFF_KW, OFF_VW, OFF_ZB, OFF_GB = (
    1280, 1792, 1920, 2048, 2176, 2304, 2432, 2560, 3072)
D_PROJ = 3096
D_PROJ_PAD = 3200


def _dot(a, b):
    return jnp.dot(a, b, preferred_element_type=F32)


def _dot_nt(a, b):
    return lax.dot_general(a, b, (((1,), (1,)), ((), ())), preferred_element_type=F32)


def _split3(x):
    hi = x.astype(MXU_DTYPE)
    r1 = x - hi.astype(F32)
    mid = r1.astype(MXU_DTYPE)
    lo = (r1 - mid.astype(F32)).astype(MXU_DTYPE)
    return hi, mid, lo


def _mod_kernel(c_ref, w_ref, b_ref, o_ref):
    sc = jax.nn.silu(c_ref[...])
    w = w_ref[...]
    acc = jnp.zeros(o_ref.shape, F32)
    for a in _split3(sc):
        for b in _split3(w)[:2]:
            acc = acc + _dot(a, b)
    o_ref[...] = acc + b_ref[...]


def _mod_call(c, w_ada, b_ada):
    bsz, d = c.shape
    n = w_ada.shape[1]
    tn = 512
    return pl.pallas_call(
        _mod_kernel,
        grid=(n // tn,),
        in_specs=[pl.BlockSpec((bsz, d), lambda j: (0, 0)),
                  pl.BlockSpec((d, tn), lambda j: (0, j)),
                  pl.BlockSpec((1, tn), lambda j: (0, j))],
        out_specs=pl.BlockSpec((bsz, tn), lambda j: (0, j)),
        out_shape=jax.ShapeDtypeStruct((bsz, n), F32),
        name="adaln_mod",
    )(c, w_ada, b_ada.reshape(1, n))


def _proj_kernel(x_ref, mod_ref, gain_ref, w_ref, kg_ref, bg_ref,
                 qa_ref, ka_ref, va_ref, sza_ref, qb_ref, kc_ref, vc_ref,
                 ks_ref, vs_ref, kw_ref, vw_ref, szb_ref, gb_ref):
    tm = x_ref.shape[1]
    ts = tm // PROJ_SUBTILES
    si = pl.program_id(1)

    def normed(sub):
        x = x_ref[0, sub * ts:(sub + 1) * ts]
        ms = jnp.mean(x * x, axis=-1, keepdims=True)
        y = x * lax.rsqrt(ms + EPS) * gain_ref[...]
        h = y * (1.0 + mod_ref[0, 1:2, :]) + mod_ref[0, 0:1, :]
        return h.astype(MXU_DTYPE)

    hbs = [normed(sub) for sub in range(PROJ_SUBTILES)]
    lane = lax.broadcasted_iota(jnp.int32, (ts, LANES), 1)
    lo = lane < HEAD_DIM

    def half_norm(t, gain_row):
        sq = t * t
        s_lo = jnp.sum(jnp.where(lo, sq, 0.0), axis=-1, keepdims=True)
        s_hi = jnp.sum(jnp.where(lo, 0.0, sq), axis=-1, keepdims=True)
        inv = jnp.where(lo, lax.rsqrt(s_lo * (1.0 / HEAD_DIM) + EPS),
                        lax.rsqrt(s_hi * (1.0 / HEAD_DIM) + EPS))
        return t * inv * gain_row

    def split_heads(t, extra):
        return (jnp.where(lo, t, extra), jnp.where(lo, pltpu.roll(t, HEAD_DIM, axis=1), extra))

    for sub, hb in enumerate(hbs):
        rows = slice(sub * ts, (sub + 1) * ts)
        row = lax.broadcasted_iota(jnp.int32, (ts, LANES), 0) + si * tm + sub * ts
        onehot = jnp.where(lane - HEAD_DIM == row // SEL_LEN, 1.0, 0.0)

        def seg(off, n):
            return _dot(hb, w_ref[:, off:off + n])

        def write_q(ref, off, gain_row):
            t = seg(off, REP * KV_GROUPS * HEAD_DIM)
            for c in range(REP * KV_GROUPS // 2):
                pair = split_heads(half_norm(t[:, LANES * c:LANES * (c + 1)], gain_row), 0.0)
                for j in range(2):
                    col = LANES * (2 * c + j)
                    ref[0, rows, col:col + LANES] = pair[j].astype(ref.dtype)

        def write_kv(k_ref, vt_ref, off, gain_row):
            kv = seg(off, 2 * LANES)
            for g, t in enumerate(split_heads(half_norm(kv[:, :LANES], gain_row), onehot)):
                k_ref[0, g, rows] = t.astype(k_ref.dtype)
            if vt_ref.shape[3] > HEAD_DIM:
                vts = [t.T for t in split_heads(kv[:, LANES:], 1.0)]
            else:
                vt = kv[:, LANES:].T
                vts = [vt[HEAD_DIM * g:HEAD_DIM * (g + 1)] for g in range(KV_GROUPS)]
            for g in range(KV_GROUPS):
                for j in range(ts // Q_TILE):
                    vt_ref[0, g, sub * (ts // Q_TILE) + j] = (
                        vts[g][:, Q_TILE * j:Q_TILE * (j + 1)].astype(vt_ref.dtype))

        write_q(qa_ref, OFF_QA, kg_ref[3:4, :])
        write_kv(ka_ref, va_ref, OFF_KA, kg_ref[0:1, :])
        sza_ref[0, rows] = jax.nn.silu(seg(OFF_ZA, 512)).astype(sza_ref.dtype)
        write_q(qb_ref, OFF_QB, kg_ref[4:5, :])
        kvc = seg(OFF_KC, 2 * LANES)
        kc_ref[0, rows] = kvc[:, :LANES].astype(kc_ref.dtype)
        vc_ref[0, rows] = kvc[:, LANES:].astype(vc_ref.dtype)
        write_kv(ks_ref, vs_ref, OFF_KS, kg_ref[1:2, :])
        write_kv(kw_ref, vw_ref, OFF_KW, kg_ref[2:3, :])
        szb_ref[0, rows] = jax.nn.silu(seg(OFF_ZB, 512)).astype(szb_ref.dtype)
        gates = jax.nn.sigmoid(seg(OFF_GB, LANES) + bg_ref[...])
        gb_ref[0, 0, rows] = gates
        gb_ref[0, 1, rows] = pltpu.roll(gates, LANES - REP * 3, axis=1)


def _proj_call(x, mod3, norm_gain, w_in_p, gains, bgate):
    bsz, s, d = x.shape
    tm = PROJ_TM
    dt = MXU_DTYPE
    nt = tm // Q_TILE
    qwidth = REP * KV_GROUPS * LANES
    rowq = pl.BlockSpec((1, tm, qwidth), lambda b, i: (b, i, 0))
    row512 = pl.BlockSpec((1, tm, 512), lambda b, i: (b, i, 0))
    row128 = pl.BlockSpec((1, tm, LANES), lambda b, i: (b, i, 0))
    grp = pl.BlockSpec((1, KV_GROUPS, tm, LANES), lambda b, i: (b, 0, i, 0))
    grpt = pl.BlockSpec((1, KV_GROUPS, nt, HEAD_DIM, Q_TILE), lambda b, i: (b, 0, i, 0, 0))
    grpt1 = pl.BlockSpec((1, KV_GROUPS, nt, LANES, Q_TILE), lambda b, i: (b, 0, i, 0, 0))
    sgrpt1 = jax.ShapeDtypeStruct((bsz, KV_GROUPS, s // Q_TILE, LANES, Q_TILE), dt)
    sq = jax.ShapeDtypeStruct((bsz, s, qwidth), dt)
    s512 = jax.ShapeDtypeStruct((bsz, s, 512), dt)
    s128 = jax.ShapeDtypeStruct((bsz, s, LANES), dt)
    sgrp = jax.ShapeDtypeStruct((bsz, KV_GROUPS, s, LANES), dt)
    sgrpt = jax.ShapeDtypeStruct((bsz, KV_GROUPS, s // Q_TILE, HEAD_DIM, Q_TILE), dt)
    return pl.pallas_call(
        _proj_kernel,
        grid=(bsz, s // tm),
        in_specs=[pl.BlockSpec((1, tm, d), lambda b, i: (b, i, 0)),
                  pl.BlockSpec((1, 3, d), lambda b, i: (b, 0, 0)),
                  pl.BlockSpec((1, d), lambda b, i: (0, 0)),
                  pl.BlockSpec((d, D_PROJ_PAD), lambda b, i: (0, 0)),
                  pl.BlockSpec((8, LANES), lambda b, i: (0, 0)),
                  pl.BlockSpec((1, LANES), lambda b, i: (0, 0))],
        out_specs=[rowq, grp, grpt1, row512, rowq, row128, row128, grp, grpt, grp, grpt, row512, grp],
        out_shape=[sq, sgrp, sgrpt1, s512, sq, s128, s128, sgrp, sgrpt, sgrp, sgrpt, s512,
                   jax.ShapeDtypeStruct((bsz, KV_GROUPS, s, LANES), F32)],
        compiler_params=pltpu.CompilerParams(
            dimension_semantics=("arbitrary", "arbitrary"), vmem_limit_bytes=VMEM_LIMIT),
        name="norm_in_proj",
    )(x, mod3, norm_gain, w_in_p, gains, bgate)


def _compress_kernel(kc_ref, vc_ref, wkt_ref, wkb_ref, wvt_ref, wvb_ref, pos_ref, w2k_ref, w2v_ref,
                     kg_ref, ko_ref, vo_ref):
    ncp = kc_ref.shape[1]

    def hidden(h_ref, wt_ref, wb_ref, ptop, pbot):
        hf = h_ref[0].astype(F32)
        top = _dot((hf + ptop).astype(MXU_DTYPE), wt_ref[...])
        bot = _dot((hf + pbot).astype(MXU_DTYPE), wb_ref[...])
        pre = top + pltpu.roll(bot, ncp - 1, axis=0)
        return jax.nn.silu(pre).astype(MXU_DTYPE)

    hk = hidden(kc_ref, wkt_ref, wkb_ref, pos_ref[0:1, :], pos_ref[1:2, :])
    hv = hidden(vc_ref, wvt_ref, wvb_ref, pos_ref[2:3, :], pos_ref[3:4, :])
    nh = w2k_ref.shape[0]
    for g in range(KV_GROUPS):
        k = _dot(hk[:, g * nh:(g + 1) * nh], w2k_ref[...])
        ss = jnp.sum(k * k, axis=-1, keepdims=True) * (1.0 / HEAD_DIM)
        ko_ref[0, g] = (k * lax.rsqrt(ss + EPS) * kg_ref[...]).astype(ko_ref.dtype)
        v = _dot(hv[:, g * nh:(g + 1) * nh], w2v_ref[...])
        vo_ref[0, g] = v.T[:HEAD_DIM].astype(vo_ref.dtype)


def _compress_call(kc_r, vc_r, wkt, wkb, wvt, wvb, pos4, w2k, w2v, kgain):
    bsz, ncp, width = kc_r.shape
    full = lambda a: pl.BlockSpec(a.shape, lambda b: (0,) * a.ndim)
    return pl.pallas_call(
        _compress_kernel,
        grid=(bsz,),
        in_specs=[pl.BlockSpec((1, ncp, width), lambda b: (b, 0, 0)),
                  pl.BlockSpec((1, ncp, width), lambda b: (b, 0, 0)),
                  full(wkt), full(wkb), full(wvt), full(wvb), full(pos4), full(w2k), full(w2v),
                  full(kgain)],
        out_specs=[pl.BlockSpec((1, KV_GROUPS, ncp, LANES), lambda b: (b, 0, 0, 0)),
                   pl.BlockSpec((1, KV_GROUPS, HEAD_DIM, ncp), lambda b: (b, 0, 0, 0))],
        out_shape=[jax.ShapeDtypeStruct((bsz, KV_GROUPS, ncp, LANES), MXU_DTYPE),
                   jax.ShapeDtypeStruct((bsz, KV_GROUPS, HEAD_DIM, ncp), MXU_DTYPE)],
        compiler_params=pltpu.CompilerParams(
            dimension_semantics=("arbitrary",), vmem_limit_bytes=VMEM_LIMIT),
        name="nsa_compress",
    )(kc_r, vc_r, wkt, wkb, wvt, wvb, pos4, w2k, w2v, kgain)


def _tile_rows(t):
    return slice(t * Q_TILE, (t + 1) * Q_TILE)


def _stacked_queries(q_ref, t, g):
    return jnp.concatenate(
        [q_ref[0, _tile_rows(t), LANES * (REP * g + r):LANES * (REP * g + r + 1)]
         for r in range(REP)], axis=0)


def _with_mask(qst, mask):
    lo = lax.broadcasted_iota(jnp.int32, (1, LANES), 1) < HEAD_DIM
    return jnp.where(lo, qst, mask.astype(qst.dtype))


def _open(bound):
    return 0.0 if bound is None else -bound


def _range_mask(first_tile, last_tile, bound):
    blk = lax.broadcasted_iota(jnp.int32, (1, LANES), 1) - HEAD_DIM
    return jnp.where(blk < 2 * first_tile, NEG,
                     jnp.where(blk > 2 * last_tile + 1, NEG, _open(bound)))


def _k_window(ref, g, first_tile, ntiles):
    start = pl.multiple_of(jnp.maximum(first_tile, 0) * Q_TILE, Q_TILE)
    return ref[0, g, pl.ds(start, ntiles * Q_TILE), :]


def _vt_window(ref, g, first_tile, ntiles):
    start = jnp.maximum(first_tile, 0)
    return jnp.concatenate([ref[0, g, start + t] for t in range(ntiles)], axis=1)


def _near_table(nb, first):
    diag = nb[Q_TILE:]
    return jnp.concatenate([jnp.where(first, diag, nb[:Q_TILE]), diag], axis=0)


def _colmax(s):
    return jnp.max(s, axis=0, keepdims=True)


def _empty_state(bound, cols):
    zero = (jnp.zeros((1, cols), F32), jnp.zeros((HEAD_DIM, cols), F32))
    return zero if bound is not None else (jnp.full((1, cols), NEG, F32),) + zero


def _accumulate(state, s, vt, bound):
    if vt.shape[0] > HEAD_DIM:
        colsum = lambda e: 0.0
    else:
        colsum = lambda e: jnp.sum(e, axis=0, keepdims=True)
    if bound is not None:
        l, acc = state
        e = jnp.exp2(s)
        return l + colsum(e), acc + _dot(vt, e.astype(MXU_DTYPE))
    m, l, acc = state
    m_new = jnp.maximum(m, _colmax(s))
    alpha = jnp.exp2(m - m_new)
    e = jnp.exp2(s - m_new)
    return m_new, alpha * l + colsum(e), alpha * acc + _dot(vt, e.astype(MXU_DTYPE))


def _values(state):
    l, acc = state[-2], state[-1]
    if acc.shape[0] > HEAD_DIM:
        l = l + acc[HEAD_DIM:HEAD_DIM + 1]
    return acc[:HEAD_DIM] / l


def _store_heads(o_t, sz_ref, o_ref, t, g):
    rows = _tile_rows(t)
    for c in range(REP // 2):
        pair = jnp.concatenate([o_t[:, Q_TILE * (2 * c):Q_TILE * (2 * c + 1)],
                                o_t[:, Q_TILE * (2 * c + 1):Q_TILE * (2 * c + 2)]], axis=0)
        cols = slice(LANES * (2 * g + c), LANES * (2 * g + c + 1))
        o_ref[0, rows, cols] = (pair.T * sz_ref[0, rows, cols].astype(F32)).astype(o_ref.dtype)


def _both_paths(par_ref, body):
    bounded = par_ref[1] > 0.5
    pl.when(bounded)(lambda: body(par_ref[0]))
    pl.when(jnp.logical_not(bounded))(lambda: body(None))


def _attn_a_body(bound, sinks_ref, q_ref, k_ref, vt_ref, sz_ref, nb_ref, o_ref):
    tiles = q_ref.shape[1] // Q_TILE
    units = [(t, g) for t in range(tiles) for g in range(KV_GROUPS)]
    tile = lambda t: pl.program_id(1) * tiles + t
    sinks = [jnp.concatenate([jnp.full((1, Q_TILE), sinks_ref[REP * g + r], F32)
                              for r in range(REP)], axis=1) for g in range(KV_GROUPS)]
    scores = []
    for t, g in units:
        i = tile(t)
        q = _with_mask(_stacked_queries(q_ref, t, g), _range_mask(i - 1, i, bound))
        nb = _near_table(nb_ref[g], i == 0)
        scores.append(_dot_nt(_k_window(k_ref, g, i - 1, 2), q) + nb)
    outs = []
    for (t, g), s in zip(units, scores):
        zeros = jnp.zeros((vt_ref.shape[3], ROWS), F32)
        if bound is not None:
            state = (jnp.exp2(sinks[g] - bound), zeros)
        else:
            state = (sinks[g], jnp.ones((1, ROWS), F32), zeros)
        outs.append(_values(_accumulate(state, s, _vt_window(vt_ref, g, tile(t) - 1, 2), bound)))
    for (t, g), o in zip(units, outs):
        _store_heads(o, sz_ref, o_ref, t, g)


def _attn_a_kernel(par_ref, *refs):
    _both_paths(par_ref, lambda bound: _attn_a_body(bound, *refs))


def _attn_a_call(par, sinks, qa, ka, vta, sza, nba):
    bsz, s, qwidth = qa.shape
    owidth = sza.shape[2]
    full = lambda a: pl.BlockSpec((1,) + a.shape[1:], lambda b, i: (b,) + (0,) * (a.ndim - 1))
    smem = pl.BlockSpec(memory_space=pltpu.SMEM)
    rows = SWA_STEP_TILES * Q_TILE
    return pl.pallas_call(
        _attn_a_kernel,
        grid=(bsz, s // rows),
        in_specs=[smem, smem,
                  pl.BlockSpec((1, rows, qwidth), lambda b, i: (b, i, 0)),
                  full(ka), full(vta),
                  pl.BlockSpec((1, rows, owidth), lambda b, i: (b, i, 0)),
                  pl.BlockSpec(nba.shape, lambda b, i: (0, 0, 0))],
        out_specs=pl.BlockSpec((1, rows, owidth), lambda b, i: (b, i, 0)),
        out_shape=jax.ShapeDtypeStruct((bsz, s, owidth), MXU_DTYPE),
        compiler_params=pltpu.CompilerParams(
            dimension_semantics=("arbitrary", "arbitrary"), vmem_limit_bytes=VMEM_LIMIT),
        name="attn_swa_sink",
    )(par, sinks, qa, ka, vta, sza, nba)


def _nsa_scores(t, g, i, bound, q_ref, kc_ref, kw_ref, nb_ref):
    qst = _stacked_queries(q_ref, t, g)
    q_near = _with_mask(qst, _range_mask(i - 1, i, bound))
    nb = nb_ref[g]
    nb_near = _near_table(nb[Q_TILE:], i == 0)
    sc = _dot_nt(kc_ref[0, g], q_near)
    q_wfar = _with_mask(qst, _range_mask(i - WIN_TILES, i - 2, bound))
    s_wf = _dot_nt(_k_window(kw_ref, g, i - WIN_TILES, WIN_TILES - 1), q_wfar)
    edge = jnp.where(i >= WIN_TILES, nb[:Q_TILE], 0.0)
    s_wf = jnp.concatenate([s_wf[:Q_TILE] + edge, s_wf[Q_TILE:]], axis=0)
    s_wn = _dot_nt(_k_window(kw_ref, g, i - 1, 2), q_near) + nb_near
    return qst, nb_near, sc, s_wf, s_wn


def _nsa_compressed(g, i, bound, sc, vct_ref, psum_ref, score_out):
    ncp = sc.shape[0]
    tok = i * Q_TILE + (lax.broadcasted_iota(jnp.int32, (1, ROWS), 1) & (Q_TILE - 1))
    last_visible = (tok - (CMP_LEN - 1)) // CMP_STRIDE
    z = jnp.where(lax.broadcasted_iota(jnp.int32, (ncp, ROWS), 0) <= last_visible, sc, NEG)
    if bound is None:
        m = _colmax(z)
        m = jnp.where(m > 0.5 * NEG, m, 0.0)
    else:
        m = bound
    e = jnp.exp2(z - m)
    p = e * (1.0 / jnp.maximum(jnp.sum(e, axis=0, keepdims=True), 1e-30))
    o_cmp = _dot(vct_ref[0, g], p.astype(MXU_DTYPE))
    psum_ref[0:_IMP_PAD] = jnp.zeros((_IMP_PAD, Q_TILE), F32)
    psum_ref[_IMP_PAD:_IMP_PAD + ncp] = sum(p[:, r * Q_TILE:(r + 1) * Q_TILE] for r in range(REP))
    if psum_ref.shape[0] > _IMP_PAD + ncp:
        psum_ref[_IMP_PAD + ncp:] = jnp.zeros((psum_ref.shape[0] - _IMP_PAD - ncp, Q_TILE), F32)
    imp = sum(w * psum_ref[pl.ds(_IMP_PAD + k, SEL_LEN, stride=SEL_LEN // CMP_STRIDE), :]
              for k, w in _IMP_TAPS)

    blk = lax.broadcasted_iota(jnp.int32, (SEL_LEN, Q_TILE), 0)
    cur = (i * Q_TILE + lax.broadcasted_iota(jnp.int32, (SEL_LEN, Q_TILE), 1)) // SEL_LEN
    bonus = jnp.where(blk == 0, FORCE_BONUS,
                      jnp.where(blk == cur, FORCE_BONUS,
                                jnp.where(blk == cur - 1, FORCE_BONUS, 0.0)))
    score_out[...] = jnp.where(blk <= cur, imp + bonus, NEG)
    return o_cmp


def _nsa_window(g, i, bound, s_wf, s_wn, vwt_ref):
    state = _accumulate(_empty_state(bound, ROWS), s_wf,
                        _vt_window(vwt_ref, g, i - WIN_TILES, WIN_TILES - 1), bound)
    return _values(_accumulate(state, s_wn, _vt_window(vwt_ref, g, i - 1, 2), bound))


def _nsa_select(i, bound, score_in, topk, nslab):
    sub = lax.broadcasted_iota(jnp.int32, (8, Q_TILE), 0)
    cur_t = (i * Q_TILE + lax.broadcasted_iota(jnp.int32, (8, Q_TILE), 1)) // SEL_LEN
    slabs = [score_in[8 * v:8 * (v + 1), :] for v in range(nslab)]
    cnts = [jnp.zeros((8, Q_TILE), F32) for _ in range(nslab)]
    for jp in range(8 * nslab):
        rowv = score_in[jp:jp + 1, :]
        for v in range(nslab):
            ge = jnp.where(rowv >= slabs[v], 1.0, 0.0)
            gt = jnp.where(rowv > slabs[v], 1.0, 0.0)
            if 8 * v > jp:
                beats = ge
            elif 8 * v + 7 < jp:
                beats = gt
            else:
                beats = jnp.where(sub + 8 * v > jp, ge, gt)
            cnts[v] = cnts[v] + beats
    sel_t = [jnp.where(cnts[v] < topk, jnp.where(sub + 8 * v <= cur_t, _open(bound), NEG), NEG)
             for v in range(nslab)]
    closed = lambda rows: [jnp.full((rows, Q_TILE), NEG, F32)] if rows else []
    sel_t = jnp.concatenate(
        closed(LANES - SEL_LEN) + sel_t + closed(SEL_LEN - 8 * nslab), axis=0)
    return jnp.concatenate([sel_t.T] * REP, axis=0)


def _nsa_select_all(units, bound, score_ref, topk):
    last_tile = units[-1][2]
    step_slabs = 2
    levels = SEL_LEN // 8 // step_slabs
    level = (2 * last_tile + 1) // (8 * step_slabs)

    def branch(k):
        return lambda: tuple(_nsa_select(i, bound, score_ref.at[u], topk, step_slabs * (k + 1))
                             for u, (t, g, i) in enumerate(units))

    return lax.switch(jnp.minimum(level, levels - 1), [branch(k) for k in range(levels)])


def _nsa_selected_queries(i, qst, selm):
    blk_row = lax.broadcasted_iota(jnp.int32, (1, LANES), 1) - HEAD_DIM
    return _with_mask(qst, selm), _with_mask(qst, jnp.where(blk_row >= 2 * (i - 1), NEG, selm))


def _attn_b_body(bound, q_ref, kc_ref, vct_ref, ks_ref, vst_ref, kw_ref, vwt_ref, sz_ref, gb_ref,
                 nb_ref, o_ref, score_ref, psum_ref, *, topk):
    tiles = q_ref.shape[1] // Q_TILE
    units = [(t, g, pl.program_id(1) * tiles + t) for t in range(tiles) for g in range(KV_GROUPS)]
    scores = [_nsa_scores(t, g, i, bound, q_ref, kc_ref, kw_ref, nb_ref) for t, g, i in units]
    o_cmp = [_nsa_compressed(g, i, bound, scores[u][2], vct_ref, psum_ref.at[u], score_ref.at[u])
             for u, (t, g, i) in enumerate(units)]
    o_win = [_nsa_window(g, i, bound, scores[u][3], scores[u][4], vwt_ref)
             for u, (t, g, i) in enumerate(units)]
    selm = _nsa_select_all(units, bound, score_ref, topk)
    qs = [_nsa_selected_queries(i, scores[u][0], selm[u]) for u, (t, g, i) in enumerate(units)]

    def far_chunks(chunks, carry):
        s_c = [[_dot_nt(_k_window(ks_ref, g, c * FAR_TILES, FAR_TILES), qs[u][1])
                for u, (t, g, i) in enumerate(units)] for c in chunks]
        states = [carry[nstate * u:nstate * (u + 1)] for u in range(len(units))]
        for n, c in enumerate(chunks):
            for u, (t, g, i) in enumerate(units):
                states[u] = _accumulate(states[u], s_c[n][u],
                                        _vt_window(vst_ref, g, c * FAR_TILES, FAR_TILES), bound)
        return tuple(v for st in states for v in st)

    empty = _empty_state(bound, ROWS)
    nstate = len(empty)
    nfar = (units[-1][2] + FAR_TILES - 2) // FAR_TILES
    far = lax.fori_loop(0, nfar // 2, lambda p, carry: far_chunks([2 * p, 2 * p + 1], carry),
                        empty * len(units))
    far = lax.cond(nfar % 2 == 1, lambda carry: far_chunks([nfar - 1], carry),
                   lambda carry: carry, far)

    s_near = [_dot_nt(_k_window(ks_ref, g, i - 1, 2), qs[u][0]) + scores[u][1]
              for u, (t, g, i) in enumerate(units)]
    for u, (t, g, i) in enumerate(units):
        o_sel = _values(_accumulate(far[nstate * u:nstate * (u + 1)], s_near[u],
                                    _vt_window(vst_ref, g, i - 1, 2), bound))
        gates = gb_ref[0, g, _tile_rows(t)].T
        mix = []
        for r in range(REP):
            cols = slice(r * Q_TILE, (r + 1) * Q_TILE)
            mix.append(gates[3 * r:3 * r + 1] * o_cmp[u][:, cols]
                       + gates[3 * r + 1:3 * r + 2] * o_sel[:, cols]
                       + gates[3 * r + 2:3 * r + 3] * o_win[u][:, cols])
        _store_heads(jnp.concatenate(mix, axis=1), sz_ref, o_ref, t, g)


def _attn_b_kernel(par_ref, *refs, topk):
    _both_paths(par_ref, lambda bound: _attn_b_body(bound, *refs, topk=topk))


def _attn_b_call(par, qb, kcmp, vcmpt, ks, vst, kw, vwt, szb, gb, nbw):
    bsz, s, qwidth = qb.shape
    owidth = szb.shape[2]
    full = lambda a: pl.BlockSpec((1,) + a.shape[1:], lambda b, i: (b,) + (0,) * (a.ndim - 1))
    rows = NSA_STEP_TILES * Q_TILE
    chains = NSA_STEP_TILES * KV_GROUPS
    imp_rows = HEAD_DIM * (SEL_LEN // CMP_STRIDE)
    return pl.pallas_call(
        functools.partial(_attn_b_kernel, topk=min(SEL_TOPK, s // SEL_LEN)),
        grid=(bsz, s // rows),
        in_specs=[pl.BlockSpec(memory_space=pltpu.SMEM),
                  pl.BlockSpec((1, rows, qwidth), lambda b, i: (b, i, 0)),
                  full(kcmp), full(vcmpt), full(ks), full(vst), full(kw), full(vwt),
                  pl.BlockSpec((1, rows, owidth), lambda b, i: (b, i, 0)),
                  pl.BlockSpec((1, KV_GROUPS, rows, LANES), lambda b, i: (b, 0, i, 0)),
                  pl.BlockSpec(nbw.shape, lambda b, i: (0, 0, 0))],
        out_specs=pl.BlockSpec((1, rows, owidth), lambda b, i: (b, i, 0)),
        out_shape=jax.ShapeDtypeStruct((bsz, s, owidth), MXU_DTYPE),
        scratch_shapes=[pltpu.VMEM((chains, SEL_LEN, Q_TILE), F32),
                        pltpu.VMEM((chains, 2 * _IMP_PAD + max(kcmp.shape[2], imp_rows), Q_TILE), F32)],
        compiler_params=pltpu.CompilerParams(
            dimension_semantics=("arbitrary", "arbitrary"), vmem_limit_bytes=VMEM_LIMIT),
        name="attn_nsa",
    )(par, qb, kcmp, vcmpt, ks, vst, kw, vwt, szb, gb, nbw)


def _out_kernel(x_ref, mod_ref, ya_ref, yb_ref, w_ref, o_ref):
    half = ya_ref.shape[2]
    out = _dot(ya_ref[0], w_ref[0:half, :]) + _dot(yb_ref[0], w_ref[half:2 * half, :])
    o_ref[0] = x_ref[0] + mod_ref[0, 2:3, :] * out


def _out_call(x, mod3, ya, yb, w_out):
    bsz, s, d = x.shape
    tm = OUT_TM
    xs = pl.BlockSpec((1, tm, d), lambda b, i: (b, i, 0))
    ys = pl.BlockSpec((1, tm, 512), lambda b, i: (b, i, 0))
    return pl.pallas_call(
        _out_kernel,
        grid=(bsz, s // tm),
        in_specs=[xs, pl.BlockSpec((1, 3, d), lambda b, i: (b, 0, 0)), ys, ys,
                  pl.BlockSpec(w_out.shape, lambda b, i: (0, 0))],
        out_specs=xs,
        out_shape=jax.ShapeDtypeStruct(x.shape, x.dtype),
        compiler_params=pltpu.CompilerParams(
            dimension_semantics=("arbitrary", "arbitrary"), vmem_limit_bytes=VMEM_LIMIT),
        name="out_proj_residual",
    )(x, mod3, ya, yb, w_out)


def _t5_bucket(dist):
    n = np.maximum(dist, 0)
    max_exact = N_BUCKETS // 2
    nf = np.maximum(n, 1).astype(np.float32)
    ratio = np.log(nf / np.float32(max_exact)) / np.float32(math.log(MAX_DISTANCE / max_exact))
    large = max_exact + (ratio * np.float32(N_BUCKETS - max_exact)).astype(np.int32)
    large = np.minimum(large, N_BUCKETS - 1)
    return np.where(n < max_exact, n, large).astype(np.int32)


def _bias_table_kernel(rel_ref, idx_ref, idx_edge_ref, nba_ref, nbw_ref):
    h = pl.program_id(0)
    hb = h + pl.num_programs(0)

    def lookup(idx, head):
        acc = jnp.zeros(idx.shape, F32)
        for b in range(N_BUCKETS):
            acc = jnp.where(idx == b, rel_ref[b, head], acc)
        return acc

    idx = idx_ref[...]
    dist = (lax.broadcasted_iota(jnp.int32, idx.shape, 1) + Q_TILE
            - lax.broadcasted_iota(jnp.int32, idx.shape, 0))
    causal = dist >= 0
    far = rel_ref[N_BUCKETS - 1, hb]
    nba_ref[0] = jnp.where(causal, jnp.where(dist < SWA_WINDOW, lookup(idx, h) * LOG2E, NEG), NEG)
    near_b = jnp.where(causal, (lookup(idx, hb) - far) * LOG2E, NEG)
    idx_e = idx_edge_ref[...]
    dist_e = (lax.broadcasted_iota(jnp.int32, idx_e.shape, 1) + NSA_WINDOW
              - lax.broadcasted_iota(jnp.int32, idx_e.shape, 0))
    edge_b = jnp.where(dist_e < NSA_WINDOW, (lookup(idx_e, hb) - far) * LOG2E, NEG)
    nbw_ref[0] = jnp.concatenate([edge_b, near_b], axis=0)


def _near_tables(rel_bias):
    nheads = rel_bias.shape[1] // 2
    tq = np.arange(Q_TILE)[None, :]
    idx = _t5_bucket(tq + Q_TILE - np.arange(2 * Q_TILE)[:, None])
    idx_edge = _t5_bucket(tq + NSA_WINDOW - np.arange(Q_TILE)[:, None])
    return pl.pallas_call(
        _bias_table_kernel,
        grid=(nheads,),
        in_specs=[pl.BlockSpec(memory_space=pltpu.SMEM),
                  pl.BlockSpec(idx.shape, lambda h: (0, 0)),
                  pl.BlockSpec(idx_edge.shape, lambda h: (0, 0))],
        out_specs=[pl.BlockSpec((1, 2 * Q_TILE, Q_TILE), lambda h: (h // REP, 0, h % REP)),
                   pl.BlockSpec((1, 3 * Q_TILE, Q_TILE), lambda h: (h // REP, 0, h % REP))],
        out_shape=[jax.ShapeDtypeStruct((nheads // REP, 2 * Q_TILE, ROWS), F32),
                   jax.ShapeDtypeStruct((nheads // REP, 3 * Q_TILE, ROWS), F32)],
        name="t5_bias_tables",
    )(rel_bias, idx, idx_edge)


def _compress_weights(w1, pos):
    hid = w1.shape[1]
    half = CMP_LEN // 2
    w1r = w1.reshape(CMP_LEN, HEAD_DIM, hid)
    eye = jnp.eye(KV_GROUPS, dtype=w1.dtype)
    expand = lambda w: jnp.einsum("ldj,gh->lgdhj", w, eye).reshape(
        half * KV_GROUPS * HEAD_DIM, KV_GROUPS * hid).astype(MXU_DTYPE)
    prow = lambda p: jnp.broadcast_to(p[:, None, :], (half, KV_GROUPS, HEAD_DIM)).reshape(1, -1)
    return expand(w1r[:half]), expand(w1r[half:]), prow(pos[:half]), prow(pos[half:])


def _logit_bound(q_gain, k_gains, bias, floor=None):
    gk = jnp.max(jnp.stack([jnp.max(jnp.abs(k)) for k in k_gains]))
    m = 1.02 * HEAD_DIM * jnp.max(jnp.abs(q_gain)) * gk + jnp.max(jnp.abs(bias))
    if floor is not None:
        m = jnp.maximum(m, jnp.max(floor))
    m = jnp.ceil(m).astype(F32)
    return jnp.stack([m, (m <= MAX_BOUND).astype(F32)])


def _upper_zero(row):
    return jnp.concatenate([row, jnp.zeros_like(row)]).reshape(1, LANES).astype(F32)


def _layer(x, c, w_ada, b_ada, norm_gain, w_in, b_nsa_gate, q_gain_a, k_gain_a, sinks, q_gain_b,
           k_gain_cmp, k_gain_sel, k_gain_win, cmp_pos_k, cmp_pos_v, w_cmp_k1, w_cmp_k2,
           w_cmp_v1, w_cmp_v2, w_out, rel_bias):
    bsz, s, d = x.shape
    assert s % (FAR_TILES * Q_TILE) == 0 and s // SEL_LEN <= HEAD_DIM and s // Q_TILE >= WIN_TILES
    assert FAR_TILES % NSA_STEP_TILES == 0 and s % (SWA_STEP_TILES * Q_TILE) == 0
    assert w_in.shape == (d, D_PROJ) and s % PROJ_TM == 0 and s % OUT_TM == 0
    qscale = HEAD_DIM ** -0.5 * LOG2E

    mod3 = _mod_call(c, w_ada, b_ada).reshape(bsz, 3, d)
    w_in_p = jnp.pad(w_in, ((0, 0), (0, D_PROJ_PAD - D_PROJ))).astype(MXU_DTYPE)
    tile2 = lambda gn: jnp.concatenate([gn, gn]).astype(F32)
    gains = jnp.zeros((8, LANES), F32)
    for n, gn in enumerate((k_gain_a, k_gain_sel, k_gain_win, q_gain_a * qscale, q_gain_b * qscale)):
        gains = gains.at[n].set(tile2(gn))
    bgate = jnp.pad(b_nsa_gate, (0, LANES - b_nsa_gate.shape[0])).reshape(1, LANES).astype(F32)
    (qa, ka, vta, sza, qb, kc, vc, ks, vst, kw, vwt, szb, gb) = _proj_call(
        x, mod3, norm_gain.reshape(1, d).astype(F32), w_in_p, gains, bgate)

    ncp = s // CMP_STRIDE
    wkt, wkb, pkt, pkb = _compress_weights(w_cmp_k1, cmp_pos_k)
    wvt, wvb, pvt, pvb = _compress_weights(w_cmp_v1, cmp_pos_v)
    pos4 = jnp.concatenate([pkt, pkb, pvt, pvb], axis=0).astype(F32)
    pad2 = lambda w: jnp.pad(w, ((0, 0), (0, LANES - HEAD_DIM))).astype(MXU_DTYPE)
    half_block = CMP_STRIDE * LANES
    kcmp, vcmpt = _compress_call(kc.reshape(bsz, ncp, half_block), vc.reshape(bsz, ncp, half_block),
                                 wkt, wkb, wvt, wvb, pos4, pad2(w_cmp_k2), pad2(w_cmp_v2),
                                 _upper_zero(k_gain_cmp))

    nba, nbw = _near_tables(rel_bias.astype(F32))
    half = rel_bias.shape[1] // 2
    sinks2 = sinks.astype(F32) * LOG2E
    par_a = _logit_bound(q_gain_a * qscale, [k_gain_a], rel_bias[:, :half] * LOG2E, floor=sinks2)
    par_b = _logit_bound(q_gain_b * qscale, [k_gain_cmp, k_gain_sel, k_gain_win],
                         (rel_bias[:, half:] - rel_bias[N_BUCKETS - 1, half:]) * LOG2E)
    ya = _attn_a_call(par_a, sinks2, qa, ka, vta, sza, nba)
    yb = _attn_b_call(par_b, qb, kcmp, vcmpt, ks, vst, kw, vwt, szb, gb, nbw)
    return _out_call(x, mod3, ya, yb, w_out.astype(MXU_DTYPE))


def kernel(x, c, w_ada, b_ada, norm_gain, w_in, b_nsa_gate, q_gain_a, k_gain_a, sinks, q_gain_b,
           k_gain_cmp, k_gain_sel, k_gain_win, cmp_pos_k, cmp_pos_v, w_cmp_k1, w_cmp_k2,
           w_cmp_v1, w_cmp_v2, w_out, rel_bias):
    for l in range(w_ada.shape[0]):
        x = _layer(x, c, w_ada[l], b_ada[l], norm_gain[l], w_in[l], b_nsa_gate[l], q_gain_a[l],
                   k_gain_a[l], sinks[l], q_gain_b[l], k_gain_cmp[l], k_gain_sel[l],
                   k_gain_win[l], cmp_pos_k[l], cmp_pos_v[l], w_cmp_k1[l], w_cmp_k2[l],
                   w_cmp_v1[l], w_cmp_v2[l], w_out[l], rel_bias)
    return x
```

```python
import functools
import math

import jax
import jax.numpy as jnp
import numpy as np
from jax import lax
from jax.experimental import pallas as pl
from jax.experimental.pallas import tpu as pltpu

MXU_DTYPE = jnp.bfloat16
F32 = jnp.float32

HEAD_DIM = 64
LANES = 128
Q_TILE = 128
KV_GROUPS = 2
REP = 4
ROWS = REP * Q_TILE
FAR_TILES = 4
SWA_STEP_TILES = 16
NSA_STEP_TILES = 2
SWA_WINDOW = 128
NSA_WINDOW = 512
WIN_TILES = NSA_WINDOW // Q_TILE
CMP_LEN = 32
CMP_STRIDE = 16
SEL_LEN = 64
SEL_TOPK = 16
N_BUCKETS = 32
MAX_DISTANCE = 128
FORCE_BONUS = 1e4
EPS = 1e-6
NEG = -1e30
LOG2E = 1.4426950408889634
MAX_BOUND = 40.0
PROJ_TM = 1024
OUT_TM = 1024
PROJ_SUBTILES = 4
VMEM_LIMIT = 48 * 1024 * 1024

_IMP_TAPS = tuple(
    (k, (min(CMP_STRIDE * k + CMP_LEN, SEL_LEN) - max(CMP_STRIDE * k, 0)) / CMP_LEN)
    for k in range(-(CMP_LEN // CMP_STRIDE) + 1, SEL_LEN // CMP_STRIDE))
_IMP_PAD = 8

OFF_QA, OFF_KA, OFF_VA, OFF_ZA = 0, 512, 640, 768
OFF_QB, OFF_KC, OFF_VC, OFF_KS, OFF_VS, OFF_KW, OFF_VW, OFF_ZB, OFF_GB = (
    1280, 1792, 1920, 2048, 2176, 2304, 2432, 2560, 3072)
D_PROJ = 3096
D_PROJ_PAD = 3200


def _dot(a, b):
    return jnp.dot(a, b, preferred_element_type=F32)


def _dot_nt(a, b):
    return lax.dot_general(a, b, (((1,), (1,)), ((), ())), preferred_element_type=F32)


def _split3(x):
    hi = x.astype(MXU_DTYPE)
    r1 = x - hi.astype(F32)
    mid = r1.astype(MXU_DTYPE)
    lo = (r1 - mid.astype(F32)).astype(MXU_DTYPE)
    return hi, mid, lo


def _mod_kernel(c_ref, w_ref, b_ref, o_ref):
    sc = jax.nn.silu(c_ref[...])
    w = w_ref[...]
    acc = jnp.zeros(o_ref.shape, F32)
    for a in _split3(sc):
        for b in _split3(w)[:2]:
            acc = acc + _dot(a, b)
    o_ref[...] = acc + b_ref[...]


def _mod_call(c, w_ada, b_ada):
    bsz, d = c.shape
    n = w_ada.shape[1]
    tn = 512
    return pl.pallas_call(
        _mod_kernel,
        grid=(n // tn,),
        in_specs=[pl.BlockSpec((bsz, d), lambda j: (0, 0)),
                  pl.BlockSpec((d, tn), lambda j: (0, j)),
                  pl.BlockSpec((1, tn), lambda j: (0, j))],
        out_specs=pl.BlockSpec((bsz, tn), lambda j: (0, j)),
        out_shape=jax.ShapeDtypeStruct((bsz, n), F32),
        name="adaln_mod",
    )(c, w_ada, b_ada.reshape(1, n))


def _proj_kernel(x_ref, mod_ref, gain_ref, w_ref, kg_ref, bg_ref,
                 qa_ref, ka_ref, va_ref, sza_ref, qb_ref, kc_ref, vc_ref,
                 ks_ref, vs_ref, kw_ref, vw_ref, szb_ref, gb_ref):
    tm = x_ref.shape[1]
    ts = tm // PROJ_SUBTILES
    si = pl.program_id(1)

    def normed(sub):
        x = x_ref[0, sub * ts:(sub + 1) * ts]
        ms = jnp.mean(x * x, axis=-1, keepdims=True)
        y = x * lax.rsqrt(ms + EPS) * gain_ref[...]
        h = y * (1.0 + mod_ref[0, 1:2, :]) + mod_ref[0, 0:1, :]
        return h.astype(MXU_DTYPE)

    hbs = [normed(sub) for sub in range(PROJ_SUBTILES)]
    lane = lax.broadcasted_iota(jnp.int32, (ts, LANES), 1)
    lo = lane < HEAD_DIM

    def half_norm(t, gain_row):
        sq = t * t
        s_lo = jnp.sum(jnp.where(lo, sq, 0.0), axis=-1, keepdims=True)
        s_hi = jnp.sum(jnp.where(lo, 0.0, sq), axis=-1, keepdims=True)
        inv = jnp.where(lo, lax.rsqrt(s_lo * (1.0 / HEAD_DIM) + EPS),
                        lax.rsqrt(s_hi * (1.0 / HEAD_DIM) + EPS))
        return t * inv * gain_row

    def split_heads(t, extra):
        return (jnp.where(lo, t, extra), jnp.where(lo, pltpu.roll(t, HEAD_DIM, axis=1), extra))

    for sub, hb in enumerate(hbs):
        rows = slice(sub * ts, (sub + 1) * ts)
        row = lax.broadcasted_iota(jnp.int32, (ts, LANES), 0) + si * tm + sub * ts
        onehot = jnp.where(lane - HEAD_DIM == row // SEL_LEN, 1.0, 0.0)

        def seg(off, n):
            return _dot(hb, w_ref[:, off:off + n])

        def write_q(ref, off, gain_row):
            t = seg(off, REP * KV_GROUPS * HEAD_DIM)
            for c in range(REP * KV_GROUPS // 2):
                pair = split_heads(half_norm(t[:, LANES * c:LANES * (c + 1)], gain_row), 0.0)
                for j in range(2):
                    col = LANES * (2 * c + j)
                    ref[0, rows, col:col + LANES] = pair[j].astype(ref.dtype)

        def write_kv(k_ref, vt_ref, off, gain_row):
            kv = seg(off, 2 * LANES)
            for g, t in enumerate(split_heads(half_norm(kv[:, :LANES], gain_row), onehot)):
                k_ref[0, g, rows] = t.astype(k_ref.dtype)
            if vt_ref.shape[3] > HEAD_DIM:
                vts = [t.T for t in split_heads(kv[:, LANES:], 1.0)]
            else:
                vt = kv[:, LANES:].T
                vts = [vt[HEAD_DIM * g:HEAD_DIM * (g + 1)] for g in range(KV_GROUPS)]
            for g in range(KV_GROUPS):
                for j in range(ts // Q_TILE):
                    vt_ref[0, g, sub * (ts // Q_TILE) + j] = (
                        vts[g][:, Q_TILE * j:Q_TILE * (j + 1)].astype(vt_ref.dtype))

        write_q(qa_ref, OFF_QA, kg_ref[3:4, :])
        write_kv(ka_ref, va_ref, OFF_KA, kg_ref[0:1, :])
        sza_ref[0, rows] = jax.nn.silu(seg(OFF_ZA, 512)).astype(sza_ref.dtype)
        write_q(qb_ref, OFF_QB, kg_ref[4:5, :])
        kvc = seg(OFF_KC, 2 * LANES)
        kc_ref[0, rows] = kvc[:, :LANES].astype(kc_ref.dtype)
        vc_ref[0, rows] = kvc[:, LANES:].astype(vc_ref.dtype)
        write_kv(ks_ref, vs_ref, OFF_KS, kg_ref[1:2, :])
        write_kv(kw_ref, vw_ref, OFF_KW, kg_ref[2:3, :])
        szb_ref[0, rows] = jax.nn.silu(seg(OFF_ZB, 512)).astype(szb_ref.dtype)
        gates = jax.nn.sigmoid(seg(OFF_GB, LANES) + bg_ref[...])
        gb_ref[0, 0, rows] = gates
        gb_ref[0, 1, rows] = pltpu.roll(gates, LANES - REP * 3, axis=1)


def _proj_call(x, mod3, norm_gain, w_in_p, gains, bgate):
    bsz, s, d = x.shape
    tm = PROJ_TM
    dt = MXU_DTYPE
    nt = tm // Q_TILE
    qwidth = REP * KV_GROUPS * LANES
    rowq = pl.BlockSpec((1, tm, qwidth), lambda b, i: (b, i, 0))
    row512 = pl.BlockSpec((1, tm, 512), lambda b, i: (b, i, 0))
    row128 = pl.BlockSpec((1, tm, LANES), lambda b, i: (b, i, 0))
    grp = pl.BlockSpec((1, KV_GROUPS, tm, LANES), lambda b, i: (b, 0, i, 0))
    grpt = pl.BlockSpec((1, KV_GROUPS, nt, HEAD_DIM, Q_TILE), lambda b, i: (b, 0, i, 0, 0))
    grpt1 = pl.BlockSpec((1, KV_GROUPS, nt, LANES, Q_TILE), lambda b, i: (b, 0, i, 0, 0))
    sgrpt1 = jax.ShapeDtypeStruct((bsz, KV_GROUPS, s // Q_TILE, LANES, Q_TILE), dt)
    sq = jax.ShapeDtypeStruct((bsz, s, qwidth), dt)
    s512 = jax.ShapeDtypeStruct((bsz, s, 512), dt)
    s128 = jax.ShapeDtypeStruct((bsz, s, LANES), dt)
    sgrp = jax.ShapeDtypeStruct((bsz, KV_GROUPS, s, LANES), dt)
    sgrpt = jax.ShapeDtypeStruct((bsz, KV_GROUPS, s // Q_TILE, HEAD_DIM, Q_TILE), dt)
    return pl.pallas_call(
        _proj_kernel,
        grid=(bsz, s // tm),
        in_specs=[pl.BlockSpec((1, tm, d), lambda b, i: (b, i, 0)),
                  pl.BlockSpec((1, 3, d), lambda b, i: (b, 0, 0)),
                  pl.BlockSpec((1, d), lambda b, i: (0, 0)),
                  pl.BlockSpec((d, D_PROJ_PAD), lambda b, i: (0, 0)),
                  pl.BlockSpec((8, LANES), lambda b, i: (0, 0)),
                  pl.BlockSpec((1, LANES), lambda b, i: (0, 0))],
        out_specs=[rowq, grp, grpt1, row512, rowq, row128, row128, grp, grpt, grp, grpt, row512, grp],
        out_shape=[sq, sgrp, sgrpt1, s512, sq, s128, s128, sgrp, sgrpt, sgrp, sgrpt, s512,
                   jax.ShapeDtypeStruct((bsz, KV_GROUPS, s, LANES), F32)],
        compiler_params=pltpu.CompilerParams(
            dimension_semantics=("arbitrary", "arbitrary"), vmem_limit_bytes=VMEM_LIMIT),
        name="norm_in_proj",
    )(x, mod3, norm_gain, w_in_p, gains, bgate)


def _compress_kernel(kc_ref, vc_ref, wkt_ref, wkb_ref, wvt_ref, wvb_ref, pos_ref, w2k_ref, w2v_ref,
                     kg_ref, ko_ref, vo_ref):
    ncp = kc_ref.shape[1]

    def hidden(h_ref, wt_ref, wb_ref, ptop, pbot):
        hf = h_ref[0].astype(F32)
        top = _dot((hf + ptop).astype(MXU_DTYPE), wt_ref[...])
        bot = _dot((hf + pbot).astype(MXU_DTYPE), wb_ref[...])
        pre = top + pltpu.roll(bot, ncp - 1, axis=0)
        return jax.nn.silu(pre).astype(MXU_DTYPE)

    hk = hidden(kc_ref, wkt_ref, wkb_ref, pos_ref[0:1, :], pos_ref[1:2, :])
    hv = hidden(vc_ref, wvt_ref, wvb_ref, pos_ref[2:3, :], pos_ref[3:4, :])
    nh = w2k_ref.shape[0]
    for g in range(KV_GROUPS):
        k = _dot(hk[:, g * nh:(g + 1) * nh], w2k_ref[...])
        ss = jnp.sum(k * k, axis=-1, keepdims=True) * (1.0 / HEAD_DIM)
        ko_ref[0, g] = (k * lax.rsqrt(ss + EPS) * kg_ref[...]).astype(ko_ref.dtype)
        v = _dot(hv[:, g * nh:(g + 1) * nh], w2v_ref[...])
        vo_ref[0, g] = v.T[:HEAD_DIM].astype(vo_ref.dtype)


def _compress_call(kc_r, vc_r, wkt, wkb, wvt, wvb, pos4, w2k, w2v, kgain):
    bsz, ncp, width = kc_r.shape
    full = lambda a: pl.BlockSpec(a.shape, lambda b: (0,) * a.ndim)
    return pl.pallas_call(
        _compress_kernel,
        grid=(bsz,),
        in_specs=[pl.BlockSpec((1, ncp, width), lambda b: (b, 0, 0)),
                  pl.BlockSpec((1, ncp, width), lambda b: (b, 0, 0)),
                  full(wkt), full(wkb), full(wvt), full(wvb), full(pos4), full(w2k), full(w2v),
                  full(kgain)],
        out_specs=[pl.BlockSpec((1, KV_GROUPS, ncp, LANES), lambda b: (b, 0, 0, 0)),
                   pl.BlockSpec((1, KV_GROUPS, HEAD_DIM, ncp), lambda b: (b, 0, 0, 0))],
        out_shape=[jax.ShapeDtypeStruct((bsz, KV_GROUPS, ncp, LANES), MXU_DTYPE),
                   jax.ShapeDtypeStruct((bsz, KV_GROUPS, HEAD_DIM, ncp), MXU_DTYPE)],
        compiler_params=pltpu.CompilerParams(
            dimension_semantics=("arbitrary",), vmem_limit_bytes=VMEM_LIMIT),
        name="nsa_compress",
    )(kc_r, vc_r, wkt, wkb, wvt, wvb, pos4, w2k, w2v, kgain)


def _tile_rows(t):
    return slice(t * Q_TILE, (t + 1) * Q_TILE)


def _stacked_queries(q_ref, t, g):
    return jnp.concatenate(
        [q_ref[0, _tile_rows(t), LANES * (REP * g + r):LANES * (REP * g + r + 1)]
         for r in range(REP)], axis=0)


def _with_mask(qst, mask):
    lo = lax.broadcasted_iota(jnp.int32, (1, LANES), 1) < HEAD_DIM
    return jnp.where(lo, qst, mask.astype(qst.dtype))


def _open(bound):
    return 0.0 if bound is None else -bound


def _range_mask(first_tile, last_tile, bound):
    blk = lax.broadcasted_iota(jnp.int32, (1, LANES), 1) - HEAD_DIM
    return jnp.where(blk < 2 * first_tile, NEG,
                     jnp.where(blk > 2 * last_tile + 1, NEG, _open(bound)))


def _k_window(ref, g, first_tile, ntiles):
    start = pl.multiple_of(jnp.maximum(first_tile, 0) * Q_TILE, Q_TILE)
    return ref[0, g, pl.ds(start, ntiles * Q_TILE), :]


def _vt_window(ref, g, first_tile, ntiles):
    start = jnp.maximum(first_tile, 0)
    return jnp.concatenate([ref[0, g, start + t] for t in range(ntiles)], axis=1)


def _near_table(nb, first):
    diag = nb[Q_TILE:]
    return jnp.concatenate([jnp.where(first, diag, nb[:Q_TILE]), diag], axis=0)


def _colmax(s):
    return jnp.max(s, axis=0, keepdims=True)


def _empty_state(bound, cols):
    zero = (jnp.zeros((1, cols), F32), jnp.zeros((HEAD_DIM, cols), F32))
    return zero if bound is not None else (jnp.full((1, cols), NEG, F32),) + zero


def _accumulate(state, s, vt, bound):
    if vt.shape[0] > HEAD_DIM:
        colsum = lambda e: 0.0
    else:
        colsum = lambda e: jnp.sum(e, axis=0, keepdims=True)
    if bound is not None:
        l, acc = state
        e = jnp.exp2(s)
        return l + colsum(e), acc + _dot(vt, e.astype(MXU_DTYPE))
    m, l, acc = state
    m_new = jnp.maximum(m, _colmax(s))
    alpha = jnp.exp2(m - m_new)
    e = jnp.exp2(s - m_new)
    return m_new, alpha * l + colsum(e), alpha * acc + _dot(vt, e.astype(MXU_DTYPE))


def _values(state):
    l, acc = state[-2], state[-1]
    if acc.shape[0] > HEAD_DIM:
        l = l + acc[HEAD_DIM:HEAD_DIM + 1]
    return acc[:HEAD_DIM] / l


def _store_heads(o_t, sz_ref, o_ref, t, g):
    rows = _tile_rows(t)
    for c in range(REP // 2):
        pair = jnp.concatenate([o_t[:, Q_TILE * (2 * c):Q_TILE * (2 * c + 1)],
                                o_t[:, Q_TILE * (2 * c + 1):Q_TILE * (2 * c + 2)]], axis=0)
        cols = slice(LANES * (2 * g + c), LANES * (2 * g + c + 1))
        o_ref[0, rows, cols] = (pair.T * sz_ref[0, rows, cols].astype(F32)).astype(o_ref.dtype)


def _both_paths(par_ref, body):
    bounded = par_ref[1] > 0.5
    pl.when(bounded)(lambda: body(par_ref[0]))
    pl.when(jnp.logical_not(bounded))(lambda: body(None))


def _attn_a_body(bound, sinks_ref, q_ref, k_ref, vt_ref, sz_ref, nb_ref, o_ref):
    tiles = q_ref.shape[1] // Q_TILE
    units = [(t, g) for t in range(tiles) for g in range(KV_GROUPS)]
    tile = lambda t: pl.program_id(1) * tiles + t
    sinks = [jnp.concatenate([jnp.full((1, Q_TILE), sinks_ref[REP * g + r], F32)
                              for r in range(REP)], axis=1) for g in range(KV_GROUPS)]
    scores = []
    for t, g in units:
        i = tile(t)
        q = _with_mask(_stacked_queries(q_ref, t, g), _range_mask(i - 1, i, bound))
        nb = _near_table(nb_ref[g], i == 0)
        scores.append(_dot_nt(_k_window(k_ref, g, i - 1, 2), q) + nb)
    outs = []
    for (t, g), s in zip(units, scores):
        zeros = jnp.zeros((vt_ref.shape[3], ROWS), F32)
        if bound is not None:
            state = (jnp.exp2(sinks[g] - bound), zeros)
        else:
            state = (sinks[g], jnp.ones((1, ROWS), F32), zeros)
        outs.append(_values(_accumulate(state, s, _vt_window(vt_ref, g, tile(t) - 1, 2), bound)))
    for (t, g), o in zip(units, outs):
        _store_heads(o, sz_ref, o_ref, t, g)


def _attn_a_kernel(par_ref, *refs):
    _both_paths(par_ref, lambda bound: _attn_a_body(bound, *refs))


def _attn_a_call(par, sinks, qa, ka, vta, sza, nba):
    bsz, s, qwidth = qa.shape
    owidth = sza.shape[2]
    full = lambda a: pl.BlockSpec((1,) + a.shape[1:], lambda b, i: (b,) + (0,) * (a.ndim - 1))
    smem = pl.BlockSpec(memory_space=pltpu.SMEM)
    rows = SWA_STEP_TILES * Q_TILE
    return pl.pallas_call(
        _attn_a_kernel,
        grid=(bsz, s // rows),
        in_specs=[smem, smem,
                  pl.BlockSpec((1, rows, qwidth), lambda b, i: (b, i, 0)),
                  full(ka), full(vta),
                  pl.BlockSpec((1, rows, owidth), lambda b, i: (b, i, 0)),
                  pl.BlockSpec(nba.shape, lambda b, i: (0, 0, 0))],
        out_specs=pl.BlockSpec((1, rows, owidth), lambda b, i: (b, i, 0)),
        out_shape=jax.ShapeDtypeStruct((bsz, s, owidth), MXU_DTYPE),
        compiler_params=pltpu.CompilerParams(
            dimension_semantics=("arbitrary", "arbitrary"), vmem_limit_bytes=VMEM_LIMIT),
        name="attn_swa_sink",
    )(par, sinks, qa, ka, vta, sza, nba)


def _nsa_scores(t, g, i, bound, q_ref, kc_ref, kw_ref, nb_ref):
    qst = _stacked_queries(q_ref, t, g)
    q_near = _with_mask(qst, _range_mask(i - 1, i, bound))
    nb = nb_ref[g]
    nb_near = _near_table(nb[Q_TILE:], i == 0)
    sc = _dot_nt(kc_ref[0, g], q_near)
    q_wfar = _with_mask(qst, _range_mask(i - WIN_TILES, i - 2, bound))
    s_wf = _dot_nt(_k_window(kw_ref, g, i - WIN_TILES, WIN_TILES - 1), q_wfar)
    edge = jnp.where(i >= WIN_TILES, nb[:Q_TILE], 0.0)
    s_wf = jnp.concatenate([s_wf[:Q_TILE] + edge, s_wf[Q_TILE:]], axis=0)
    s_wn = _dot_nt(_k_window(kw_ref, g, i - 1, 2), q_near) + nb_near
    return qst, nb_near, sc, s_wf, s_wn


def _nsa_compressed(g, i, bound, sc, vct_ref, psum_ref, score_out):
    ncp = sc.shape[0]
    tok = i * Q_TILE + (lax.broadcasted_iota(jnp.int32, (1, ROWS), 1) & (Q_TILE - 1))
    last_visible = (tok - (CMP_LEN - 1)) // CMP_STRIDE
    z = jnp.where(lax.broadcasted_iota(jnp.int32, (ncp, ROWS), 0) <= last_visible, sc, NEG)
    if bound is None:
        m = _colmax(z)
        m = jnp.where(m > 0.5 * NEG, m, 0.0)
    else:
        m = bound
    e = jnp.exp2(z - m)
    p = e * (1.0 / jnp.maximum(jnp.sum(e, axis=0, keepdims=True), 1e-30))
    o_cmp = _dot(vct_ref[0, g], p.astype(MXU_DTYPE))
    psum_ref[0:_IMP_PAD] = jnp.zeros((_IMP_PAD, Q_TILE), F32)
    psum_ref[_IMP_PAD:_IMP_PAD + ncp] = sum(p[:, r * Q_TILE:(r + 1) * Q_TILE] for r in range(REP))
    if psum_ref.shape[0] > _IMP_PAD + ncp:
        psum_ref[_IMP_PAD + ncp:] = jnp.zeros((psum_ref.shape[0] - _IMP_PAD - ncp, Q_TILE), F32)
    imp = sum(w * psum_ref[pl.ds(_IMP_PAD + k, SEL_LEN, stride=SEL_LEN // CMP_STRIDE), :]
              for k, w in _IMP_TAPS)

    blk = lax.broadcasted_iota(jnp.int32, (SEL_LEN, Q_TILE), 0)
    cur = (i * Q_TILE + lax.broadcasted_iota(jnp.int32, (SEL_LEN, Q_TILE), 1)) // SEL_LEN
    bonus = jnp.where(blk == 0, FORCE_BONUS,
                      jnp.where(blk == cur, FORCE_BONUS,
                                jnp.where(blk == cur - 1, FORCE_BONUS, 0.0)))
    score_out[...] = jnp.where(blk <= cur, imp + bonus, NEG)
    return o_cmp


def _nsa_window(g, i, bound, s_wf, s_wn, vwt_ref):
    state = _accumulate(_empty_state(bound, ROWS), s_wf,
                        _vt_window(vwt_ref, g, i - WIN_TILES, WIN_TILES - 1), bound)
    return _values(_accumulate(state, s_wn, _vt_window(vwt_ref, g, i - 1, 2), bound))


def _nsa_select(i, bound, score_in, topk, nslab):
    sub = lax.broadcasted_iota(jnp.int32, (8, Q_TILE), 0)
    cur_t = (i * Q_TILE + lax.broadcasted_iota(jnp.int32, (8, Q_TILE), 1)) // SEL_LEN
    slabs = [score_in[8 * v:8 * (v + 1), :] for v in range(nslab)]
    cnts = [jnp.zeros((8, Q_TILE), F32) for _ in range(nslab)]
    for jp in range(8 * nslab):
        rowv = score_in[jp:jp + 1, :]
        for v in range(nslab):
            ge = jnp.where(rowv >= slabs[v], 1.0, 0.0)
            gt = jnp.where(rowv > slabs[v], 1.0, 0.0)
            if 8 * v > jp:
                beats = ge
            elif 8 * v + 7 < jp:
                beats = gt
            else:
                beats = jnp.where(sub + 8 * v > jp, ge, gt)
            cnts[v] = cnts[v] + beats
    sel_t = [jnp.where(cnts[v] < topk, jnp.where(sub + 8 * v <= cur_t, _open(bound), NEG), NEG)
             for v in range(nslab)]
    closed = lambda rows: [jnp.full((rows, Q_TILE), NEG, F32)] if rows else []
    sel_t = jnp.concatenate(
        closed(LANES - SEL_LEN) + sel_t + closed(SEL_LEN - 8 * nslab), axis=0)
    return jnp.concatenate([sel_t.T] * REP, axis=0)


def _nsa_select_all(units, bound, score_ref, topk):
    last_tile = units[-1][2]
    step_slabs = 2
    levels = SEL_LEN // 8 // step_slabs
    level = (2 * last_tile + 1) // (8 * step_slabs)

    def branch(k):
        return lambda: tuple(_nsa_select(i, bound, score_ref.at[u], topk, step_slabs * (k + 1))
                             for u, (t, g, i) in enumerate(units))

    return lax.switch(jnp.minimum(level, levels - 1), [branch(k) for k in range(levels)])


def _nsa_selected_queries(i, qst, selm):
    blk_row = lax.broadcasted_iota(jnp.int32, (1, LANES), 1) - HEAD_DIM
    return _with_mask(qst, selm), _with_mask(qst, jnp.where(blk_row >= 2 * (i - 1), NEG, selm))


def _attn_b_body(bound, q_ref, kc_ref, vct_ref, ks_ref, vst_ref, kw_ref, vwt_ref, sz_ref, gb_ref,
                 nb_ref, o_ref, score_ref, psum_ref, *, topk):
    tiles = q_ref.shape[1] // Q_TILE
    units = [(t, g, pl.program_id(1) * tiles + t) for t in range(tiles) for g in range(KV_GROUPS)]
    scores = [_nsa_scores(t, g, i, bound, q_ref, kc_ref, kw_ref, nb_ref) for t, g, i in units]
    o_cmp = [_nsa_compressed(g, i, bound, scores[u][2], vct_ref, psum_ref.at[u], score_ref.at[u])
             for u, (t, g, i) in enumerate(units)]
    o_win = [_nsa_window(g, i, bound, scores[u][3], scores[u][4], vwt_ref)
             for u, (t, g, i) in enumerate(units)]
    selm = _nsa_select_all(units, bound, score_ref, topk)
    qs = [_nsa_selected_queries(i, scores[u][0], selm[u]) for u, (t, g, i) in enumerate(units)]

    def far_chunks(chunks, carry):
        s_c = [[_dot_nt(_k_window(ks_ref, g, c * FAR_TILES, FAR_TILES), qs[u][1])
                for u, (t, g, i) in enumerate(units)] for c in chunks]
        states = [carry[nstate * u:nstate * (u + 1)] for u in range(len(units))]
        for n, c in enumerate(chunks):
            for u, (t, g, i) in enumerate(units):
                states[u] = _accumulate(states[u], s_c[n][u],
                                        _vt_window(vst_ref, g, c * FAR_TILES, FAR_TILES), bound)
        return tuple(v for st in states for v in st)

    empty = _empty_state(bound, ROWS)
    nstate = len(empty)
    nfar = (units[-1][2] + FAR_TILES - 2) // FAR_TILES
    far = lax.fori_loop(0, nfar // 2, lambda p, carry: far_chunks([2 * p, 2 * p + 1], carry),
                        empty * len(units))
    far = lax.cond(nfar % 2 == 1, lambda carry: far_chunks([nfar - 1], carry),
                   lambda carry: carry, far)

    s_near = [_dot_nt(_k_window(ks_ref, g, i - 1, 2), qs[u][0]) + scores[u][1]
              for u, (t, g, i) in enumerate(units)]
    for u, (t, g, i) in enumerate(units):
        o_sel = _values(_accumulate(far[nstate * u:nstate * (u + 1)], s_near[u],
                                    _vt_window(vst_ref, g, i - 1, 2), bound))
        gates = gb_ref[0, g, _tile_rows(t)].T
        mix = []
        for r in range(REP):
            cols = slice(r * Q_TILE, (r + 1) * Q_TILE)
            mix.append(gates[3 * r:3 * r + 1] * o_cmp[u][:, cols]
                       + gates[3 * r + 1:3 * r + 2] * o_sel[:, cols]
                       + gates[3 * r + 2:3 * r + 3] * o_win[u][:, cols])
        _store_heads(jnp.concatenate(mix, axis=1), sz_ref, o_ref, t, g)


def _attn_b_kernel(par_ref, *refs, topk):
    _both_paths(par_ref, lambda bound: _attn_b_body(bound, *refs, topk=topk))


def _attn_b_call(par, qb, kcmp, vcmpt, ks, vst, kw, vwt, szb, gb, nbw):
    bsz, s, qwidth = qb.shape
    owidth = szb.shape[2]
    full = lambda a: pl.BlockSpec((1,) + a.shape[1:], lambda b, i: (b,) + (0,) * (a.ndim - 1))
    rows = NSA_STEP_TILES * Q_TILE
    chains = NSA_STEP_TILES * KV_GROUPS
    imp_rows = HEAD_DIM * (SEL_LEN // CMP_STRIDE)
    return pl.pallas_call(
        functools.partial(_attn_b_kernel, topk=min(SEL_TOPK, s // SEL_LEN)),
        grid=(bsz, s // rows),
        in_specs=[pl.BlockSpec(memory_space=pltpu.SMEM),
                  pl.BlockSpec((1, rows, qwidth), lambda b, i: (b, i, 0)),
                  full(kcmp), full(vcmpt), full(ks), full(vst), full(kw), full(vwt),
                  pl.BlockSpec((1, rows, owidth), lambda b, i: (b, i, 0)),
                  pl.BlockSpec((1, KV_GROUPS, rows, LANES), lambda b, i: (b, 0, i, 0)),
                  pl.BlockSpec(nbw.shape, lambda b, i: (0, 0, 0))],
        out_specs=pl.BlockSpec((1, rows, owidth), lambda b, i: (b, i, 0)),
        out_shape=jax.ShapeDtypeStruct((bsz, s, owidth), MXU_DTYPE),
        scratch_shapes=[pltpu.VMEM((chains, SEL_LEN, Q_TILE), F32),
                        pltpu.VMEM((chains, 2 * _IMP_PAD + max(kcmp.shape[2], imp_rows), Q_TILE), F32)],
        compiler_params=pltpu.CompilerParams(
            dimension_semantics=("arbitrary", "arbitrary"), vmem_limit_bytes=VMEM_LIMIT),
        name="attn_nsa",
    )(par, qb, kcmp, vcmpt, ks, vst, kw, vwt, szb, gb, nbw)


def _out_kernel(x_ref, mod_ref, ya_ref, yb_ref, w_ref, o_ref):
    half = ya_ref.shape[2]
    out = _dot(ya_ref[0], w_ref[0:half, :]) + _dot(yb_ref[0], w_ref[half:2 * half, :])
    o_ref[0] = x_ref[0] + mod_ref[0, 2:3, :] * out


def _out_call(x, mod3, ya, yb, w_out):
    bsz, s, d = x.shape
    tm = OUT_TM
    xs = pl.BlockSpec((1, tm, d), lambda b, i: (b, i, 0))
    ys = pl.BlockSpec((1, tm, 512), lambda b, i: (b, i, 0))
    return pl.pallas_call(
        _out_kernel,
        grid=(bsz, s // tm),
        in_specs=[xs, pl.BlockSpec((1, 3, d), lambda b, i: (b, 0, 0)), ys, ys,
                  pl.BlockSpec(w_out.shape, lambda b, i: (0, 0))],
        out_specs=xs,
        out_shape=jax.ShapeDtypeStruct(x.shape, x.dtype),
        compiler_params=pltpu.CompilerParams(
            dimension_semantics=("arbitrary", "arbitrary"), vmem_limit_bytes=VMEM_LIMIT),
        name="out_proj_residual",
    )(x, mod3, ya, yb, w_out)


def _t5_bucket(dist):
    n = np.maximum(dist, 0)
    max_exact = N_BUCKETS // 2
    nf = np.maximum(n, 1).astype(np.float32)
    ratio = np.log(nf / np.float32(max_exact)) / np.float32(math.log(MAX_DISTANCE / max_exact))
    large = max_exact + (ratio * np.float32(N_BUCKETS - max_exact)).astype(np.int32)
    large = np.minimum(large, N_BUCKETS - 1)
    return np.where(n < max_exact, n, large).astype(np.int32)


def _bias_table_kernel(rel_ref, idx_ref, idx_edge_ref, nba_ref, nbw_ref):
    h = pl.program_id(0)
    hb = h + pl.num_programs(0)

    def lookup(idx, head):
        acc = jnp.zeros(idx.shape, F32)
        for b in range(N_BUCKETS):
            acc = jnp.where(idx == b, rel_ref[b, head], acc)
        return acc

    idx = idx_ref[...]
    dist = (lax.broadcasted_iota(jnp.int32, idx.shape, 1) + Q_TILE
            - lax.broadcasted_iota(jnp.int32, idx.shape, 0))
    causal = dist >= 0
    far = rel_ref[N_BUCKETS - 1, hb]
    nba_ref[0] = jnp.where(causal, jnp.where(dist < SWA_WINDOW, lookup(idx, h) * LOG2E, NEG), NEG)
    near_b = jnp.where(causal, (lookup(idx, hb) - far) * LOG2E, NEG)
    idx_e = idx_edge_ref[...]
    dist_e = (lax.broadcasted_iota(jnp.int32, idx_e.shape, 1) + NSA_WINDOW
              - lax.broadcasted_iota(jnp.int32, idx_e.shape, 0))
    edge_b = jnp.where(dist_e < NSA_WINDOW, (lookup(idx_e, hb) - far) * LOG2E, NEG)
    nbw_ref[0] = jnp.concatenate([edge_b, near_b], axis=0)


def _near_tables(rel_bias):
    nheads = rel_bias.shape[1] // 2
    tq = np.arange(Q_TILE)[None, :]
    idx = _t5_bucket(tq + Q_TILE - np.arange(2 * Q_TILE)[:, None])
    idx_edge = _t5_bucket(tq + NSA_WINDOW - np.arange(Q_TILE)[:, None])
    return pl.pallas_call(
        _bias_table_kernel,
        grid=(nheads,),
        in_specs=[pl.BlockSpec(memory_space=pltpu.SMEM),
                  pl.BlockSpec(idx.shape, lambda h: (0, 0)),
                  pl.BlockSpec(idx_edge.shape, lambda h: (0, 0))],
        out_specs=[pl.BlockSpec((1, 2 * Q_TILE, Q_TILE), lambda h: (h // REP, 0, h % REP)),
                   pl.BlockSpec((1, 3 * Q_TILE, Q_TILE), lambda h: (h // REP, 0, h % REP))],
        out_shape=[jax.ShapeDtypeStruct((nheads // REP, 2 * Q_TILE, ROWS), F32),
                   jax.ShapeDtypeStruct((nheads // REP, 3 * Q_TILE, ROWS), F32)],
        name="t5_bias_tables",
    )(rel_bias, idx, idx_edge)


def _compress_weights(w1, pos):
    hid = w1.shape[1]
    half = CMP_LEN // 2
    w1r = w1.reshape(CMP_LEN, HEAD_DIM, hid)
    eye = jnp.eye(KV_GROUPS, dtype=w1.dtype)
    expand = lambda w: jnp.einsum("ldj,gh->lgdhj", w, eye).reshape(
        half * KV_GROUPS * HEAD_DIM, KV_GROUPS * hid).astype(MXU_DTYPE)
    prow = lambda p: jnp.broadcast_to(p[:, None, :], (half, KV_GROUPS, HEAD_DIM)).reshape(1, -1)
    return expand(w1r[:half]), expand(w1r[half:]), prow(pos[:half]), prow(pos[half:])


def _logit_bound(q_gain, k_gains, bias, floor=None):
    gk = jnp.max(jnp.stack([jnp.max(jnp.abs(k)) for k in k_gains]))
    m = 1.02 * HEAD_DIM * jnp.max(jnp.abs(q_gain)) * gk + jnp.max(jnp.abs(bias))
    if floor is not None:
        m = jnp.maximum(m, jnp.max(floor))
    m = jnp.ceil(m).astype(F32)
    return jnp.stack([m, (m <= MAX_BOUND).astype(F32)])


def _upper_zero(row):
    return jnp.concatenate([row, jnp.zeros_like(row)]).reshape(1, LANES).astype(F32)


def _layer(x, c, w_ada, b_ada, norm_gain, w_in, b_nsa_gate, q_gain_a, k_gain_a, sinks, q_gain_b,
           k_gain_cmp, k_gain_sel, k_gain_win, cmp_pos_k, cmp_pos_v, w_cmp_k1, w_cmp_k2,
           w_cmp_v1, w_cmp_v2, w_out, rel_bias):
    bsz, s, d = x.shape
    assert s % (FAR_TILES * Q_TILE) == 0 and s // SEL_LEN <= HEAD_DIM and s // Q_TILE >= WIN_TILES
    assert FAR_TILES % NSA_STEP_TILES == 0 and s % (SWA_STEP_TILES * Q_TILE) == 0
    assert w_in.shape == (d, D_PROJ) and s % PROJ_TM == 0 and s % OUT_TM == 0
    qscale = HEAD_DIM ** -0.5 * LOG2E

    mod3 = _mod_call(c, w_ada, b_ada).reshape(bsz, 3, d)
    w_in_p = jnp.pad(w_in, ((0, 0), (0, D_PROJ_PAD - D_PROJ))).astype(MXU_DTYPE)
    tile2 = lambda gn: jnp.concatenate([gn, gn]).astype(F32)
    gains = jnp.zeros((8, LANES), F32)
    for n, gn in enumerate((k_gain_a, k_gain_sel, k_gain_win, q_gain_a * qscale, q_gain_b * qscale)):
        gains = gains.at[n].set(tile2(gn))
    bgate = jnp.pad(b_nsa_gate, (0, LANES - b_nsa_gate.shape[0])).reshape(1, LANES).astype(F32)
    (qa, ka, vta, sza, qb, kc, vc, ks, vst, kw, vwt, szb, gb) = _proj_call(
        x, mod3, norm_gain.reshape(1, d).astype(F32), w_in_p, gains, bgate)

    ncp = s // CMP_STRIDE
    wkt, wkb, pkt, pkb = _compress_weights(w_cmp_k1, cmp_pos_k)
    wvt, wvb, pvt, pvb = _compress_weights(w_cmp_v1, cmp_pos_v)
    pos4 = jnp.concatenate([pkt, pkb, pvt, pvb], axis=0).astype(F32)
    pad2 = lambda w: jnp.pad(w, ((0, 0), (0, LANES - HEAD_DIM))).astype(MXU_DTYPE)
    half_block = CMP_STRIDE * LANES
    kcmp, vcmpt = _compress_call(kc.reshape(bsz, ncp, half_block), vc.reshape(bsz, ncp, half_block),
                                 wkt, wkb, wvt, wvb, pos4, pad2(w_cmp_k2), pad2(w_cmp_v2),
                                 _upper_zero(k_gain_cmp))

    nba, nbw = _near_tables(rel_bias.astype(F32))
    half = rel_bias.shape[1] // 2
    sinks2 = sinks.astype(F32) * LOG2E
    par_a = _logit_bound(q_gain_a * qscale, [k_gain_a], rel_bias[:, :half] * LOG2E, floor=sinks2)
    par_b = _logit_bound(q_gain_b * qscale, [k_gain_cmp, k_gain_sel, k_gain_win],
                         (rel_bias[:, half:] - rel_bias[N_BUCKETS - 1, half:]) * LOG2E)
    ya = _attn_a_call(par_a, sinks2, qa, ka, vta, sza, nba)
    yb = _attn_b_call(par_b, qb, kcmp, vcmpt, ks, vst, kw, vwt, szb, gb, nbw)
    return _out_call(x, mod3, ya, yb, w_out.astype(MXU_DTYPE))


def kernel(x, c, w_ada, b_ada, norm_gain, w_in, b_nsa_gate, q_gain_a, k_gain_a, sinks, q_gain_b,
           k_gain_cmp, k_gain_sel, k_gain_win, cmp_pos_k, cmp_pos_v, w_cmp_k1, w_cmp_k2,
           w_cmp_v1, w_cmp_v2, w_out, rel_bias):
    for l in range(w_ada.shape[0]):
        x = _layer(x, c, w_ada[l], b_ada[l], norm_gain[l], w_in[l], b_nsa_gate[l], q_gain_a[l],
                   k_gain_a[l], sinks[l], q_gain_b[l], k_gain_cmp[l], k_gain_sel[l],
                   k_gain_win[l], cmp_pos_k[l], cmp_pos_v[l], w_cmp_k1[l], w_cmp_k2[l],
                   w_cmp_v1[l], w_cmp_v2[l], w_out[l], rel_bias)
    return x
```

```python
import functools
import math

import jax
import jax.numpy as jnp
import numpy as np
from jax import lax
from jax.experimental import pallas as pl
from jax.experimental.pallas import tpu as pltpu

MXU_DTYPE = jnp.bfloat16
F32 = jnp.float32

HEAD_DIM = 64
LANES = 128
Q_TILE = 128
KV_GROUPS = 2
REP = 4
ROWS = REP * Q_TILE
FAR_TILES = 4
SWA_STEP_TILES = 16
NSA_STEP_TILES = 2
SWA_WINDOW = 128
NSA_WINDOW = 512
WIN_TILES = NSA_WINDOW // Q_TILE
CMP_LEN = 32
CMP_STRIDE = 16
SEL_LEN = 64
SEL_TOPK = 16
N_BUCKETS = 32
MAX_DISTANCE = 128
FORCE_BONUS = 1e4
EPS = 1e-6
NEG = -1e30
LOG2E = 1.4426950408889634
MAX_BOUND = 40.0
PROJ_TM = 1024
OUT_TM = 1024
PROJ_SUBTILES = 4
VMEM_LIMIT = 48 * 1024 * 1024

_IMP_TAPS = tuple(
    (k, (min(CMP_STRIDE * k + CMP_LEN, SEL_LEN) - max(CMP_STRIDE * k, 0)) / CMP_LEN)
    for k in range(-(CMP_LEN // CMP_STRIDE) + 1, SEL_LEN // CMP_STRIDE))
_IMP_PAD = 8

OFF_QA, OFF_KA, OFF_ZA, OFF_QB, OFF_KC, OFF_KS, OFF_KW, OFF_ZB, OFF_GB = (
    0, 512, 768, 1280, 1792, 2048, 2304, 2560, 3072)
D_PROJ = 3096
D_PROJ_PAD = 3200


def _dot(a, b):
    return jnp.dot(a, b, preferred_element_type=F32)


def _dot_nt(a, b):
    return lax.dot_general(a, b, (((1,), (1,)), ((), ())), preferred_element_type=F32)


def _split3(x):
    hi = x.astype(MXU_DTYPE)
    r1 = x - hi.astype(F32)
    mid = r1.astype(MXU_DTYPE)
    lo = (r1 - mid.astype(F32)).astype(MXU_DTYPE)
    return hi, mid, lo


def _mod_kernel(c_ref, w_ref, b_ref, o_ref):
    sc = jax.nn.silu(c_ref[...])
    w = w_ref[...]
    acc = jnp.zeros(o_ref.shape, F32)
    for a in _split3(sc):
        for b in _split3(w)[:2]:
            acc = acc + _dot(a, b)
    o_ref[...] = acc + b_ref[...]


def _mod_call(c, w_ada, b_ada):
    bsz, d = c.shape
    n = w_ada.shape[1]
    tn = 512
    return pl.pallas_call(
        _mod_kernel,
        grid=(n // tn,),
        in_specs=[pl.BlockSpec((bsz, d), lambda j: (0, 0)),
                  pl.BlockSpec((d, tn), lambda j: (0, j)),
                  pl.BlockSpec((1, tn), lambda j: (0, j))],
        out_specs=pl.BlockSpec((bsz, tn), lambda j: (0, j)),
        out_shape=jax.ShapeDtypeStruct((bsz, n), F32),
        name="adaln_mod",
    )(c, w_ada, b_ada.reshape(1, n))


def _proj_kernel(x_ref, mod_ref, gain_ref, w_ref, kg_ref, bg_ref,
                 qa_ref, ka_ref, va_ref, sza_ref, qb_ref, kc_ref, vc_ref,
                 ks_ref, vs_ref, kw_ref, vw_ref, szb_ref, gb_ref):
    tm = x_ref.shape[1]
    ts = tm // PROJ_SUBTILES
    si = pl.program_id(1)

    def normed(sub):
        x = x_ref[0, sub * ts:(sub + 1) * ts]
        ms = jnp.mean(x * x, axis=-1, keepdims=True)
        y = x * lax.rsqrt(ms + EPS) * gain_ref[...]
        h = y * (1.0 + mod_ref[0, 1:2, :]) + mod_ref[0, 0:1, :]
        return h.astype(MXU_DTYPE)

    hbs = [normed(sub) for sub in range(PROJ_SUBTILES)]
    lane = lax.broadcasted_iota(jnp.int32, (ts, LANES), 1)
    lo = lane < HEAD_DIM

    def half_norm(t, gain_row):
        sq = t * t
        s_lo = jnp.sum(jnp.where(lo, sq, 0.0), axis=-1, keepdims=True)
        s_hi = jnp.sum(jnp.where(lo, 0.0, sq), axis=-1, keepdims=True)
        inv = jnp.where(lo, lax.rsqrt(s_lo * (1.0 / HEAD_DIM) + EPS),
                        lax.rsqrt(s_hi * (1.0 / HEAD_DIM) + EPS))
        return t * inv * gain_row

    def split_heads(t, extra):
        return (jnp.where(lo, t, extra), jnp.where(lo, pltpu.roll(t, HEAD_DIM, axis=1), extra))

    for sub, hb in enumerate(hbs):
        rows = slice(sub * ts, (sub + 1) * ts)
        row = lax.broadcasted_iota(jnp.int32, (ts, LANES), 0) + si * tm + sub * ts
        onehot = jnp.where(lane - HEAD_DIM == row // SEL_LEN, 1.0, 0.0)

        def seg(off, n):
            return _dot(hb, w_ref[:, off:off + n])

        def write_q(ref, off, gain_row):
            t = seg(off, REP * KV_GROUPS * HEAD_DIM)
            for c in range(REP * KV_GROUPS // 2):
                pair = split_heads(half_norm(t[:, LANES * c:LANES * (c + 1)], gain_row), 0.0)
                for j in range(2):
                    col = LANES * (2 * c + j)
                    ref[0, rows, col:col + LANES] = pair[j].astype(ref.dtype)

        def write_kv(k_ref, vt_ref, off, gain_row):
            kv = seg(off, 2 * LANES)
            for g, t in enumerate(split_heads(half_norm(kv[:, :LANES], gain_row), onehot)):
                k_ref[0, g, rows] = t.astype(k_ref.dtype)
            if vt_ref.shape[3] > HEAD_DIM:
                vts = [t.T for t in split_heads(kv[:, LANES:], 1.0)]
            else:
                vt = kv[:, LANES:].T
                vts = [vt[HEAD_DIM * g:HEAD_DIM * (g + 1)] for g in range(KV_GROUPS)]
            for g in range(KV_GROUPS):
                for j in range(ts // Q_TILE):
                    vt_ref[0, g, sub * (ts // Q_TILE) + j] = (
                        vts[g][:, Q_TILE * j:Q_TILE * (j + 1)].astype(vt_ref.dtype))

        write_q(qa_ref, OFF_QA, kg_ref[3:4, :])
        write_kv(ka_ref, va_ref, OFF_KA, kg_ref[0:1, :])
        sza_ref[0, rows] = jax.nn.silu(seg(OFF_ZA, 512)).astype(sza_ref.dtype)
        write_q(qb_ref, OFF_QB, kg_ref[4:5, :])
        kvc = seg(OFF_KC, 2 * LANES)
        kc_ref[0, rows] = kvc[:, :LANES].astype(kc_ref.dtype)
        vc_ref[0, rows] = kvc[:, LANES:].astype(vc_ref.dtype)
        write_kv(ks_ref, vs_ref, OFF_KS, kg_ref[1:2, :])
        write_kv(kw_ref, vw_ref, OFF_KW, kg_ref[2:3, :])
        szb_ref[0, rows] = jax.nn.silu(seg(OFF_ZB, 512)).astype(szb_ref.dtype)
        gates = jax.nn.sigmoid(seg(OFF_GB, LANES) + bg_ref[...])
        gb_ref[0, 0, rows] = gates
        gb_ref[0, 1, rows] = pltpu.roll(gates, LANES - REP * 3, axis=1)


def _proj_call(x, mod3, norm_gain, w_in_p, gains, bgate):
    bsz, s, d = x.shape
    tm = PROJ_TM
    dt = MXU_DTYPE
    nt = tm // Q_TILE
    qwidth = REP * KV_GROUPS * LANES
    rowq = pl.BlockSpec((1, tm, qwidth), lambda b, i: (b, i, 0))
    row512 = pl.BlockSpec((1, tm, 512), lambda b, i: (b, i, 0))
    row128 = pl.BlockSpec((1, tm, LANES), lambda b, i: (b, i, 0))
    grp = pl.BlockSpec((1, KV_GROUPS, tm, LANES), lambda b, i: (b, 0, i, 0))
    grpt = pl.BlockSpec((1, KV_GROUPS, nt, HEAD_DIM, Q_TILE), lambda b, i: (b, 0, i, 0, 0))
    grpt1 = pl.BlockSpec((1, KV_GROUPS, nt, LANES, Q_TILE), lambda b, i: (b, 0, i, 0, 0))
    sgrpt1 = jax.ShapeDtypeStruct((bsz, KV_GROUPS, s // Q_TILE, LANES, Q_TILE), dt)
    sq = jax.ShapeDtypeStruct((bsz, s, qwidth), dt)
    s512 = jax.ShapeDtypeStruct((bsz, s, 512), dt)
    s128 = jax.ShapeDtypeStruct((bsz, s, LANES), dt)
    sgrp = jax.ShapeDtypeStruct((bsz, KV_GROUPS, s, LANES), dt)
    sgrpt = jax.ShapeDtypeStruct((bsz, KV_GROUPS, s // Q_TILE, HEAD_DIM, Q_TILE), dt)
    return pl.pallas_call(
        _proj_kernel,
        grid=(bsz, s // tm),
        in_specs=[pl.BlockSpec((1, tm, d), lambda b, i: (b, i, 0)),
                  pl.BlockSpec((1, 3, d), lambda b, i: (b, 0, 0)),
                  pl.BlockSpec((1, d), lambda b, i: (0, 0)),
                  pl.BlockSpec((d, D_PROJ_PAD), lambda b, i: (0, 0)),
                  pl.BlockSpec((8, LANES), lambda b, i: (0, 0)),
                  pl.BlockSpec((1, LANES), lambda b, i: (0, 0))],
        out_specs=[rowq, grp, grpt1, row512, rowq, row128, row128, grp, grpt, grp, grpt, row512, grp],
        out_shape=[sq, sgrp, sgrpt1, s512, sq, s128, s128, sgrp, sgrpt, sgrp, sgrpt, s512,
                   jax.ShapeDtypeStruct((bsz, KV_GROUPS, s, LANES), F32)],
        compiler_params=pltpu.CompilerParams(
            dimension_semantics=("arbitrary", "arbitrary"), vmem_limit_bytes=VMEM_LIMIT),
        name="norm_in_proj",
    )(x, mod3, norm_gain, w_in_p, gains, bgate)


def _compress_kernel(kc_ref, vc_ref, wkt_ref, wkb_ref, wvt_ref, wvb_ref, pos_ref, w2k_ref, w2v_ref,
                     kg_ref, ko_ref, vo_ref):
    ncp = kc_ref.shape[1]

    def hidden(h_ref, wt_ref, wb_ref, ptop, pbot):
        hf = h_ref[0].astype(F32)
        top = _dot((hf + ptop).astype(MXU_DTYPE), wt_ref[...])
        bot = _dot((hf + pbot).astype(MXU_DTYPE), wb_ref[...])
        pre = top + pltpu.roll(bot, ncp - 1, axis=0)
        return jax.nn.silu(pre).astype(MXU_DTYPE)

    hk = hidden(kc_ref, wkt_ref, wkb_ref, pos_ref[0:1, :], pos_ref[1:2, :])
    hv = hidden(vc_ref, wvt_ref, wvb_ref, pos_ref[2:3, :], pos_ref[3:4, :])
    nh = w2k_ref.shape[0]
    for g in range(KV_GROUPS):
        k = _dot(hk[:, g * nh:(g + 1) * nh], w2k_ref[...])
        ss = jnp.sum(k * k, axis=-1, keepdims=True) * (1.0 / HEAD_DIM)
        ko_ref[0, g] = (k * lax.rsqrt(ss + EPS) * kg_ref[...]).astype(ko_ref.dtype)
        v = _dot(hv[:, g * nh:(g + 1) * nh], w2v_ref[...])
        vo_ref[0, g] = v.T[:HEAD_DIM].astype(vo_ref.dtype)


def _compress_call(kc_r, vc_r, wkt, wkb, wvt, wvb, pos4, w2k, w2v, kgain):
    bsz, ncp, width = kc_r.shape
    full = lambda a: pl.BlockSpec(a.shape, lambda b: (0,) * a.ndim)
    return pl.pallas_call(
        _compress_kernel,
        grid=(bsz,),
        in_specs=[pl.BlockSpec((1, ncp, width), lambda b: (b, 0, 0)),
                  pl.BlockSpec((1, ncp, width), lambda b: (b, 0, 0)),
                  full(wkt), full(wkb), full(wvt), full(wvb), full(pos4), full(w2k), full(w2v),
                  full(kgain)],
        out_specs=[pl.BlockSpec((1, KV_GROUPS, ncp, LANES), lambda b: (b, 0, 0, 0)),
                   pl.BlockSpec((1, KV_GROUPS, HEAD_DIM, ncp), lambda b: (b, 0, 0, 0))],
        out_shape=[jax.ShapeDtypeStruct((bsz, KV_GROUPS, ncp, LANES), MXU_DTYPE),
                   jax.ShapeDtypeStruct((bsz, KV_GROUPS, HEAD_DIM, ncp), MXU_DTYPE)],
        compiler_params=pltpu.CompilerParams(
            dimension_semantics=("arbitrary",), vmem_limit_bytes=VMEM_LIMIT),
        name="nsa_compress",
    )(kc_r, vc_r, wkt, wkb, wvt, wvb, pos4, w2k, w2v, kgain)


def _tile_rows(t):
    return slice(t * Q_TILE, (t + 1) * Q_TILE)


def _stacked_queries(q_ref, t, g):
    return jnp.concatenate(
        [q_ref[0, _tile_rows(t), LANES * (REP * g + r):LANES * (REP * g + r + 1)]
         for r in range(REP)], axis=0)


def _with_mask(qst, mask):
    lo = lax.broadcasted_iota(jnp.int32, (1, LANES), 1) < HEAD_DIM
    return jnp.where(lo, qst, mask.astype(qst.dtype))


def _open(bound):
    return 0.0 if bound is None else -bound


def _range_mask(first_tile, last_tile, bound):
    blk = lax.broadcasted_iota(jnp.int32, (1, LANES), 1) - HEAD_DIM
    return jnp.where(blk < 2 * first_tile, NEG,
                     jnp.where(blk > 2 * last_tile + 1, NEG, _open(bound)))


def _k_window(ref, g, first_tile, ntiles):
    start = pl.multiple_of(jnp.maximum(first_tile, 0) * Q_TILE, Q_TILE)
    return ref[0, g, pl.ds(start, ntiles * Q_TILE), :]


def _vt_window(ref, g, first_tile, ntiles):
    start = jnp.maximum(first_tile, 0)
    return jnp.concatenate([ref[0, g, start + t] for t in range(ntiles)], axis=1)


def _near_table(nb, first):
    diag = nb[Q_TILE:]
    return jnp.concatenate([jnp.where(first, diag, nb[:Q_TILE]), diag], axis=0)


def _colmax(s):
    return jnp.max(s, axis=0, keepdims=True)


def _empty_state(bound, cols):
    zero = (jnp.zeros((1, cols), F32), jnp.zeros((HEAD_DIM, cols), F32))
    return zero if bound is not None else (jnp.full((1, cols), NEG, F32),) + zero


def _accumulate(state, s, vt, bound):
    if vt.shape[0] > HEAD_DIM:
        colsum = lambda e: 0.0
    else:
        colsum = lambda e: jnp.sum(e, axis=0, keepdims=True)
    if bound is not None:
        l, acc = state
        e = jnp.exp2(s)
        return l + colsum(e), acc + _dot(vt, e.astype(MXU_DTYPE))
    m, l, acc = state
    m_new = jnp.maximum(m, _colmax(s))
    alpha = jnp.exp2(m - m_new)
    e = jnp.exp2(s - m_new)
    return m_new, alpha * l + colsum(e), alpha * acc + _dot(vt, e.astype(MXU_DTYPE))


def _values(state):
    l, acc = state[-2], state[-1]
    if acc.shape[0] > HEAD_DIM:
        l = l + acc[HEAD_DIM:HEAD_DIM + 1]
    return acc[:HEAD_DIM] / l


def _store_heads(o_t, sz_ref, o_ref, t, g):
    rows = _tile_rows(t)
    for c in range(REP // 2):
        pair = jnp.concatenate([o_t[:, Q_TILE * (2 * c):Q_TILE * (2 * c + 1)],
                                o_t[:, Q_TILE * (2 * c + 1):Q_TILE * (2 * c + 2)]], axis=0)
        cols = slice(LANES * (2 * g + c), LANES * (2 * g + c + 1))
        o_ref[0, rows, cols] = (pair.T * sz_ref[0, rows, cols].astype(F32)).astype(o_ref.dtype)


def _both_paths(par_ref, body):
    bounded = par_ref[1] > 0.5
    pl.when(bounded)(lambda: body(par_ref[0]))
    pl.when(jnp.logical_not(bounded))(lambda: body(None))


def _attn_a_body(bound, sinks_ref, q_ref, k_ref, vt_ref, sz_ref, nb_ref, o_ref):
    tiles = q_ref.shape[1] // Q_TILE
    units = [(t, g) for t in range(tiles) for g in range(KV_GROUPS)]
    tile = lambda t: pl.program_id(1) * tiles + t
    sinks = [jnp.concatenate([jnp.full((1, Q_TILE), sinks_ref[REP * g + r], F32)
                              for r in range(REP)], axis=1) for g in range(KV_GROUPS)]
    scores = []
    for t, g in units:
        i = tile(t)
        q = _with_mask(_stacked_queries(q_ref, t, g), _range_mask(i - 1, i, bound))
        nb = _near_table(nb_ref[g], i == 0)
        scores.append(_dot_nt(_k_window(k_ref, g, i - 1, 2), q) + nb)
    outs = []
    for (t, g), s in zip(units, scores):
        zeros = jnp.zeros((vt_ref.shape[3], ROWS), F32)
        if bound is not None:
            state = (jnp.exp2(sinks[g] - bound), zeros)
        else:
            state = (sinks[g], jnp.ones((1, ROWS), F32), zeros)
        outs.append(_values(_accumulate(state, s, _vt_window(vt_ref, g, tile(t) - 1, 2), bound)))
    for (t, g), o in zip(units, outs):
        _store_heads(o, sz_ref, o_ref, t, g)


def _attn_a_kernel(par_ref, *refs):
    _both_paths(par_ref, lambda bound: _attn_a_body(bound, *refs))


def _attn_a_call(par, sinks, qa, ka, vta, sza, nba):
    bsz, s, qwidth = qa.shape
    owidth = sza.shape[2]
    full = lambda a: pl.BlockSpec((1,) + a.shape[1:], lambda b, i: (b,) + (0,) * (a.ndim - 1))
    smem = pl.BlockSpec(memory_space=pltpu.SMEM)
    rows = SWA_STEP_TILES * Q_TILE
    return pl.pallas_call(
        _attn_a_kernel,
        grid=(bsz, s // rows),
        in_specs=[smem, smem,
                  pl.BlockSpec((1, rows, qwidth), lambda b, i: (b, i, 0)),
                  full(ka), full(vta),
                  pl.BlockSpec((1, rows, owidth), lambda b, i: (b, i, 0)),
                  pl.BlockSpec(nba.shape, lambda b, i: (0, 0, 0))],
        out_specs=pl.BlockSpec((1, rows, owidth), lambda b, i: (b, i, 0)),
        out_shape=jax.ShapeDtypeStruct((bsz, s, owidth), MXU_DTYPE),
        compiler_params=pltpu.CompilerParams(
            dimension_semantics=("arbitrary", "arbitrary"), vmem_limit_bytes=VMEM_LIMIT),
        name="attn_swa_sink",
    )(par, sinks, qa, ka, vta, sza, nba)


def _nsa_scores(t, g, i, bound, q_ref, kc_ref, kw_ref, nb_ref):
    qst = _stacked_queries(q_ref, t, g)
    q_near = _with_mask(qst, _range_mask(i - 1, i, bound))
    nb = nb_ref[g]
    nb_near = _near_table(nb[Q_TILE:], i == 0)
    sc = _dot_nt(kc_ref[0, g], q_near)
    q_wfar = _with_mask(qst, _range_mask(i - WIN_TILES, i - 2, bound))
    s_wf = _dot_nt(_k_window(kw_ref, g, i - WIN_TILES, WIN_TILES - 1), q_wfar)
    edge = jnp.where(i >= WIN_TILES, nb[:Q_TILE], 0.0)
    s_wf = jnp.concatenate([s_wf[:Q_TILE] + edge, s_wf[Q_TILE:]], axis=0)
    s_wn = _dot_nt(_k_window(kw_ref, g, i - 1, 2), q_near) + nb_near
    return qst, nb_near, sc, s_wf, s_wn


def _nsa_compressed(g, i, bound, sc, vct_ref, psum_ref, score_out):
    ncp = sc.shape[0]
    tok = i * Q_TILE + (lax.broadcasted_iota(jnp.int32, (1, ROWS), 1) & (Q_TILE - 1))
    last_visible = (tok - (CMP_LEN - 1)) // CMP_STRIDE
    z = jnp.where(lax.broadcasted_iota(jnp.int32, (ncp, ROWS), 0) <= last_visible, sc, NEG)
    if bound is None:
        m = _colmax(z)
        m = jnp.where(m > 0.5 * NEG, m, 0.0)
    else:
        m = bound
    e = jnp.exp2(z - m)
    p = e * (1.0 / jnp.maximum(jnp.sum(e, axis=0, keepdims=True), 1e-30))
    o_cmp = _dot(vct_ref[0, g], p.astype(MXU_DTYPE))
    psum_ref[0:_IMP_PAD] = jnp.zeros((_IMP_PAD, Q_TILE), F32)
    psum_ref[_IMP_PAD:_IMP_PAD + ncp] = sum(p[:, r * Q_TILE:(r + 1) * Q_TILE] for r in range(REP))
    if psum_ref.shape[0] > _IMP_PAD + ncp:
        psum_ref[_IMP_PAD + ncp:] = jnp.zeros((psum_ref.shape[0] - _IMP_PAD - ncp, Q_TILE), F32)
    imp = sum(w * psum_ref[pl.ds(_IMP_PAD + k, SEL_LEN, stride=SEL_LEN // CMP_STRIDE), :]
              for k, w in _IMP_TAPS)

    blk = lax.broadcasted_iota(jnp.int32, (SEL_LEN, Q_TILE), 0)
    cur = (i * Q_TILE + lax.broadcasted_iota(jnp.int32, (SEL_LEN, Q_TILE), 1)) // SEL_LEN
    bonus = jnp.where(blk == 0, FORCE_BONUS,
                      jnp.where(blk == cur, FORCE_BONUS,
                                jnp.where(blk == cur - 1, FORCE_BONUS, 0.0)))
    score_out[...] = jnp.where(blk <= cur, imp + bonus, NEG)
    return o_cmp


def _nsa_window(g, i, bound, s_wf, s_wn, vwt_ref):
    state = _accumulate(_empty_state(bound, ROWS), s_wf,
                        _vt_window(vwt_ref, g, i - WIN_TILES, WIN_TILES - 1), bound)
    return _values(_accumulate(state, s_wn, _vt_window(vwt_ref, g, i - 1, 2), bound))


def _nsa_select(i, bound, score_in, topk, nslab):
    sub = lax.broadcasted_iota(jnp.int32, (8, Q_TILE), 0)
    cur_t = (i * Q_TILE + lax.broadcasted_iota(jnp.int32, (8, Q_TILE), 1)) // SEL_LEN
    slabs = [score_in[8 * v:8 * (v + 1), :] for v in range(nslab)]
    cnts = [jnp.zeros((8, Q_TILE), F32) for _ in range(nslab)]
    for jp in range(8 * nslab):
        rowv = score_in[jp:jp + 1, :]
        for v in range(nslab):
            ge = jnp.where(rowv >= slabs[v], 1.0, 0.0)
            gt = jnp.where(rowv > slabs[v], 1.0, 0.0)
            if 8 * v > jp:
                beats = ge
            elif 8 * v + 7 < jp:
                beats = gt
            else:
                beats = jnp.where(sub + 8 * v > jp, ge, gt)
            cnts[v] = cnts[v] + beats
    sel_t = [jnp.where(cnts[v] < topk, jnp.where(sub + 8 * v <= cur_t, _open(bound), NEG), NEG)
             for v in range(nslab)]
    closed = lambda rows: [jnp.full((rows, Q_TILE), NEG, F32)] if rows else []
    sel_t = jnp.concatenate(
        closed(LANES - SEL_LEN) + sel_t + closed(SEL_LEN - 8 * nslab), axis=0)
    return jnp.concatenate([sel_t.T] * REP, axis=0)


def _nsa_select_all(units, bound, score_ref, topk):
    last_tile = units[-1][2]
    step_slabs = 2
    levels = SEL_LEN // 8 // step_slabs
    level = (2 * last_tile + 1) // (8 * step_slabs)

    def branch(k):
        return lambda: tuple(_nsa_select(i, bound, score_ref.at[u], topk, step_slabs * (k + 1))
                             for u, (t, g, i) in enumerate(units))

    return lax.switch(jnp.minimum(level, levels - 1), [branch(k) for k in range(levels)])


def _nsa_selected_queries(i, qst, selm):
    blk_row = lax.broadcasted_iota(jnp.int32, (1, LANES), 1) - HEAD_DIM
    return _with_mask(qst, selm), _with_mask(qst, jnp.where(blk_row >= 2 * (i - 1), NEG, selm))


def _attn_b_body(bound, q_ref, kc_ref, vct_ref, ks_ref, vst_ref, kw_ref, vwt_ref, sz_ref, gb_ref,
                 nb_ref, o_ref, score_ref, psum_ref, *, topk):
    tiles = q_ref.shape[1] // Q_TILE
    units = [(t, g, pl.program_id(1) * tiles + t) for t in range(tiles) for g in range(KV_GROUPS)]
    scores = [_nsa_scores(t, g, i, bound, q_ref, kc_ref, kw_ref, nb_ref) for t, g, i in units]
    o_win = [_nsa_window(g, i, bound, scores[u][3], scores[u][4], vwt_ref)
             for u, (t, g, i) in enumerate(units)]
    o_cmp = [_nsa_compressed(g, i, bound, scores[u][2], vct_ref, psum_ref.at[u], score_ref.at[u])
             for u, (t, g, i) in enumerate(units)]
    selm = _nsa_select_all(units, bound, score_ref, topk)
    qs = [_nsa_selected_queries(i, scores[u][0], selm[u]) for u, (t, g, i) in enumerate(units)]

    def far_chunks(chunks, carry):
        s_c = [[_dot_nt(_k_window(ks_ref, g, c * FAR_TILES, FAR_TILES), qs[u][1])
                for u, (t, g, i) in enumerate(units)] for c in chunks]
        states = [carry[nstate * u:nstate * (u + 1)] for u in range(len(units))]
        for n, c in enumerate(chunks):
            for u, (t, g, i) in enumerate(units):
                states[u] = _accumulate(states[u], s_c[n][u],
                                        _vt_window(vst_ref, g, c * FAR_TILES, FAR_TILES), bound)
        return tuple(v for st in states for v in st)

    empty = _empty_state(bound, ROWS)
    nstate = len(empty)
    nfar = (units[-1][2] + FAR_TILES - 2) // FAR_TILES
    far = lax.fori_loop(0, nfar // 2, lambda p, carry: far_chunks([2 * p, 2 * p + 1], carry),
                        empty * len(units))
    far = lax.cond(nfar % 2 == 1, lambda carry: far_chunks([nfar - 1], carry),
                   lambda carry: carry, far)

    s_near = [_dot_nt(_k_window(ks_ref, g, i - 1, 2), qs[u][0]) + scores[u][1]
              for u, (t, g, i) in enumerate(units)]
    for u, (t, g, i) in enumerate(units):
        o_sel = _values(_accumulate(far[nstate * u:nstate * (u + 1)], s_near[u],
                                    _vt_window(vst_ref, g, i - 1, 2), bound))
        gates = gb_ref[0, g, _tile_rows(t)].T
        mix = []
        for r in range(REP):
            cols = slice(r * Q_TILE, (r + 1) * Q_TILE)
            mix.append(gates[3 * r:3 * r + 1] * o_cmp[u][:, cols]
                       + gates[3 * r + 1:3 * r + 2] * o_sel[:, cols]
                       + gates[3 * r + 2:3 * r + 3] * o_win[u][:, cols])
        _store_heads(jnp.concatenate(mix, axis=1), sz_ref, o_ref, t, g)


def _attn_b_kernel(par_ref, *refs, topk):
    _both_paths(par_ref, lambda bound: _attn_b_body(bound, *refs, topk=topk))


def _attn_b_call(par, qb, kcmp, vcmpt, ks, vst, kw, vwt, szb, gb, nbw):
    bsz, s, qwidth = qb.shape
    owidth = szb.shape[2]
    full = lambda a: pl.BlockSpec((1,) + a.shape[1:], lambda b, i: (b,) + (0,) * (a.ndim - 1))
    rows = NSA_STEP_TILES * Q_TILE
    chains = NSA_STEP_TILES * KV_GROUPS
    imp_rows = HEAD_DIM * (SEL_LEN // CMP_STRIDE)
    return pl.pallas_call(
        functools.partial(_attn_b_kernel, topk=min(SEL_TOPK, s // SEL_LEN)),
        grid=(bsz, s // rows),
        in_specs=[pl.BlockSpec(memory_space=pltpu.SMEM),
                  pl.BlockSpec((1, rows, qwidth), lambda b, i: (b, i, 0)),
                  full(kcmp), full(vcmpt), full(ks), full(vst), full(kw), full(vwt),
                  pl.BlockSpec((1, rows, owidth), lambda b, i: (b, i, 0)),
                  pl.BlockSpec((1, KV_GROUPS, rows, LANES), lambda b, i: (b, 0, i, 0)),
                  pl.BlockSpec(nbw.shape, lambda b, i: (0, 0, 0))],
        out_specs=pl.BlockSpec((1, rows, owidth), lambda b, i: (b, i, 0)),
        out_shape=jax.ShapeDtypeStruct((bsz, s, owidth), MXU_DTYPE),
        scratch_shapes=[pltpu.VMEM((chains, SEL_LEN, Q_TILE), F32),
                        pltpu.VMEM((chains, 2 * _IMP_PAD + max(kcmp.shape[2], imp_rows), Q_TILE), F32)],
        compiler_params=pltpu.CompilerParams(
            dimension_semantics=("arbitrary", "arbitrary"), vmem_limit_bytes=VMEM_LIMIT),
        name="attn_nsa",
    )(par, qb, kcmp, vcmpt, ks, vst, kw, vwt, szb, gb, nbw)


def _out_kernel(x_ref, mod_ref, ya_ref, yb_ref, w_ref, o_ref):
    half = ya_ref.shape[2]
    out = _dot(ya_ref[0], w_ref[0:half, :]) + _dot(yb_ref[0], w_ref[half:2 * half, :])
    o_ref[0] = x_ref[0] + mod_ref[0, 2:3, :] * out


def _out_call(x, mod3, ya, yb, w_out):
    bsz, s, d = x.shape
    tm = OUT_TM
    xs = pl.BlockSpec((1, tm, d), lambda b, i: (b, i, 0))
    ys = pl.BlockSpec((1, tm, 512), lambda b, i: (b, i, 0))
    return pl.pallas_call(
        _out_kernel,
        grid=(bsz, s // tm),
        in_specs=[xs, pl.BlockSpec((1, 3, d), lambda b, i: (b, 0, 0)), ys, ys,
                  pl.BlockSpec(w_out.shape, lambda b, i: (0, 0))],
        out_specs=xs,
        out_shape=jax.ShapeDtypeStruct(x.shape, x.dtype),
        compiler_params=pltpu.CompilerParams(
            dimension_semantics=("arbitrary", "arbitrary"), vmem_limit_bytes=VMEM_LIMIT),
        name="out_proj_residual",
    )(x, mod3, ya, yb, w_out)


def _t5_bucket(dist):
    n = np.maximum(dist, 0)
    max_exact = N_BUCKETS // 2
    nf = np.maximum(n, 1).astype(np.float32)
    ratio = np.log(nf / np.float32(max_exact)) / np.float32(math.log(MAX_DISTANCE / max_exact))
    large = max_exact + (ratio * np.float32(N_BUCKETS - max_exact)).astype(np.int32)
    large = np.minimum(large, N_BUCKETS - 1)
    return np.where(n < max_exact, n, large).astype(np.int32)


def _bias_table_kernel(rel_ref, idx_ref, idx_edge_ref, nba_ref, nbw_ref):
    h = pl.program_id(0)
    hb = h + pl.num_programs(0)

    def lookup(idx, head):
        acc = jnp.zeros(idx.shape, F32)
        for b in range(N_BUCKETS):
            acc = jnp.where(idx == b, rel_ref[b, head], acc)
        return acc

    idx = idx_ref[...]
    dist = (lax.broadcasted_iota(jnp.int32, idx.shape, 1) + Q_TILE
            - lax.broadcasted_iota(jnp.int32, idx.shape, 0))
    causal = dist >= 0
    far = rel_ref[N_BUCKETS - 1, hb]
    nba_ref[0] = jnp.where(causal, jnp.where(dist < SWA_WINDOW, lookup(idx, h) * LOG2E, NEG), NEG)
    near_b = jnp.where(causal, (lookup(idx, hb) - far) * LOG2E, NEG)
    idx_e = idx_edge_ref[...]
    dist_e = (lax.broadcasted_iota(jnp.int32, idx_e.shape, 1) + NSA_WINDOW
              - lax.broadcasted_iota(jnp.int32, idx_e.shape, 0))
    edge_b = jnp.where(dist_e < NSA_WINDOW, (lookup(idx_e, hb) - far) * LOG2E, NEG)
    nbw_ref[0] = jnp.concatenate([edge_b, near_b], axis=0)


def _near_tables(rel_bias):
    nheads = rel_bias.shape[1] // 2
    tq = np.arange(Q_TILE)[None, :]
    idx = _t5_bucket(tq + Q_TILE - np.arange(2 * Q_TILE)[:, None])
    idx_edge = _t5_bucket(tq + NSA_WINDOW - np.arange(Q_TILE)[:, None])
    return pl.pallas_call(
        _bias_table_kernel,
        grid=(nheads,),
        in_specs=[pl.BlockSpec(memory_space=pltpu.SMEM),
                  pl.BlockSpec(idx.shape, lambda h: (0, 0)),
                  pl.BlockSpec(idx_edge.shape, lambda h: (0, 0))],
        out_specs=[pl.BlockSpec((1, 2 * Q_TILE, Q_TILE), lambda h: (h // REP, 0, h % REP)),
                   pl.BlockSpec((1, 3 * Q_TILE, Q_TILE), lambda h: (h // REP, 0, h % REP))],
        out_shape=[jax.ShapeDtypeStruct((nheads // REP, 2 * Q_TILE, ROWS), F32),
                   jax.ShapeDtypeStruct((nheads // REP, 3 * Q_TILE, ROWS), F32)],
        name="t5_bias_tables",
    )(rel_bias, idx, idx_edge)


def _compress_weights(w1, pos):
    hid = w1.shape[1]
    half = CMP_LEN // 2
    w1r = w1.reshape(CMP_LEN, HEAD_DIM, hid)
    eye = jnp.eye(KV_GROUPS, dtype=w1.dtype)
    expand = lambda w: jnp.einsum("ldj,gh->lgdhj", w, eye).reshape(
        half * KV_GROUPS * HEAD_DIM, KV_GROUPS * hid).astype(MXU_DTYPE)
    prow = lambda p: jnp.broadcast_to(p[:, None, :], (half, KV_GROUPS, HEAD_DIM)).reshape(1, -1)
    return expand(w1r[:half]), expand(w1r[half:]), prow(pos[:half]), prow(pos[half:])


def _logit_bound(q_gain, k_gains, bias, floor=None):
    gk = jnp.max(jnp.stack([jnp.max(jnp.abs(k)) for k in k_gains]))
    m = 1.02 * HEAD_DIM * jnp.max(jnp.abs(q_gain)) * gk + jnp.max(jnp.abs(bias))
    if floor is not None:
        m = jnp.maximum(m, jnp.max(floor))
    m = jnp.ceil(m).astype(F32)
    return jnp.stack([m, (m <= MAX_BOUND).astype(F32)])


def _upper_zero(row):
    return jnp.concatenate([row, jnp.zeros_like(row)]).reshape(1, LANES).astype(F32)


def _layer(x, c, w_ada, b_ada, norm_gain, w_in, b_nsa_gate, q_gain_a, k_gain_a, sinks, q_gain_b,
           k_gain_cmp, k_gain_sel, k_gain_win, cmp_pos_k, cmp_pos_v, w_cmp_k1, w_cmp_k2,
           w_cmp_v1, w_cmp_v2, w_out, rel_bias):
    bsz, s, d = x.shape
    assert s % (FAR_TILES * Q_TILE) == 0 and s // SEL_LEN <= HEAD_DIM and s // Q_TILE >= WIN_TILES
    assert FAR_TILES % NSA_STEP_TILES == 0 and s % (SWA_STEP_TILES * Q_TILE) == 0
    assert w_in.shape == (d, D_PROJ) and s % PROJ_TM == 0 and s % OUT_TM == 0
    qscale = HEAD_DIM ** -0.5 * LOG2E

    mod3 = _mod_call(c, w_ada, b_ada).reshape(bsz, 3, d)
    w_in_p = jnp.pad(w_in, ((0, 0), (0, D_PROJ_PAD - D_PROJ))).astype(MXU_DTYPE)
    tile2 = lambda gn: jnp.concatenate([gn, gn]).astype(F32)
    gains = jnp.zeros((8, LANES), F32)
    for n, gn in enumerate((k_gain_a, k_gain_sel, k_gain_win, q_gain_a * qscale, q_gain_b * qscale)):
        gains = gains.at[n].set(tile2(gn))
    bgate = jnp.pad(b_nsa_gate, (0, LANES - b_nsa_gate.shape[0])).reshape(1, LANES).astype(F32)
    (qa, ka, vta, sza, qb, kc, vc, ks, vst, kw, vwt, szb, gb) = _proj_call(
        x, mod3, norm_gain.reshape(1, d).astype(F32), w_in_p, gains, bgate)

    ncp = s // CMP_STRIDE
    wkt, wkb, pkt, pkb = _compress_weights(w_cmp_k1, cmp_pos_k)
    wvt, wvb, pvt, pvb = _compress_weights(w_cmp_v1, cmp_pos_v)
    pos4 = jnp.concatenate([pkt, pkb, pvt, pvb], axis=0).astype(F32)
    pad2 = lambda w: jnp.pad(w, ((0, 0), (0, LANES - HEAD_DIM))).astype(MXU_DTYPE)
    half_block = CMP_STRIDE * LANES
    kcmp, vcmpt = _compress_call(kc.reshape(bsz, ncp, half_block), vc.reshape(bsz, ncp, half_block),
                                 wkt, wkb, wvt, wvb, pos4, pad2(w_cmp_k2), pad2(w_cmp_v2),
                                 _upper_zero(k_gain_cmp))

    nba, nbw = _near_tables(rel_bias.astype(F32))
    half = rel_bias.shape[1] // 2
    sinks2 = sinks.astype(F32) * LOG2E
    par_a = _logit_bound(q_gain_a * qscale, [k_gain_a], rel_bias[:, :half] * LOG2E, floor=sinks2)
    par_b = _logit_bound(q_gain_b * qscale, [k_gain_cmp, k_gain_sel, k_gain_win],
                         (rel_bias[:, half:] - rel_bias[N_BUCKETS - 1, half:]) * LOG2E)
    ya = _attn_a_call(par_a, sinks2, qa, ka, vta, sza, nba)
    yb = _attn_b_call(par_b, qb, kcmp, vcmpt, ks, vst, kw, vwt, szb, gb, nbw)
    return _out_call(x, mod3, ya, yb, w_out.astype(MXU_DTYPE))


def kernel(x, c, w_ada, b_ada, norm_gain, w_in, b_nsa_gate, q_gain_a, k_gain_a, sinks, q_gain_b,
           k_gain_cmp, k_gain_sel, k_gain_win, cmp_pos_k, cmp_pos_v, w_cmp_k1, w_cmp_k2,
           w_cmp_v1, w_cmp_v2, w_out, rel_bias):
    for l in range(w_ada.shape[0]):
        x = _layer(x, c, w_ada[l], b_ada[l], norm_gain[l], w_in[l], b_nsa_gate[l], q_gain_a[l],
                   k_gain_a[l], sinks[l], q_gain_b[l], k_gain_cmp[l], k_gain_sel[l],
                   k_gain_win[l], cmp_pos_k[l], cmp_pos_v[l], w_cmp_k1[l], w_cmp_k2[l],
                   w_cmp_v1[l], w_cmp_v2[l], w_out[l], rel_bias)
    return x
```

```python
import functools
import math

import jax
import jax.numpy as jnp
import numpy as np
from jax import lax
from jax.experimental import pallas as pl
from jax.experimental.pallas import tpu as pltpu

MXU_DTYPE = jnp.bfloat16
F32 = jnp.float32

HEAD_DIM = 64
LANES = 128
Q_TILE = 128
KV_GROUPS = 2
REP = 4
ROWS = REP * Q_TILE
FAR_TILES = 4
SWA_STEP_TILES = 16
NSA_STEP_TILES = 2
SWA_WINDOW = 128
NSA_WINDOW = 512
WIN_TILES = NSA_WINDOW // Q_TILE
CMP_LEN = 32
CMP_STRIDE = 16
SEL_LEN = 64
SEL_TOPK = 16
N_BUCKETS = 32
MAX_DISTANCE = 128
FORCE_BONUS = 1e4
EPS = 1e-6
NEG = -1e30
LOG2E = 1.4426950408889634
MAX_BOUND = 40.0
PROJ_TM = 1024
OUT_TM = 1024
PROJ_SUBTILES = 4
VMEM_LIMIT = 48 * 1024 * 1024

_IMP_TAPS = tuple(
    (k, (min(CMP_STRIDE * k + CMP_LEN, SEL_LEN) - max(CMP_STRIDE * k, 0)) / CMP_LEN)
    for k in range(-(CMP_LEN // CMP_STRIDE) + 1, SEL_LEN // CMP_STRIDE))
_IMP_PAD = 8

OFF_QA, OFF_KA, OFF_ZA, OFF_QB, OFF_KC, OFF_KS, OFF_KW, OFF_ZB, OFF_GB = (
    0, 512, 768, 1280, 1792, 2048, 2304, 2560, 3072)
D_PROJ = 3096
D_PROJ_PAD = 3200


def _dot(a, b):
    return jnp.dot(a, b, preferred_element_type=F32)


def _dot_nt(a, b):
    return lax.dot_general(a, b, (((1,), (1,)), ((), ())), preferred_element_type=F32)


def _split3(x):
    hi = x.astype(MXU_DTYPE)
    r1 = x - hi.astype(F32)
    mid = r1.astype(MXU_DTYPE)
    lo = (r1 - mid.astype(F32)).astype(MXU_DTYPE)
    return hi, mid, lo


def _mod_kernel(c_ref, w_ref, b_ref, o_ref):
    sc = jax.nn.silu(c_ref[...])
    w = w_ref[...]
    acc = jnp.zeros(o_ref.shape, F32)
    for a in _split3(sc):
        for b in _split3(w)[:2]:
            acc = acc + _dot(a, b)
    o_ref[...] = acc + b_ref[...]


def _mod_call(c, w_ada, b_ada):
    bsz, d = c.shape
    n = w_ada.shape[1]
    tn = 512
    return pl.pallas_call(
        _mod_kernel,
        grid=(n // tn,),
        in_specs=[pl.BlockSpec((bsz, d), lambda j: (0, 0)),
                  pl.BlockSpec((d, tn), lambda j: (0, j)),
                  pl.BlockSpec((1, tn), lambda j: (0, j))],
        out_specs=pl.BlockSpec((bsz, tn), lambda j: (0, j)),
        out_shape=jax.ShapeDtypeStruct((bsz, n), F32),
        name="adaln_mod",
    )(c, w_ada, b_ada.reshape(1, n))


def _proj_kernel(x_ref, mod_ref, gain_ref, w_ref, kg_ref, bg_ref,
                 qa_ref, ka_ref, va_ref, sza_ref, qb_ref, kc_ref, vc_ref,
                 ks_ref, vs_ref, kw_ref, vw_ref, szb_ref, gb_ref):
    tm = x_ref.shape[1]
    ts = tm // PROJ_SUBTILES
    si = pl.program_id(1)

    def normed(sub):
        x = x_ref[0, sub * ts:(sub + 1) * ts]
        ms = jnp.mean(x * x, axis=-1, keepdims=True)
        y = x * lax.rsqrt(ms + EPS) * gain_ref[...]
        h = y * (1.0 + mod_ref[0, 1:2, :]) + mod_ref[0, 0:1, :]
        return h.astype(MXU_DTYPE)

    hbs = [normed(sub) for sub in range(PROJ_SUBTILES)]
    lane = lax.broadcasted_iota(jnp.int32, (ts, LANES), 1)
    lo = lane < HEAD_DIM

    def half_norm(t, gain_row):
        sq = t * t
        s_lo = jnp.sum(jnp.where(lo, sq, 0.0), axis=-1, keepdims=True)
        s_hi = jnp.sum(jnp.where(lo, 0.0, sq), axis=-1, keepdims=True)
        inv = jnp.where(lo, lax.rsqrt(s_lo * (1.0 / HEAD_DIM) + EPS),
                        lax.rsqrt(s_hi * (1.0 / HEAD_DIM) + EPS))
        return t * inv * gain_row

    def split_heads(t, extra):
        return (jnp.where(lo, t, extra), jnp.where(lo, pltpu.roll(t, HEAD_DIM, axis=1), extra))

    for sub, hb in enumerate(hbs):
        rows = slice(sub * ts, (sub + 1) * ts)
        row = lax.broadcasted_iota(jnp.int32, (ts, LANES), 0) + si * tm + sub * ts
        onehot = jnp.where(lane - HEAD_DIM == row // SEL_LEN, 1.0, 0.0)

        def seg(off, n):
            return _dot(hb, w_ref[:, off:off + n])

        def write_q(ref, off, gain_row):
            t = seg(off, REP * KV_GROUPS * HEAD_DIM)
            for c in range(REP * KV_GROUPS // 2):
                pair = split_heads(half_norm(t[:, LANES * c:LANES * (c + 1)], gain_row), 0.0)
                for j in range(2):
                    col = LANES * (2 * c + j)
                    ref[0, rows, col:col + LANES] = pair[j].astype(ref.dtype)

        def write_kv(k_ref, vt_ref, off, gain_row):
            kv = seg(off, 2 * LANES)
            for g, t in enumerate(split_heads(half_norm(kv[:, :LANES], gain_row), onehot)):
                k_ref[0, g, rows] = t.astype(k_ref.dtype)
            if vt_ref.shape[3] > HEAD_DIM:
                vts = [t.T for t in split_heads(kv[:, LANES:], 1.0)]
            else:
                vt = kv[:, LANES:].T
                vts = [vt[HEAD_DIM * g:HEAD_DIM * (g + 1)] for g in range(KV_GROUPS)]
            for g in range(KV_GROUPS):
                for j in range(ts // Q_TILE):
                    vt_ref[0, g, sub * (ts // Q_TILE) + j] = (
                        vts[g][:, Q_TILE * j:Q_TILE * (j + 1)].astype(vt_ref.dtype))

        write_q(qa_ref, OFF_QA, kg_ref[3:4, :])
        write_kv(ka_ref, va_ref, OFF_KA, kg_ref[0:1, :])
        sza_ref[0, rows] = jax.nn.silu(seg(OFF_ZA, 512)).astype(sza_ref.dtype)
        write_q(qb_ref, OFF_QB, kg_ref[4:5, :])
        kvc = seg(OFF_KC, 2 * LANES)
        kc_ref[0, rows] = kvc[:, :LANES].astype(kc_ref.dtype)
        vc_ref[0, rows] = kvc[:, LANES:].astype(vc_ref.dtype)
        write_kv(ks_ref, vs_ref, OFF_KS, kg_ref[1:2, :])
        write_kv(kw_ref, vw_ref, OFF_KW, kg_ref[2:3, :])
        szb_ref[0, rows] = jax.nn.silu(seg(OFF_ZB, 512)).astype(szb_ref.dtype)
        gates = jax.nn.sigmoid(seg(OFF_GB, LANES) + bg_ref[...])
        gb_ref[0, 0, rows] = gates
        gb_ref[0, 1, rows] = pltpu.roll(gates, LANES - REP * 3, axis=1)


def _proj_call(x, mod3, norm_gain, w_in_p, gains, bgate):
    bsz, s, d = x.shape
    tm = PROJ_TM
    dt = MXU_DTYPE
    nt = tm // Q_TILE
    qwidth = REP * KV_GROUPS * LANES
    rowq = pl.BlockSpec((1, tm, qwidth), lambda b, i: (b, i, 0))
    row512 = pl.BlockSpec((1, tm, 512), lambda b, i: (b, i, 0))
    row128 = pl.BlockSpec((1, tm, LANES), lambda b, i: (b, i, 0))
    grp = pl.BlockSpec((1, KV_GROUPS, tm, LANES), lambda b, i: (b, 0, i, 0))
    grpt = pl.BlockSpec((1, KV_GROUPS, nt, HEAD_DIM, Q_TILE), lambda b, i: (b, 0, i, 0, 0))
    grpt1 = pl.BlockSpec((1, KV_GROUPS, nt, LANES, Q_TILE), lambda b, i: (b, 0, i, 0, 0))
    sgrpt1 = jax.ShapeDtypeStruct((bsz, KV_GROUPS, s // Q_TILE, LANES, Q_TILE), dt)
    sq = jax.ShapeDtypeStruct((bsz, s, qwidth), dt)
    s512 = jax.ShapeDtypeStruct((bsz, s, 512), dt)
    s128 = jax.ShapeDtypeStruct((bsz, s, LANES), dt)
    sgrp = jax.ShapeDtypeStruct((bsz, KV_GROUPS, s, LANES), dt)
    sgrpt = jax.ShapeDtypeStruct((bsz, KV_GROUPS, s // Q_TILE, HEAD_DIM, Q_TILE), dt)
    return pl.pallas_call(
        _proj_kernel,
        grid=(bsz, s // tm),
        in_specs=[pl.BlockSpec((1, tm, d), lambda b, i: (b, i, 0)),
                  pl.BlockSpec((1, 3, d), lambda b, i: (b, 0, 0)),
                  pl.BlockSpec((1, d), lambda b, i: (0, 0)),
                  pl.BlockSpec((d, D_PROJ_PAD), lambda b, i: (0, 0)),
                  pl.BlockSpec((8, LANES), lambda b, i: (0, 0)),
                  pl.BlockSpec((1, LANES), lambda b, i: (0, 0))],
        out_specs=[rowq, grp, grpt1, row512, rowq, row128, row128, grp, grpt, grp, grpt, row512, grp],
        out_shape=[sq, sgrp, sgrpt1, s512, sq, s128, s128, sgrp, sgrpt, sgrp, sgrpt, s512,
                   jax.ShapeDtypeStruct((bsz, KV_GROUPS, s, LANES), F32)],
        compiler_params=pltpu.CompilerParams(
            dimension_semantics=("arbitrary", "arbitrary"), vmem_limit_bytes=VMEM_LIMIT),
        name="norm_in_proj",
    )(x, mod3, norm_gain, w_in_p, gains, bgate)


def _compress_kernel(kc_ref, vc_ref, wkt_ref, wkb_ref, wvt_ref, wvb_ref, pos_ref, w2k_ref, w2v_ref,
                     kg_ref, ko_ref, vo_ref):
    ncp = kc_ref.shape[1]

    def hidden(h_ref, wt_ref, wb_ref, ptop, pbot):
        hf = h_ref[0].astype(F32)
        top = _dot((hf + ptop).astype(MXU_DTYPE), wt_ref[...])
        bot = _dot((hf + pbot).astype(MXU_DTYPE), wb_ref[...])
        pre = top + pltpu.roll(bot, ncp - 1, axis=0)
        return jax.nn.silu(pre).astype(MXU_DTYPE)

    hk = hidden(kc_ref, wkt_ref, wkb_ref, pos_ref[0:1, :], pos_ref[1:2, :])
    hv = hidden(vc_ref, wvt_ref, wvb_ref, pos_ref[2:3, :], pos_ref[3:4, :])
    nh = w2k_ref.shape[0]
    for g in range(KV_GROUPS):
        k = _dot(hk[:, g * nh:(g + 1) * nh], w2k_ref[...])
        ss = jnp.sum(k * k, axis=-1, keepdims=True) * (1.0 / HEAD_DIM)
        ko_ref[0, g] = (k * lax.rsqrt(ss + EPS) * kg_ref[...]).astype(ko_ref.dtype)
        v = _dot(hv[:, g * nh:(g + 1) * nh], w2v_ref[...])
        vo_ref[0, g] = v.T[:HEAD_DIM].astype(vo_ref.dtype)


def _compress_call(kc_r, vc_r, wkt, wkb, wvt, wvb, pos4, w2k, w2v, kgain):
    bsz, ncp, width = kc_r.shape
    full = lambda a: pl.BlockSpec(a.shape, lambda b: (0,) * a.ndim)
    return pl.pallas_call(
        _compress_kernel,
        grid=(bsz,),
        in_specs=[pl.BlockSpec((1, ncp, width), lambda b: (b, 0, 0)),
                  pl.BlockSpec((1, ncp, width), lambda b: (b, 0, 0)),
                  full(wkt), full(wkb), full(wvt), full(wvb), full(pos4), full(w2k), full(w2v),
                  full(kgain)],
        out_specs=[pl.BlockSpec((1, KV_GROUPS, ncp, LANES), lambda b: (b, 0, 0, 0)),
                   pl.BlockSpec((1, KV_GROUPS, HEAD_DIM, ncp), lambda b: (b, 0, 0, 0))],
        out_shape=[jax.ShapeDtypeStruct((bsz, KV_GROUPS, ncp, LANES), MXU_DTYPE),
                   jax.ShapeDtypeStruct((bsz, KV_GROUPS, HEAD_DIM, ncp), MXU_DTYPE)],
        compiler_params=pltpu.CompilerParams(
            dimension_semantics=("arbitrary",), vmem_limit_bytes=VMEM_LIMIT),
        name="nsa_compress",
    )(kc_r, vc_r, wkt, wkb, wvt, wvb, pos4, w2k, w2v, kgain)


def _tile_rows(t):
    return slice(t * Q_TILE, (t + 1) * Q_TILE)


def _stacked_queries(q_ref, t, g):
    return jnp.concatenate(
        [q_ref[0, _tile_rows(t), LANES * (REP * g + r):LANES * (REP * g + r + 1)]
         for r in range(REP)], axis=0)


def _with_mask(qst, mask):
    lo = lax.broadcasted_iota(jnp.int32, (1, LANES), 1) < HEAD_DIM
    return jnp.where(lo, qst, mask.astype(qst.dtype))


def _open(bound):
    return 0.0 if bound is None else -bound


def _range_mask(first_tile, last_tile, bound):
    blk = lax.broadcasted_iota(jnp.int32, (1, LANES), 1) - HEAD_DIM
    return jnp.where(blk < 2 * first_tile, NEG,
                     jnp.where(blk > 2 * last_tile + 1, NEG, _open(bound)))


def _k_window(ref, g, first_tile, ntiles):
    start = pl.multiple_of(jnp.maximum(first_tile, 0) * Q_TILE, Q_TILE)
    return ref[0, g, pl.ds(start, ntiles * Q_TILE), :]


def _vt_window(ref, g, first_tile, ntiles):
    start = jnp.maximum(first_tile, 0)
    return jnp.concatenate([ref[0, g, start + t] for t in range(ntiles)], axis=1)


def _near_table(nb, first):
    diag = nb[Q_TILE:]
    return jnp.concatenate([jnp.where(first, diag, nb[:Q_TILE]), diag], axis=0)


def _colmax(s):
    return jnp.max(s, axis=0, keepdims=True)


def _empty_state(bound, cols):
    zero = (jnp.zeros((1, cols), F32), jnp.zeros((HEAD_DIM, cols), F32))
    return zero if bound is not None else (jnp.full((1, cols), NEG, F32),) + zero


def _accumulate(state, s, vt, bound):
    if vt.shape[0] > HEAD_DIM:
        colsum = lambda e: 0.0
    else:
        colsum = lambda e: jnp.sum(e, axis=0, keepdims=True)
    if bound is not None:
        l, acc = state
        e = jnp.exp2(s)
        return l + colsum(e), acc + _dot(vt, e.astype(MXU_DTYPE))
    m, l, acc = state
    m_new = jnp.maximum(m, _colmax(s))
    alpha = jnp.exp2(m - m_new)
    e = jnp.exp2(s - m_new)
    return m_new, alpha * l + colsum(e), alpha * acc + _dot(vt, e.astype(MXU_DTYPE))


def _values(state):
    l, acc = state[-2], state[-1]
    if acc.shape[0] > HEAD_DIM:
        l = l + acc[HEAD_DIM:HEAD_DIM + 1]
    return acc[:HEAD_DIM] / l


def _store_heads(o_t, sz_ref, o_ref, t, g):
    rows = _tile_rows(t)
    for c in range(REP // 2):
        pair = jnp.concatenate([o_t[:, Q_TILE * (2 * c):Q_TILE * (2 * c + 1)],
                                o_t[:, Q_TILE * (2 * c + 1):Q_TILE * (2 * c + 2)]], axis=0)
        cols = slice(LANES * (2 * g + c), LANES * (2 * g + c + 1))
        o_ref[0, rows, cols] = (pair.T * sz_ref[0, rows, cols].astype(F32)).astype(o_ref.dtype)


def _both_paths(par_ref, body):
    bounded = par_ref[1] > 0.5
    pl.when(bounded)(lambda: body(par_ref[0]))
    pl.when(jnp.logical_not(bounded))(lambda: body(None))


def _attn_a_body(bound, sinks_ref, q_ref, k_ref, vt_ref, sz_ref, nb_ref, o_ref):
    tiles = q_ref.shape[1] // Q_TILE
    units = [(t, g) for t in range(tiles) for g in range(KV_GROUPS)]
    tile = lambda t: pl.program_id(1) * tiles + t
    sinks = [jnp.concatenate([jnp.full((1, Q_TILE), sinks_ref[REP * g + r], F32)
                              for r in range(REP)], axis=1) for g in range(KV_GROUPS)]
    scores = []
    for t, g in units:
        i = tile(t)
        q = _with_mask(_stacked_queries(q_ref, t, g), _range_mask(i - 1, i, bound))
        nb = _near_table(nb_ref[g], i == 0)
        scores.append(_dot_nt(_k_window(k_ref, g, i - 1, 2), q) + nb)
    outs = []
    for (t, g), s in zip(units, scores):
        zeros = jnp.zeros((vt_ref.shape[3], ROWS), F32)
        if bound is not None:
            state = (jnp.exp2(sinks[g] - bound), zeros)
        else:
            state = (sinks[g], jnp.ones((1, ROWS), F32), zeros)
        outs.append(_values(_accumulate(state, s, _vt_window(vt_ref, g, tile(t) - 1, 2), bound)))
    for (t, g), o in zip(units, outs):
        _store_heads(o, sz_ref, o_ref, t, g)


def _attn_a_kernel(par_ref, *refs):
    _both_paths(par_ref, lambda bound: _attn_a_body(bound, *refs))


def _attn_a_call(par, sinks, qa, ka, vta, sza, nba):
    bsz, s, qwidth = qa.shape
    owidth = sza.shape[2]
    full = lambda a: pl.BlockSpec((1,) + a.shape[1:], lambda b, i: (b,) + (0,) * (a.ndim - 1))
    smem = pl.BlockSpec(memory_space=pltpu.SMEM)
    rows = SWA_STEP_TILES * Q_TILE
    return pl.pallas_call(
        _attn_a_kernel,
        grid=(bsz, s // rows),
        in_specs=[smem, smem,
                  pl.BlockSpec((1, rows, qwidth), lambda b, i: (b, i, 0)),
                  full(ka), full(vta),
                  pl.BlockSpec((1, rows, owidth), lambda b, i: (b, i, 0)),
                  pl.BlockSpec(nba.shape, lambda b, i: (0, 0, 0))],
        out_specs=pl.BlockSpec((1, rows, owidth), lambda b, i: (b, i, 0)),
        out_shape=jax.ShapeDtypeStruct((bsz, s, owidth), MXU_DTYPE),
        compiler_params=pltpu.CompilerParams(
            dimension_semantics=("arbitrary", "arbitrary"), vmem_limit_bytes=VMEM_LIMIT),
        name="attn_swa_sink",
    )(par, sinks, qa, ka, vta, sza, nba)


def _nsa_scores(t, g, i, bound, q_ref, kc_ref, kw_ref, nb_ref):
    qst = _stacked_queries(q_ref, t, g)
    q_near = _with_mask(qst, _range_mask(i - 1, i, bound))
    nb = nb_ref[g]
    nb_near = _near_table(nb[Q_TILE:], i == 0)
    sc = _dot_nt(kc_ref[0, g], q_near)
    q_wfar = _with_mask(qst, _range_mask(i - WIN_TILES, i - 2, bound))
    s_wf = _dot_nt(_k_window(kw_ref, g, i - WIN_TILES, WIN_TILES - 1), q_wfar)
    edge = jnp.where(i >= WIN_TILES, nb[:Q_TILE], 0.0)
    s_wf = jnp.concatenate([s_wf[:Q_TILE] + edge, s_wf[Q_TILE:]], axis=0)
    s_wn = _dot_nt(_k_window(kw_ref, g, i - 1, 2), q_near) + nb_near
    return qst, nb_near, sc, s_wf, s_wn


def _nsa_compressed(g, i, bound, sc, vct_ref, psum_ref, score_out):
    ncp = sc.shape[0]
    tok = i * Q_TILE + (lax.broadcasted_iota(jnp.int32, (1, ROWS), 1) & (Q_TILE - 1))
    last_visible = (tok - (CMP_LEN - 1)) // CMP_STRIDE
    z = jnp.where(lax.broadcasted_iota(jnp.int32, (ncp, ROWS), 0) <= last_visible, sc, NEG)
    if bound is None:
        m = _colmax(z)
        m = jnp.where(m > 0.5 * NEG, m, 0.0)
    else:
        m = bound
    e = jnp.exp2(z - m)
    p = e * (1.0 / jnp.maximum(jnp.sum(e, axis=0, keepdims=True), 1e-30))
    o_cmp = _dot(vct_ref[0, g], p.astype(MXU_DTYPE))
    psum_ref[0:_IMP_PAD] = jnp.zeros((_IMP_PAD, Q_TILE), F32)
    psum_ref[_IMP_PAD:_IMP_PAD + ncp] = sum(p[:, r * Q_TILE:(r + 1) * Q_TILE] for r in range(REP))
    if psum_ref.shape[0] > _IMP_PAD + ncp:
        psum_ref[_IMP_PAD + ncp:] = jnp.zeros((psum_ref.shape[0] - _IMP_PAD - ncp, Q_TILE), F32)
    imp = sum(w * psum_ref[pl.ds(_IMP_PAD + k, SEL_LEN, stride=SEL_LEN // CMP_STRIDE), :]
              for k, w in _IMP_TAPS)

    blk = lax.broadcasted_iota(jnp.int32, (SEL_LEN, Q_TILE), 0)
    cur = (i * Q_TILE + lax.broadcasted_iota(jnp.int32, (SEL_LEN, Q_TILE), 1)) // SEL_LEN
    bonus = jnp.where(blk == 0, FORCE_BONUS,
                      jnp.where(blk == cur, FORCE_BONUS,
                                jnp.where(blk == cur - 1, FORCE_BONUS, 0.0)))
    score_out[...] = jnp.where(blk <= cur, imp + bonus, NEG)
    return o_cmp


def _nsa_window(g, i, bound, s_wf, s_wn, vwt_ref):
    state = _accumulate(_empty_state(bound, ROWS), s_wf,
                        _vt_window(vwt_ref, g, i - WIN_TILES, WIN_TILES - 1), bound)
    return _values(_accumulate(state, s_wn, _vt_window(vwt_ref, g, i - 1, 2), bound))


def _nsa_select(i, bound, score_in, topk, nslab):
    sub = lax.broadcasted_iota(jnp.int32, (8, Q_TILE), 0)
    cur_t = (i * Q_TILE + lax.broadcasted_iota(jnp.int32, (8, Q_TILE), 1)) // SEL_LEN
    slabs = [score_in[8 * v:8 * (v + 1), :] for v in range(nslab)]
    cnts = [jnp.zeros((8, Q_TILE), F32) for _ in range(nslab)]
    for jp in range(8 * nslab):
        rowv = score_in[jp:jp + 1, :]
        for v in range(nslab):
            ge = jnp.where(rowv >= slabs[v], 1.0, 0.0)
            gt = jnp.where(rowv > slabs[v], 1.0, 0.0)
            if 8 * v > jp:
                beats = ge
            elif 8 * v + 7 < jp:
                beats = gt
            else:
                beats = jnp.where(sub + 8 * v > jp, ge, gt)
            cnts[v] = cnts[v] + beats
    sel_t = [jnp.where(cnts[v] < topk, jnp.where(sub + 8 * v <= cur_t, _open(bound), NEG), NEG)
             for v in range(nslab)]
    closed = lambda rows: [jnp.full((rows, Q_TILE), NEG, F32)] if rows else []
    sel_t = jnp.concatenate(
        closed(LANES - SEL_LEN) + sel_t + closed(SEL_LEN - 8 * nslab), axis=0)
    return jnp.concatenate([sel_t.T] * REP, axis=0)


def _nsa_select_all(units, bound, score_ref, topk):
    last_tile = units[-1][2]
    step_slabs = 1
    levels = SEL_LEN // 8 // step_slabs
    level = (2 * last_tile + 1) // (8 * step_slabs)

    def branch(k):
        return lambda: tuple(_nsa_select(i, bound, score_ref.at[u], topk, step_slabs * (k + 1))
                             for u, (t, g, i) in enumerate(units))

    return lax.switch(jnp.minimum(level, levels - 1), [branch(k) for k in range(levels)])


def _nsa_selected_queries(i, qst, selm):
    blk_row = lax.broadcasted_iota(jnp.int32, (1, LANES), 1) - HEAD_DIM
    return _with_mask(qst, selm), _with_mask(qst, jnp.where(blk_row >= 2 * (i - 1), NEG, selm))


def _attn_b_body(bound, q_ref, kc_ref, vct_ref, ks_ref, vst_ref, kw_ref, vwt_ref, sz_ref, gb_ref,
                 nb_ref, o_ref, score_ref, psum_ref, *, topk):
    tiles = q_ref.shape[1] // Q_TILE
    units = [(t, g, pl.program_id(1) * tiles + t) for t in range(tiles) for g in range(KV_GROUPS)]
    scores = [_nsa_scores(t, g, i, bound, q_ref, kc_ref, kw_ref, nb_ref) for t, g, i in units]
    o_win = [_nsa_window(g, i, bound, scores[u][3], scores[u][4], vwt_ref)
             for u, (t, g, i) in enumerate(units)]
    o_cmp = [_nsa_compressed(g, i, bound, scores[u][2], vct_ref, psum_ref.at[u], score_ref.at[u])
             for u, (t, g, i) in enumerate(units)]
    selm = _nsa_select_all(units, bound, score_ref, topk)
    qs = [_nsa_selected_queries(i, scores[u][0], selm[u]) for u, (t, g, i) in enumerate(units)]

    def far_windows(windows, carry):
        s_c = [[_dot_nt(_k_window(ks_ref, g, first, n), qs[u][1])
                for u, (t, g, i) in enumerate(units)] for first, n in windows]
        states = [carry[nstate * u:nstate * (u + 1)] for u in range(len(units))]
        for w, (first, n) in enumerate(windows):
            for u, (t, g, i) in enumerate(units):
                states[u] = _accumulate(states[u], s_c[w][u], _vt_window(vst_ref, g, first, n),
                                        bound)
        return tuple(v for st in states for v in st)

    chunk = lambda c: (c * FAR_TILES, FAR_TILES)
    empty = _empty_state(bound, ROWS)
    nstate = len(empty)
    need = jnp.maximum(units[-1][2] - 1, 0)
    nfar, rest = need // FAR_TILES, need % FAR_TILES
    far = lax.fori_loop(0, nfar // 2,
                        lambda p, carry: far_windows([chunk(2 * p), chunk(2 * p + 1)], carry),
                        empty * len(units))
    far = lax.cond(nfar % 2 == 1, lambda carry: far_windows([chunk(nfar - 1)], carry),
                   lambda carry: carry, far)
    for n in sorted({(tiles * s + tiles - 2) % FAR_TILES for s in range(FAR_TILES)} - {0}):
        far = lax.cond(rest == n, lambda carry, n=n: far_windows([(nfar * FAR_TILES, n)], carry),
                       lambda carry: carry, far)

    s_near = [_dot_nt(_k_window(ks_ref, g, i - 1, 2), qs[u][0]) + scores[u][1]
              for u, (t, g, i) in enumerate(units)]
    for u, (t, g, i) in enumerate(units):
        o_sel = _values(_accumulate(far[nstate * u:nstate * (u + 1)], s_near[u],
                                    _vt_window(vst_ref, g, i - 1, 2), bound))
        gates = gb_ref[0, g, _tile_rows(t)].T
        mix = []
        for r in range(REP):
            cols = slice(r * Q_TILE, (r + 1) * Q_TILE)
            mix.append(gates[3 * r:3 * r + 1] * o_cmp[u][:, cols]
                       + gates[3 * r + 1:3 * r + 2] * o_sel[:, cols]
                       + gates[3 * r + 2:3 * r + 3] * o_win[u][:, cols])
        _store_heads(jnp.concatenate(mix, axis=1), sz_ref, o_ref, t, g)


def _attn_b_kernel(par_ref, *refs, topk):
    _both_paths(par_ref, lambda bound: _attn_b_body(bound, *refs, topk=topk))


def _attn_b_call(par, qb, kcmp, vcmpt, ks, vst, kw, vwt, szb, gb, nbw):
    bsz, s, qwidth = qb.shape
    owidth = szb.shape[2]
    full = lambda a: pl.BlockSpec((1,) + a.shape[1:], lambda b, i: (b,) + (0,) * (a.ndim - 1))
    rows = NSA_STEP_TILES * Q_TILE
    chains = NSA_STEP_TILES * KV_GROUPS
    imp_rows = HEAD_DIM * (SEL_LEN // CMP_STRIDE)
    return pl.pallas_call(
        functools.partial(_attn_b_kernel, topk=min(SEL_TOPK, s // SEL_LEN)),
        grid=(bsz, s // rows),
        in_specs=[pl.BlockSpec(memory_space=pltpu.SMEM),
                  pl.BlockSpec((1, rows, qwidth), lambda b, i: (b, i, 0)),
                  full(kcmp), full(vcmpt), full(ks), full(vst), full(kw), full(vwt),
                  pl.BlockSpec((1, rows, owidth), lambda b, i: (b, i, 0)),
                  pl.BlockSpec((1, KV_GROUPS, rows, LANES), lambda b, i: (b, 0, i, 0)),
                  pl.BlockSpec(nbw.shape, lambda b, i: (0, 0, 0))],
        out_specs=pl.BlockSpec((1, rows, owidth), lambda b, i: (b, i, 0)),
        out_shape=jax.ShapeDtypeStruct((bsz, s, owidth), MXU_DTYPE),
        scratch_shapes=[pltpu.VMEM((chains, SEL_LEN, Q_TILE), F32),
                        pltpu.VMEM((chains, 2 * _IMP_PAD + max(kcmp.shape[2], imp_rows), Q_TILE), F32)],
        compiler_params=pltpu.CompilerParams(
            dimension_semantics=("arbitrary", "arbitrary"), vmem_limit_bytes=VMEM_LIMIT),
        name="attn_nsa",
    )(par, qb, kcmp, vcmpt, ks, vst, kw, vwt, szb, gb, nbw)


def _out_kernel(x_ref, mod_ref, ya_ref, yb_ref, w_ref, o_ref):
    half = ya_ref.shape[2]
    out = _dot(ya_ref[0], w_ref[0:half, :]) + _dot(yb_ref[0], w_ref[half:2 * half, :])
    o_ref[0] = x_ref[0] + mod_ref[0, 2:3, :] * out


def _out_call(x, mod3, ya, yb, w_out):
    bsz, s, d = x.shape
    tm = OUT_TM
    xs = pl.BlockSpec((1, tm, d), lambda b, i: (b, i, 0))
    ys = pl.BlockSpec((1, tm, 512), lambda b, i: (b, i, 0))
    return pl.pallas_call(
        _out_kernel,
        grid=(bsz, s // tm),
        in_specs=[xs, pl.BlockSpec((1, 3, d), lambda b, i: (b, 0, 0)), ys, ys,
                  pl.BlockSpec(w_out.shape, lambda b, i: (0, 0))],
        out_specs=xs,
        out_shape=jax.ShapeDtypeStruct(x.shape, x.dtype),
        compiler_params=pltpu.CompilerParams(
            dimension_semantics=("arbitrary", "arbitrary"), vmem_limit_bytes=VMEM_LIMIT),
        name="out_proj_residual",
    )(x, mod3, ya, yb, w_out)


def _t5_bucket(dist):
    n = np.maximum(dist, 0)
    max_exact = N_BUCKETS // 2
    nf = np.maximum(n, 1).astype(np.float32)
    ratio = np.log(nf / np.float32(max_exact)) / np.float32(math.log(MAX_DISTANCE / max_exact))
    large = max_exact + (ratio * np.float32(N_BUCKETS - max_exact)).astype(np.int32)
    large = np.minimum(large, N_BUCKETS - 1)
    return np.where(n < max_exact, n, large).astype(np.int32)


def _bias_table_kernel(rel_ref, idx_ref, idx_edge_ref, nba_ref, nbw_ref):
    h = pl.program_id(0)
    hb = h + pl.num_programs(0)

    def lookup(idx, head):
        acc = jnp.zeros(idx.shape, F32)
        for b in range(N_BUCKETS):
            acc = jnp.where(idx == b, rel_ref[b, head], acc)
        return acc

    idx = idx_ref[...]
    dist = (lax.broadcasted_iota(jnp.int32, idx.shape, 1) + Q_TILE
            - lax.broadcasted_iota(jnp.int32, idx.shape, 0))
    causal = dist >= 0
    far = rel_ref[N_BUCKETS - 1, hb]
    nba_ref[0] = jnp.where(causal, jnp.where(dist < SWA_WINDOW, lookup(idx, h) * LOG2E, NEG), NEG)
    near_b = jnp.where(causal, (lookup(idx, hb) - far) * LOG2E, NEG)
    idx_e = idx_edge_ref[...]
    dist_e = (lax.broadcasted_iota(jnp.int32, idx_e.shape, 1) + NSA_WINDOW
              - lax.broadcasted_iota(jnp.int32, idx_e.shape, 0))
    edge_b = jnp.where(dist_e < NSA_WINDOW, (lookup(idx_e, hb) - far) * LOG2E, NEG)
    nbw_ref[0] = jnp.concatenate([edge_b, near_b], axis=0)


def _near_tables(rel_bias):
    nheads = rel_bias.shape[1] // 2
    tq = np.arange(Q_TILE)[None, :]
    idx = _t5_bucket(tq + Q_TILE - np.arange(2 * Q_TILE)[:, None])
    idx_edge = _t5_bucket(tq + NSA_WINDOW - np.arange(Q_TILE)[:, None])
    return pl.pallas_call(
        _bias_table_kernel,
        grid=(nheads,),
        in_specs=[pl.BlockSpec(memory_space=pltpu.SMEM),
                  pl.BlockSpec(idx.shape, lambda h: (0, 0)),
                  pl.BlockSpec(idx_edge.shape, lambda h: (0, 0))],
        out_specs=[pl.BlockSpec((1, 2 * Q_TILE, Q_TILE), lambda h: (h // REP, 0, h % REP)),
                   pl.BlockSpec((1, 3 * Q_TILE, Q_TILE), lambda h: (h // REP, 0, h % REP))],
        out_shape=[jax.ShapeDtypeStruct((nheads // REP, 2 * Q_TILE, ROWS), F32),
                   jax.ShapeDtypeStruct((nheads // REP, 3 * Q_TILE, ROWS), F32)],
        name="t5_bias_tables",
    )(rel_bias, idx, idx_edge)


def _compress_weights(w1, pos):
    hid = w1.shape[1]
    half = CMP_LEN // 2
    w1r = w1.reshape(CMP_LEN, HEAD_DIM, hid)
    eye = jnp.eye(KV_GROUPS, dtype=w1.dtype)
    expand = lambda w: jnp.einsum("ldj,gh->lgdhj", w, eye).reshape(
        half * KV_GROUPS * HEAD_DIM, KV_GROUPS * hid).astype(MXU_DTYPE)
    prow = lambda p: jnp.broadcast_to(p[:, None, :], (half, KV_GROUPS, HEAD_DIM)).reshape(1, -1)
    return expand(w1r[:half]), expand(w1r[half:]), prow(pos[:half]), prow(pos[half:])


def _logit_bound(q_gain, k_gains, bias, floor=None):
    gk = jnp.max(jnp.stack([jnp.max(jnp.abs(k)) for k in k_gains]))
    m = 1.02 * HEAD_DIM * jnp.max(jnp.abs(q_gain)) * gk + jnp.max(jnp.abs(bias))
    if floor is not None:
        m = jnp.maximum(m, jnp.max(floor))
    m = jnp.ceil(m).astype(F32)
    return jnp.stack([m, (m <= MAX_BOUND).astype(F32)])


def _upper_zero(row):
    return jnp.concatenate([row, jnp.zeros_like(row)]).reshape(1, LANES).astype(F32)


def _layer(x, c, w_ada, b_ada, norm_gain, w_in, b_nsa_gate, q_gain_a, k_gain_a, sinks, q_gain_b,
           k_gain_cmp, k_gain_sel, k_gain_win, cmp_pos_k, cmp_pos_v, w_cmp_k1, w_cmp_k2,
           w_cmp_v1, w_cmp_v2, w_out, rel_bias):
    bsz, s, d = x.shape
    assert s % (FAR_TILES * Q_TILE) == 0 and s // SEL_LEN <= HEAD_DIM and s // Q_TILE >= WIN_TILES
    assert FAR_TILES % NSA_STEP_TILES == 0 and s % (SWA_STEP_TILES * Q_TILE) == 0
    assert w_in.shape == (d, D_PROJ) and s % PROJ_TM == 0 and s % OUT_TM == 0
    qscale = HEAD_DIM ** -0.5 * LOG2E

    mod3 = _mod_call(c, w_ada, b_ada).reshape(bsz, 3, d)
    w_in_p = jnp.pad(w_in, ((0, 0), (0, D_PROJ_PAD - D_PROJ))).astype(MXU_DTYPE)
    tile2 = lambda gn: jnp.concatenate([gn, gn]).astype(F32)
    gains = jnp.zeros((8, LANES), F32)
    for n, gn in enumerate((k_gain_a, k_gain_sel, k_gain_win, q_gain_a * qscale, q_gain_b * qscale)):
        gains = gains.at[n].set(tile2(gn))
    bgate = jnp.pad(b_nsa_gate, (0, LANES - b_nsa_gate.shape[0])).reshape(1, LANES).astype(F32)
    (qa, ka, vta, sza, qb, kc, vc, ks, vst, kw, vwt, szb, gb) = _proj_call(
        x, mod3, norm_gain.reshape(1, d).astype(F32), w_in_p, gains, bgate)

    ncp = s // CMP_STRIDE
    wkt, wkb, pkt, pkb = _compress_weights(w_cmp_k1, cmp_pos_k)
    wvt, wvb, pvt, pvb = _compress_weights(w_cmp_v1, cmp_pos_v)
    pos4 = jnp.concatenate([pkt, pkb, pvt, pvb], axis=0).astype(F32)
    pad2 = lambda w: jnp.pad(w, ((0, 0), (0, LANES - HEAD_DIM))).astype(MXU_DTYPE)
    half_block = CMP_STRIDE * LANES
    kcmp, vcmpt = _compress_call(kc.reshape(bsz, ncp, half_block), vc.reshape(bsz, ncp, half_block),
                                 wkt, wkb, wvt, wvb, pos4, pad2(w_cmp_k2), pad2(w_cmp_v2),
                                 _upper_zero(k_gain_cmp))

    nba, nbw = _near_tables(rel_bias.astype(F32))
    half = rel_bias.shape[1] // 2
    sinks2 = sinks.astype(F32) * LOG2E
    par_a = _logit_bound(q_gain_a * qscale, [k_gain_a], rel_bias[:, :half] * LOG2E, floor=sinks2)
    par_b = _logit_bound(q_gain_b * qscale, [k_gain_cmp, k_gain_sel, k_gain_win],
                         (rel_bias[:, half:] - rel_bias[N_BUCKETS - 1, half:]) * LOG2E)
    ya = _attn_a_call(par_a, sinks2, qa, ka, vta, sza, nba)
    yb = _attn_b_call(par_b, qb, kcmp, vcmpt, ks, vst, kw, vwt, szb, gb, nbw)
    return _out_call(x, mod3, ya, yb, w_out.astype(MXU_DTYPE))


def kernel(x, c, w_ada, b_ada, norm_gain, w_in, b_nsa_gate, q_gain_a, k_gain_a, sinks, q_gain_b,
           k_gain_cmp, k_gain_sel, k_gain_win, cmp_pos_k, cmp_pos_v, w_cmp_k1, w_cmp_k2,
           w_cmp_v1, w_cmp_v2, w_out, rel_bias):
    for l in range(w_ada.shape[0]):
        x = _layer(x, c, w_ada[l], b_ada[l], norm_gain[l], w_in[l], b_nsa_gate[l], q_gain_a[l],
                   k_gain_a[l], sinks[l], q_gain_b[l], k_gain_cmp[l], k_gain_sel[l],
                   k_gain_win[l], cmp_pos_k[l], cmp_pos_v[l], w_cmp_k1[l], w_cmp_k2[l],
                   w_cmp_v1[l], w_cmp_v2[l], w_out[l], rel_bias)
    return x
```

```python
import functools
import math

import jax
import jax.numpy as jnp
import numpy as np
from jax import lax
from jax.experimental import pallas as pl
from jax.experimental.pallas import tpu as pltpu

MXU_DTYPE = jnp.bfloat16
F32 = jnp.float32

HEAD_DIM = 64
LANES = 128
Q_TILE = 128
KV_GROUPS = 2
REP = 4
ROWS = REP * Q_TILE
FAR_TILES = 4
SWA_STEP_TILES = 16
NSA_STEP_TILES = 2
SWA_WINDOW = 128
NSA_WINDOW = 512
WIN_TILES = NSA_WINDOW // Q_TILE
CMP_LEN = 32
CMP_STRIDE = 16
SEL_LEN = 64
SEL_TOPK = 16
N_BUCKETS = 32
MAX_DISTANCE = 128
FORCE_BONUS = 1e4
EPS = 1e-6
NEG = -1e30
LOG2E = 1.4426950408889634
MAX_BOUND = 40.0
PROJ_TM = 1024
OUT_TM = 1024
PROJ_SUBTILES = 4
VMEM_LIMIT = 48 * 1024 * 1024

_IMP_TAPS = tuple(
    (k, (min(CMP_STRIDE * k + CMP_LEN, SEL_LEN) - max(CMP_STRIDE * k, 0)) / CMP_LEN)
    for k in range(-(CMP_LEN // CMP_STRIDE) + 1, SEL_LEN // CMP_STRIDE))
_IMP_PAD = 8

OFF_QA, OFF_KA, OFF_ZA, OFF_QB, OFF_KC, OFF_KS, OFF_KW, OFF_ZB, OFF_GB = (
    0, 512, 768, 1280, 1792, 2048, 2304, 2560, 3072)
D_PROJ = 3096
D_PROJ_PAD = 3200


def _dot(a, b):
    return jnp.dot(a, b, preferred_element_type=F32)


def _dot_nt(a, b):
    return lax.dot_general(a, b, (((1,), (1,)), ((), ())), preferred_element_type=F32)


def _split3(x):
    hi = x.astype(MXU_DTYPE)
    r1 = x - hi.astype(F32)
    mid = r1.astype(MXU_DTYPE)
    lo = (r1 - mid.astype(F32)).astype(MXU_DTYPE)
    return hi, mid, lo


def _mod_kernel(c_ref, w_ref, b_ref, o_ref):
    sc = jax.nn.silu(c_ref[...])
    w = w_ref[...]
    acc = jnp.zeros(o_ref.shape, F32)
    for a in _split3(sc):
        for b in _split3(w)[:2]:
            acc = acc + _dot(a, b)
    o_ref[...] = acc + b_ref[...]


def _mod_call(c, w_ada, b_ada):
    bsz, d = c.shape
    n = w_ada.shape[1]
    tn = 512
    return pl.pallas_call(
        _mod_kernel,
        grid=(n // tn,),
        in_specs=[pl.BlockSpec((bsz, d), lambda j: (0, 0)),
                  pl.BlockSpec((d, tn), lambda j: (0, j)),
                  pl.BlockSpec((1, tn), lambda j: (0, j))],
        out_specs=pl.BlockSpec((bsz, tn), lambda j: (0, j)),
        out_shape=jax.ShapeDtypeStruct((bsz, n), F32),
        name="adaln_mod",
    )(c, w_ada, b_ada.reshape(1, n))


def _proj_kernel(x_ref, mod_ref, gain_ref, w_ref, kg_ref, bg_ref,
                 qa_ref, ka_ref, va_ref, sza_ref, qb_ref, kc_ref, vc_ref,
                 ks_ref, vs_ref, kw_ref, vw_ref, szb_ref, gb_ref, kvc_ref):
    tm = x_ref.shape[1]
    ts = tm // PROJ_SUBTILES
    si = pl.program_id(1)

    def normed(sub):
        x = x_ref[0, sub * ts:(sub + 1) * ts]
        ms = jnp.mean(x * x, axis=-1, keepdims=True)
        y = x * lax.rsqrt(ms + EPS) * gain_ref[...]
        h = y * (1.0 + mod_ref[0, 1:2, :]) + mod_ref[0, 0:1, :]
        return h.astype(MXU_DTYPE)

    hbs = [normed(sub) for sub in range(PROJ_SUBTILES)]
    lane = lax.broadcasted_iota(jnp.int32, (ts, LANES), 1)
    lo = lane < HEAD_DIM

    def half_norm(t, gain_row):
        sq = t * t
        s_lo = jnp.sum(jnp.where(lo, sq, 0.0), axis=-1, keepdims=True)
        s_hi = jnp.sum(jnp.where(lo, 0.0, sq), axis=-1, keepdims=True)
        inv = jnp.where(lo, lax.rsqrt(s_lo * (1.0 / HEAD_DIM) + EPS),
                        lax.rsqrt(s_hi * (1.0 / HEAD_DIM) + EPS))
        return t * inv * gain_row

    def split_heads(t, extra):
        return (jnp.where(lo, t, extra), jnp.where(lo, pltpu.roll(t, HEAD_DIM, axis=1), extra))

    for sub, hb in enumerate(hbs):
        rows = slice(sub * ts, (sub + 1) * ts)
        row = lax.broadcasted_iota(jnp.int32, (ts, LANES), 0) + si * tm + sub * ts
        onehot = jnp.where(lane - HEAD_DIM == row // SEL_LEN, 1.0, 0.0)

        def seg(off, n):
            return _dot(hb, w_ref[:, off:off + n])

        def write_q(ref, off, gain_row):
            t = seg(off, REP * KV_GROUPS * HEAD_DIM)
            for c in range(REP * KV_GROUPS // 2):
                pair = split_heads(half_norm(t[:, LANES * c:LANES * (c + 1)], gain_row), 0.0)
                for j in range(2):
                    col = LANES * (2 * c + j)
                    ref[0, rows, col:col + LANES] = pair[j].astype(ref.dtype)

        def write_kv(k_ref, vt_ref, off, gain_row):
            kv = seg(off, 2 * LANES)
            for g, t in enumerate(split_heads(half_norm(kv[:, :LANES], gain_row), onehot)):
                k_ref[0, g, rows] = t.astype(k_ref.dtype)
            if vt_ref.shape[3] > HEAD_DIM:
                vts = [t.T for t in split_heads(kv[:, LANES:], 1.0)]
            else:
                vt = kv[:, LANES:].T
                vts = [vt[HEAD_DIM * g:HEAD_DIM * (g + 1)] for g in range(KV_GROUPS)]
            for g in range(KV_GROUPS):
                for j in range(ts // Q_TILE):
                    vt_ref[0, g, sub * (ts // Q_TILE) + j] = (
                        vts[g][:, Q_TILE * j:Q_TILE * (j + 1)].astype(vt_ref.dtype))

        write_q(qa_ref, OFF_QA, kg_ref[3:4, :])
        write_kv(ka_ref, va_ref, OFF_KA, kg_ref[0:1, :])
        sza_ref[0, rows] = jax.nn.silu(seg(OFF_ZA, 512)).astype(sza_ref.dtype)
        write_q(qb_ref, OFF_QB, kg_ref[4:5, :])
        kvc = seg(OFF_KC, 2 * LANES)
        kvc_ref[sub, 0] = kvc[:, :LANES]
        kvc_ref[sub, 1] = kvc[:, LANES:]
        nhb = ts // CMP_STRIDE
        for l in range(CMP_STRIDE):
            hb_rows, hb_cols = slice(sub * nhb, (sub + 1) * nhb), slice(LANES * l, LANES * (l + 1))
            for n, ref in enumerate((kc_ref, vc_ref)):
                tok = kvc_ref[sub, n, pl.ds(l, nhb, stride=CMP_STRIDE), :]
                ref[0, hb_rows, hb_cols] = tok.astype(ref.dtype)
        write_kv(ks_ref, vs_ref, OFF_KS, kg_ref[1:2, :])
        write_kv(kw_ref, vw_ref, OFF_KW, kg_ref[2:3, :])
        szb_ref[0, rows] = jax.nn.silu(seg(OFF_ZB, 512)).astype(szb_ref.dtype)
        gates = jax.nn.sigmoid(seg(OFF_GB, LANES) + bg_ref[...])
        gb_ref[0, 0, rows] = gates
        gb_ref[0, 1, rows] = pltpu.roll(gates, LANES - REP * 3, axis=1)


def _proj_call(x, mod3, norm_gain, w_in_p, gains, bgate):
    bsz, s, d = x.shape
    tm = PROJ_TM
    dt = MXU_DTYPE
    nt = tm // Q_TILE
    qwidth = REP * KV_GROUPS * LANES
    rowq = pl.BlockSpec((1, tm, qwidth), lambda b, i: (b, i, 0))
    row512 = pl.BlockSpec((1, tm, 512), lambda b, i: (b, i, 0))
    row128 = pl.BlockSpec((1, tm // CMP_STRIDE, CMP_STRIDE * LANES), lambda b, i: (b, i, 0))
    grp = pl.BlockSpec((1, KV_GROUPS, tm, LANES), lambda b, i: (b, 0, i, 0))
    grpt = pl.BlockSpec((1, KV_GROUPS, nt, HEAD_DIM, Q_TILE), lambda b, i: (b, 0, i, 0, 0))
    grpt1 = pl.BlockSpec((1, KV_GROUPS, nt, LANES, Q_TILE), lambda b, i: (b, 0, i, 0, 0))
    sgrpt1 = jax.ShapeDtypeStruct((bsz, KV_GROUPS, s // Q_TILE, LANES, Q_TILE), dt)
    sq = jax.ShapeDtypeStruct((bsz, s, qwidth), dt)
    s512 = jax.ShapeDtypeStruct((bsz, s, 512), dt)
    s128 = jax.ShapeDtypeStruct((bsz, s // CMP_STRIDE, CMP_STRIDE * LANES), dt)
    sgrp = jax.ShapeDtypeStruct((bsz, KV_GROUPS, s, LANES), dt)
    sgrpt = jax.ShapeDtypeStruct((bsz, KV_GROUPS, s // Q_TILE, HEAD_DIM, Q_TILE), dt)
    return pl.pallas_call(
        _proj_kernel,
        grid=(bsz, s // tm),
        in_specs=[pl.BlockSpec((1, tm, d), lambda b, i: (b, i, 0)),
                  pl.BlockSpec((1, 3, d), lambda b, i: (b, 0, 0)),
                  pl.BlockSpec((1, d), lambda b, i: (0, 0)),
                  pl.BlockSpec((d, D_PROJ_PAD), lambda b, i: (0, 0)),
                  pl.BlockSpec((8, LANES), lambda b, i: (0, 0)),
                  pl.BlockSpec((1, LANES), lambda b, i: (0, 0))],
        out_specs=[rowq, grp, grpt1, row512, rowq, row128, row128, grp, grpt, grp, grpt, row512, grp],
        out_shape=[sq, sgrp, sgrpt1, s512, sq, s128, s128, sgrp, sgrpt, sgrp, sgrpt, s512,
                   jax.ShapeDtypeStruct((bsz, KV_GROUPS, s, LANES), F32)],
        scratch_shapes=[pltpu.VMEM((PROJ_SUBTILES, 2, tm // PROJ_SUBTILES, LANES), F32)],
        compiler_params=pltpu.CompilerParams(
            dimension_semantics=("arbitrary", "arbitrary"), vmem_limit_bytes=VMEM_LIMIT),
        name="norm_in_proj",
    )(x, mod3, norm_gain, w_in_p, gains, bgate)


def _compress_kernel(kc_ref, vc_ref, wkt_ref, wkb_ref, wvt_ref, wvb_ref, pos_ref, w2k_ref, w2v_ref,
                     kg_ref, ko_ref, vo_ref):
    ncp = kc_ref.shape[1]

    def hidden(h_ref, wt_ref, wb_ref, ptop, pbot):
        hf = h_ref[0].astype(F32)
        top = _dot((hf + ptop).astype(MXU_DTYPE), wt_ref[...])
        bot = _dot((hf + pbot).astype(MXU_DTYPE), wb_ref[...])
        pre = top + pltpu.roll(bot, ncp - 1, axis=0)
        return jax.nn.silu(pre).astype(MXU_DTYPE)

    hk = hidden(kc_ref, wkt_ref, wkb_ref, pos_ref[0:1, :], pos_ref[1:2, :])
    hv = hidden(vc_ref, wvt_ref, wvb_ref, pos_ref[2:3, :], pos_ref[3:4, :])
    nh = w2k_ref.shape[0]
    for g in range(KV_GROUPS):
        k = _dot(hk[:, g * nh:(g + 1) * nh], w2k_ref[...])
        ss = jnp.sum(k * k, axis=-1, keepdims=True) * (1.0 / HEAD_DIM)
        ko_ref[0, g] = (k * lax.rsqrt(ss + EPS) * kg_ref[...]).astype(ko_ref.dtype)
        v = _dot(hv[:, g * nh:(g + 1) * nh], w2v_ref[...])
        vo_ref[0, g] = v.T[:HEAD_DIM].astype(vo_ref.dtype)


def _compress_call(kc_r, vc_r, wkt, wkb, wvt, wvb, pos4, w2k, w2v, kgain):
    bsz, ncp, width = kc_r.shape
    full = lambda a: pl.BlockSpec(a.shape, lambda b: (0,) * a.ndim)
    return pl.pallas_call(
        _compress_kernel,
        grid=(bsz,),
        in_specs=[pl.BlockSpec((1, ncp, width), lambda b: (b, 0, 0)),
                  pl.BlockSpec((1, ncp, width), lambda b: (b, 0, 0)),
                  full(wkt), full(wkb), full(wvt), full(wvb), full(pos4), full(w2k), full(w2v),
                  full(kgain)],
        out_specs=[pl.BlockSpec((1, KV_GROUPS, ncp, LANES), lambda b: (b, 0, 0, 0)),
                   pl.BlockSpec((1, KV_GROUPS, HEAD_DIM, ncp), lambda b: (b, 0, 0, 0))],
        out_shape=[jax.ShapeDtypeStruct((bsz, KV_GROUPS, ncp, LANES), MXU_DTYPE),
                   jax.ShapeDtypeStruct((bsz, KV_GROUPS, HEAD_DIM, ncp), MXU_DTYPE)],
        compiler_params=pltpu.CompilerParams(
            dimension_semantics=("arbitrary",), vmem_limit_bytes=VMEM_LIMIT),
        name="nsa_compress",
    )(kc_r, vc_r, wkt, wkb, wvt, wvb, pos4, w2k, w2v, kgain)


def _tile_rows(t):
    return slice(t * Q_TILE, (t + 1) * Q_TILE)


def _stacked_queries(q_ref, t, g):
    return jnp.concatenate(
        [q_ref[0, _tile_rows(t), LANES * (REP * g + r):LANES * (REP * g + r + 1)]
         for r in range(REP)], axis=0)


def _with_mask(qst, mask):
    lo = lax.broadcasted_iota(jnp.int32, (1, LANES), 1) < HEAD_DIM
    return jnp.where(lo, qst, mask.astype(qst.dtype))


def _open(bound):
    return 0.0 if bound is None else -bound


def _range_mask(first_tile, last_tile, bound):
    blk = lax.broadcasted_iota(jnp.int32, (1, LANES), 1) - HEAD_DIM
    return jnp.where(blk < 2 * first_tile, NEG,
                     jnp.where(blk > 2 * last_tile + 1, NEG, _open(bound)))


def _k_window(ref, g, first_tile, ntiles):
    start = pl.multiple_of(jnp.maximum(first_tile, 0) * Q_TILE, Q_TILE)
    return ref[0, g, pl.ds(start, ntiles * Q_TILE), :]


def _vt_window(ref, g, first_tile, ntiles):
    start = jnp.maximum(first_tile, 0)
    return jnp.concatenate([ref[0, g, start + t] for t in range(ntiles)], axis=1)


def _near_table(nb, first):
    diag = nb[Q_TILE:]
    return jnp.concatenate([jnp.where(first, diag, nb[:Q_TILE]), diag], axis=0)


def _colmax(s):
    return jnp.max(s, axis=0, keepdims=True)


def _empty_state(bound, cols):
    zero = (jnp.zeros((1, cols), F32), jnp.zeros((HEAD_DIM, cols), F32))
    return zero if bound is not None else (jnp.full((1, cols), NEG, F32),) + zero


def _accumulate(state, s, vt, bound):
    if vt.shape[0] > HEAD_DIM:
        colsum = lambda e: 0.0
    else:
        colsum = lambda e: jnp.sum(e, axis=0, keepdims=True)
    if bound is not None:
        l, acc = state
        e = jnp.exp2(s)
        return l + colsum(e), acc + _dot(vt, e.astype(MXU_DTYPE))
    m, l, acc = state
    m_new = jnp.maximum(m, _colmax(s))
    alpha = jnp.exp2(m - m_new)
    e = jnp.exp2(s - m_new)
    return m_new, alpha * l + colsum(e), alpha * acc + _dot(vt, e.astype(MXU_DTYPE))


def _values(state):
    l, acc = state[-2], state[-1]
    if acc.shape[0] > HEAD_DIM:
        l = l + acc[HEAD_DIM:HEAD_DIM + 1]
    return acc[:HEAD_DIM] / l


def _store_heads(o_t, sz_ref, o_ref, t, g):
    rows = _tile_rows(t)
    for c in range(REP // 2):
        pair = jnp.concatenate([o_t[:, Q_TILE * (2 * c):Q_TILE * (2 * c + 1)],
                                o_t[:, Q_TILE * (2 * c + 1):Q_TILE * (2 * c + 2)]], axis=0)
        cols = slice(LANES * (2 * g + c), LANES * (2 * g + c + 1))
        o_ref[0, rows, cols] = (pair.T * sz_ref[0, rows, cols].astype(F32)).astype(o_ref.dtype)


def _both_paths(par_ref, body):
    bounded = par_ref[1] > 0.5
    pl.when(bounded)(lambda: body(par_ref[0]))
    pl.when(jnp.logical_not(bounded))(lambda: body(None))


def _attn_a_body(bound, sinks_ref, q_ref, k_ref, vt_ref, sz_ref, nb_ref, o_ref):
    tiles = q_ref.shape[1] // Q_TILE
    units = [(t, g) for t in range(tiles) for g in range(KV_GROUPS)]
    tile = lambda t: pl.program_id(1) * tiles + t
    sinks = [jnp.concatenate([jnp.full((1, Q_TILE), sinks_ref[REP * g + r], F32)
                              for r in range(REP)], axis=1) for g in range(KV_GROUPS)]
    scores = []
    for t, g in units:
        i = tile(t)
        q = _with_mask(_stacked_queries(q_ref, t, g), _range_mask(i - 1, i, bound))
        nb = _near_table(nb_ref[g], i == 0)
        scores.append(_dot_nt(_k_window(k_ref, g, i - 1, 2), q) + nb)
    outs = []
    for (t, g), s in zip(units, scores):
        zeros = jnp.zeros((vt_ref.shape[3], ROWS), F32)
        if bound is not None:
            state = (jnp.exp2(sinks[g] - bound), zeros)
        else:
            state = (sinks[g], jnp.ones((1, ROWS), F32), zeros)
        outs.append(_values(_accumulate(state, s, _vt_window(vt_ref, g, tile(t) - 1, 2), bound)))
    for (t, g), o in zip(units, outs):
        _store_heads(o, sz_ref, o_ref, t, g)


def _attn_a_kernel(par_ref, *refs):
    _both_paths(par_ref, lambda bound: _attn_a_body(bound, *refs))


def _attn_a_call(par, sinks, qa, ka, vta, sza, nba):
    bsz, s, qwidth = qa.shape
    owidth = sza.shape[2]
    full = lambda a: pl.BlockSpec((1,) + a.shape[1:], lambda b, i: (b,) + (0,) * (a.ndim - 1))
    smem = pl.BlockSpec(memory_space=pltpu.SMEM)
    rows = SWA_STEP_TILES * Q_TILE
    return pl.pallas_call(
        _attn_a_kernel,
        grid=(bsz, s // rows),
        in_specs=[smem, smem,
                  pl.BlockSpec((1, rows, qwidth), lambda b, i: (b, i, 0)),
                  full(ka), full(vta),
                  pl.BlockSpec((1, rows, owidth), lambda b, i: (b, i, 0)),
                  pl.BlockSpec(nba.shape, lambda b, i: (0, 0, 0))],
        out_specs=pl.BlockSpec((1, rows, owidth), lambda b, i: (b, i, 0)),
        out_shape=jax.ShapeDtypeStruct((bsz, s, owidth), MXU_DTYPE),
        compiler_params=pltpu.CompilerParams(
            dimension_semantics=("arbitrary", "arbitrary"), vmem_limit_bytes=VMEM_LIMIT),
        name="attn_swa_sink",
    )(par, sinks, qa, ka, vta, sza, nba)


def _nsa_scores(t, g, i, bound, q_ref, kc_ref, kw_ref, nb_ref):
    qst = _stacked_queries(q_ref, t, g)
    q_near = _with_mask(qst, _range_mask(i - 1, i, bound))
    nb = nb_ref[g]
    nb_near = _near_table(nb[Q_TILE:], i == 0)
    sc = _dot_nt(kc_ref[0, g], q_near)
    q_wfar = _with_mask(qst, _range_mask(i - WIN_TILES, i - 2, bound))
    s_wf = _dot_nt(_k_window(kw_ref, g, i - WIN_TILES, WIN_TILES - 1), q_wfar)
    edge = jnp.where(i >= WIN_TILES, nb[:Q_TILE], 0.0)
    s_wf = jnp.concatenate([s_wf[:Q_TILE] + edge, s_wf[Q_TILE:]], axis=0)
    s_wn = _dot_nt(_k_window(kw_ref, g, i - 1, 2), q_near) + nb_near
    return qst, nb_near, sc, s_wf, s_wn


def _nsa_compressed(g, i, bound, sc, vct_ref, psum_ref, score_out):
    ncp = sc.shape[0]
    tok = i * Q_TILE + (lax.broadcasted_iota(jnp.int32, (1, ROWS), 1) & (Q_TILE - 1))
    last_visible = (tok - (CMP_LEN - 1)) // CMP_STRIDE
    z = jnp.where(lax.broadcasted_iota(jnp.int32, (ncp, ROWS), 0) <= last_visible, sc, NEG)
    if bound is None:
        m = _colmax(z)
        m = jnp.where(m > 0.5 * NEG, m, 0.0)
    else:
        m = bound
    e = jnp.exp2(z - m)
    p = e * (1.0 / jnp.maximum(jnp.sum(e, axis=0, keepdims=True), 1e-30))
    o_cmp = _dot(vct_ref[0, g], p.astype(MXU_DTYPE))
    psum_ref[0:_IMP_PAD] = jnp.zeros((_IMP_PAD, Q_TILE), F32)
    psum_ref[_IMP_PAD:_IMP_PAD + ncp] = sum(p[:, r * Q_TILE:(r + 1) * Q_TILE] for r in range(REP))
    if psum_ref.shape[0] > _IMP_PAD + ncp:
        psum_ref[_IMP_PAD + ncp:] = jnp.zeros((psum_ref.shape[0] - _IMP_PAD - ncp, Q_TILE), F32)
    imp = sum(w * psum_ref[pl.ds(_IMP_PAD + k, SEL_LEN, stride=SEL_LEN // CMP_STRIDE), :]
              for k, w in _IMP_TAPS)

    blk = lax.broadcasted_iota(jnp.int32, (SEL_LEN, Q_TILE), 0)
    cur = (i * Q_TILE + lax.broadcasted_iota(jnp.int32, (SEL_LEN, Q_TILE), 1)) // SEL_LEN
    bonus = jnp.where(blk == 0, FORCE_BONUS,
                      jnp.where(blk == cur, FORCE_BONUS,
                                jnp.where(blk == cur - 1, FORCE_BONUS, 0.0)))
    score_out[...] = jnp.where(blk <= cur, imp + bonus, NEG)
    return o_cmp


def _nsa_window(g, i, bound, s_wf, s_wn, vwt_ref):
    state = _accumulate(_empty_state(bound, ROWS), s_wf,
                        _vt_window(vwt_ref, g, i - WIN_TILES, WIN_TILES - 1), bound)
    return _values(_accumulate(state, s_wn, _vt_window(vwt_ref, g, i - 1, 2), bound))


def _nsa_select(i, bound, score_in, topk, nslab):
    sub = lax.broadcasted_iota(jnp.int32, (8, Q_TILE), 0)
    cur_t = (i * Q_TILE + lax.broadcasted_iota(jnp.int32, (8, Q_TILE), 1)) // SEL_LEN
    slabs = [score_in[8 * v:8 * (v + 1), :] for v in range(nslab)]
    cnts = [jnp.zeros((8, Q_TILE), F32) for _ in range(nslab)]
    for jp in range(8 * nslab):
        rowv = score_in[jp:jp + 1, :]
        for v in range(nslab):
            ge = jnp.where(rowv >= slabs[v], 1.0, 0.0)
            gt = jnp.where(rowv > slabs[v], 1.0, 0.0)
            if 8 * v > jp:
                beats = ge
            elif 8 * v + 7 < jp:
                beats = gt
            else:
                beats = jnp.where(sub + 8 * v > jp, ge, gt)
            cnts[v] = cnts[v] + beats
    sel_t = [jnp.where(cnts[v] < topk, jnp.where(sub + 8 * v <= cur_t, _open(bound), NEG), NEG)
             for v in range(nslab)]
    closed = lambda rows: [jnp.full((rows, Q_TILE), NEG, F32)] if rows else []
    sel_t = jnp.concatenate(
        closed(LANES - SEL_LEN) + sel_t + closed(SEL_LEN - 8 * nslab), axis=0)
    return jnp.concatenate([sel_t.T] * REP, axis=0)


def _nsa_select_all(units, bound, score_ref, topk):
    last_tile = units[-1][2]
    step_slabs = 1
    levels = SEL_LEN // 8 // step_slabs
    level = (2 * last_tile + 1) // (8 * step_slabs)

    def branch(k):
        return lambda: tuple(_nsa_select(i, bound, score_ref.at[u], topk, step_slabs * (k + 1))
                             for u, (t, g, i) in enumerate(units))

    return lax.switch(jnp.minimum(level, levels - 1), [branch(k) for k in range(levels)])


def _nsa_selected_queries(i, qst, selm):
    blk_row = lax.broadcasted_iota(jnp.int32, (1, LANES), 1) - HEAD_DIM
    return _with_mask(qst, selm), _with_mask(qst, jnp.where(blk_row >= 2 * (i - 1), NEG, selm))


def _attn_b_body(bound, q_ref, kc_ref, vct_ref, ks_ref, vst_ref, kw_ref, vwt_ref, sz_ref, gb_ref,
                 nb_ref, o_ref, score_ref, psum_ref, *, topk):
    tiles = q_ref.shape[1] // Q_TILE
    units = [(t, g, pl.program_id(1) * tiles + t) for t in range(tiles) for g in range(KV_GROUPS)]
    scores = [_nsa_scores(t, g, i, bound, q_ref, kc_ref, kw_ref, nb_ref) for t, g, i in units]
    o_win = [_nsa_window(g, i, bound, scores[u][3], scores[u][4], vwt_ref)
             for u, (t, g, i) in enumerate(units)]
    o_cmp = [_nsa_compressed(g, i, bound, scores[u][2], vct_ref, psum_ref.at[u], score_ref.at[u])
             for u, (t, g, i) in enumerate(units)]
    selm = _nsa_select_all(units, bound, score_ref, topk)
    qs = [_nsa_selected_queries(i, scores[u][0], selm[u]) for u, (t, g, i) in enumerate(units)]

    def far_windows(windows, carry):
        s_c = [[_dot_nt(_k_window(ks_ref, g, first, n), qs[u][1])
                for u, (t, g, i) in enumerate(units)] for first, n in windows]
        states = [carry[nstate * u:nstate * (u + 1)] for u in range(len(units))]
        for w, (first, n) in enumerate(windows):
            for u, (t, g, i) in enumerate(units):
                states[u] = _accumulate(states[u], s_c[w][u], _vt_window(vst_ref, g, first, n),
                                        bound)
        return tuple(v for st in states for v in st)

    chunk = lambda c: (c * FAR_TILES, FAR_TILES)
    empty = _empty_state(bound, ROWS)
    nstate = len(empty)
    need = jnp.maximum(units[-1][2] - 1, 0)
    nfar, rest = need // FAR_TILES, need % FAR_TILES
    far = lax.fori_loop(0, nfar // 2,
                        lambda p, carry: far_windows([chunk(2 * p), chunk(2 * p + 1)], carry),
                        empty * len(units))
    far = lax.cond(nfar % 2 == 1, lambda carry: far_windows([chunk(nfar - 1)], carry),
                   lambda carry: carry, far)
    for n in sorted({(tiles * s + tiles - 2) % FAR_TILES for s in range(FAR_TILES)} - {0}):
        far = lax.cond(rest == n, lambda carry, n=n: far_windows([(nfar * FAR_TILES, n)], carry),
                       lambda carry: carry, far)

    s_near = [_dot_nt(_k_window(ks_ref, g, i - 1, 2), qs[u][0]) + scores[u][1]
              for u, (t, g, i) in enumerate(units)]
    for u, (t, g, i) in enumerate(units):
        o_sel = _values(_accumulate(far[nstate * u:nstate * (u + 1)], s_near[u],
                                    _vt_window(vst_ref, g, i - 1, 2), bound))
        gates = gb_ref[0, g, _tile_rows(t)].T
        mix = []
        for r in range(REP):
            cols = slice(r * Q_TILE, (r + 1) * Q_TILE)
            mix.append(gates[3 * r:3 * r + 1] * o_cmp[u][:, cols]
                       + gates[3 * r + 1:3 * r + 2] * o_sel[:, cols]
                       + gates[3 * r + 2:3 * r + 3] * o_win[u][:, cols])
        _store_heads(jnp.concatenate(mix, axis=1), sz_ref, o_ref, t, g)


def _attn_b_kernel(par_ref, *refs, topk):
    _both_paths(par_ref, lambda bound: _attn_b_body(bound, *refs, topk=topk))


def _attn_b_call(par, qb, kcmp, vcmpt, ks, vst, kw, vwt, szb, gb, nbw):
    bsz, s, qwidth = qb.shape
    owidth = szb.shape[2]
    full = lambda a: pl.BlockSpec((1,) + a.shape[1:], lambda b, i: (b,) + (0,) * (a.ndim - 1))
    rows = NSA_STEP_TILES * Q_TILE
    chains = NSA_STEP_TILES * KV_GROUPS
    imp_rows = HEAD_DIM * (SEL_LEN // CMP_STRIDE)
    return pl.pallas_call(
        functools.partial(_attn_b_kernel, topk=min(SEL_TOPK, s // SEL_LEN)),
        grid=(bsz, s // rows),
        in_specs=[pl.BlockSpec(memory_space=pltpu.SMEM),
                  pl.BlockSpec((1, rows, qwidth), lambda b, i: (b, i, 0)),
                  full(kcmp), full(vcmpt), full(ks), full(vst), full(kw), full(vwt),
                  pl.BlockSpec((1, rows, owidth), lambda b, i: (b, i, 0)),
                  pl.BlockSpec((1, KV_GROUPS, rows, LANES), lambda b, i: (b, 0, i, 0)),
                  pl.BlockSpec(nbw.shape, lambda b, i: (0, 0, 0))],
        out_specs=pl.BlockSpec((1, rows, owidth), lambda b, i: (b, i, 0)),
        out_shape=jax.ShapeDtypeStruct((bsz, s, owidth), MXU_DTYPE),
        scratch_shapes=[pltpu.VMEM((chains, SEL_LEN, Q_TILE), F32),
                        pltpu.VMEM((chains, 2 * _IMP_PAD + max(kcmp.shape[2], imp_rows), Q_TILE), F32)],
        compiler_params=pltpu.CompilerParams(
            dimension_semantics=("arbitrary", "arbitrary"), vmem_limit_bytes=VMEM_LIMIT),
        name="attn_nsa",
    )(par, qb, kcmp, vcmpt, ks, vst, kw, vwt, szb, gb, nbw)


def _out_kernel(x_ref, mod_ref, ya_ref, yb_ref, w_ref, o_ref):
    half = ya_ref.shape[2]
    out = _dot(ya_ref[0], w_ref[0:half, :]) + _dot(yb_ref[0], w_ref[half:2 * half, :])
    o_ref[0] = x_ref[0] + mod_ref[0, 2:3, :] * out


def _out_call(x, mod3, ya, yb, w_out):
    bsz, s, d = x.shape
    tm = OUT_TM
    xs = pl.BlockSpec((1, tm, d), lambda b, i: (b, i, 0))
    ys = pl.BlockSpec((1, tm, 512), lambda b, i: (b, i, 0))
    return pl.pallas_call(
        _out_kernel,
        grid=(bsz, s // tm),
        in_specs=[xs, pl.BlockSpec((1, 3, d), lambda b, i: (b, 0, 0)), ys, ys,
                  pl.BlockSpec(w_out.shape, lambda b, i: (0, 0))],
        out_specs=xs,
        out_shape=jax.ShapeDtypeStruct(x.shape, x.dtype),
        compiler_params=pltpu.CompilerParams(
            dimension_semantics=("arbitrary", "arbitrary"), vmem_limit_bytes=VMEM_LIMIT),
        name="out_proj_residual",
    )(x, mod3, ya, yb, w_out)


def _t5_bucket(dist):
    n = np.maximum(dist, 0)
    max_exact = N_BUCKETS // 2
    nf = np.maximum(n, 1).astype(np.float32)
    ratio = np.log(nf / np.float32(max_exact)) / np.float32(math.log(MAX_DISTANCE / max_exact))
    large = max_exact + (ratio * np.float32(N_BUCKETS - max_exact)).astype(np.int32)
    large = np.minimum(large, N_BUCKETS - 1)
    return np.where(n < max_exact, n, large).astype(np.int32)


def _bias_table_kernel(rel_ref, idx_ref, idx_edge_ref, nba_ref, nbw_ref):
    h = pl.program_id(0)
    hb = h + pl.num_programs(0)

    def lookup(idx, head):
        acc = jnp.zeros(idx.shape, F32)
        for b in range(N_BUCKETS):
            acc = jnp.where(idx == b, rel_ref[b, head], acc)
        return acc

    idx = idx_ref[...]
    dist = (lax.broadcasted_iota(jnp.int32, idx.shape, 1) + Q_TILE
            - lax.broadcasted_iota(jnp.int32, idx.shape, 0))
    causal = dist >= 0
    far = rel_ref[N_BUCKETS - 1, hb]
    nba_ref[0] = jnp.where(causal, jnp.where(dist < SWA_WINDOW, lookup(idx, h) * LOG2E, NEG), NEG)
    near_b = jnp.where(causal, (lookup(idx, hb) - far) * LOG2E, NEG)
    idx_e = idx_edge_ref[...]
    dist_e = (lax.broadcasted_iota(jnp.int32, idx_e.shape, 1) + NSA_WINDOW
              - lax.broadcasted_iota(jnp.int32, idx_e.shape, 0))
    edge_b = jnp.where(dist_e < NSA_WINDOW, (lookup(idx_e, hb) - far) * LOG2E, NEG)
    nbw_ref[0] = jnp.concatenate([edge_b, near_b], axis=0)


def _near_tables(rel_bias):
    nheads = rel_bias.shape[1] // 2
    tq = np.arange(Q_TILE)[None, :]
    idx = _t5_bucket(tq + Q_TILE - np.arange(2 * Q_TILE)[:, None])
    idx_edge = _t5_bucket(tq + NSA_WINDOW - np.arange(Q_TILE)[:, None])
    return pl.pallas_call(
        _bias_table_kernel,
        grid=(nheads,),
        in_specs=[pl.BlockSpec(memory_space=pltpu.SMEM),
                  pl.BlockSpec(idx.shape, lambda h: (0, 0)),
                  pl.BlockSpec(idx_edge.shape, lambda h: (0, 0))],
        out_specs=[pl.BlockSpec((1, 2 * Q_TILE, Q_TILE), lambda h: (h // REP, 0, h % REP)),
                   pl.BlockSpec((1, 3 * Q_TILE, Q_TILE), lambda h: (h // REP, 0, h % REP))],
        out_shape=[jax.ShapeDtypeStruct((nheads // REP, 2 * Q_TILE, ROWS), F32),
                   jax.ShapeDtypeStruct((nheads // REP, 3 * Q_TILE, ROWS), F32)],
        name="t5_bias_tables",
    )(rel_bias, idx, idx_edge)


def _compress_weights(w1, pos):
    hid = w1.shape[1]
    half = CMP_LEN // 2
    w1r = w1.reshape(CMP_LEN, HEAD_DIM, hid)
    eye = jnp.eye(KV_GROUPS, dtype=w1.dtype)
    expand = lambda w: jnp.einsum("ldj,gh->lgdhj", w, eye).reshape(
        half * KV_GROUPS * HEAD_DIM, KV_GROUPS * hid).astype(MXU_DTYPE)
    prow = lambda p: jnp.broadcast_to(p[:, None, :], (half, KV_GROUPS, HEAD_DIM)).reshape(1, -1)
    return expand(w1r[:half]), expand(w1r[half:]), prow(pos[:half]), prow(pos[half:])


def _logit_bound(q_gain, k_gains, bias, floor=None):
    gk = jnp.max(jnp.stack([jnp.max(jnp.abs(k)) for k in k_gains]))
    m = 1.02 * HEAD_DIM * jnp.max(jnp.abs(q_gain)) * gk + jnp.max(jnp.abs(bias))
    if floor is not None:
        m = jnp.maximum(m, jnp.max(floor))
    m = jnp.ceil(m).astype(F32)
    return jnp.stack([m, (m <= MAX_BOUND).astype(F32)])


def _upper_zero(row):
    return jnp.concatenate([row, jnp.zeros_like(row)]).reshape(1, LANES).astype(F32)


def _layer(x, c, w_ada, b_ada, norm_gain, w_in, b_nsa_gate, q_gain_a, k_gain_a, sinks, q_gain_b,
           k_gain_cmp, k_gain_sel, k_gain_win, cmp_pos_k, cmp_pos_v, w_cmp_k1, w_cmp_k2,
           w_cmp_v1, w_cmp_v2, w_out, rel_bias):
    bsz, s, d = x.shape
    assert s % (FAR_TILES * Q_TILE) == 0 and s // SEL_LEN <= HEAD_DIM and s // Q_TILE >= WIN_TILES
    assert FAR_TILES % NSA_STEP_TILES == 0 and s % (SWA_STEP_TILES * Q_TILE) == 0
    assert w_in.shape == (d, D_PROJ) and s % PROJ_TM == 0 and s % OUT_TM == 0
    qscale = HEAD_DIM ** -0.5 * LOG2E

    mod3 = _mod_call(c, w_ada, b_ada).reshape(bsz, 3, d)
    w_in_p = jnp.pad(w_in, ((0, 0), (0, D_PROJ_PAD - D_PROJ))).astype(MXU_DTYPE)
    tile2 = lambda gn: jnp.concatenate([gn, gn]).astype(F32)
    gains = jnp.zeros((8, LANES), F32)
    for n, gn in enumerate((k_gain_a, k_gain_sel, k_gain_win, q_gain_a * qscale, q_gain_b * qscale)):
        gains = gains.at[n].set(tile2(gn))
    bgate = jnp.pad(b_nsa_gate, (0, LANES - b_nsa_gate.shape[0])).reshape(1, LANES).astype(F32)
    (qa, ka, vta, sza, qb, kc, vc, ks, vst, kw, vwt, szb, gb) = _proj_call(
        x, mod3, norm_gain.reshape(1, d).astype(F32), w_in_p, gains, bgate)

    ncp = s // CMP_STRIDE
    wkt, wkb, pkt, pkb = _compress_weights(w_cmp_k1, cmp_pos_k)
    wvt, wvb, pvt, pvb = _compress_weights(w_cmp_v1, cmp_pos_v)
    pos4 = jnp.concatenate([pkt, pkb, pvt, pvb], axis=0).astype(F32)
    pad2 = lambda w: jnp.pad(w, ((0, 0), (0, LANES - HEAD_DIM))).astype(MXU_DTYPE)
    kcmp, vcmpt = _compress_call(kc, vc, wkt, wkb, wvt, wvb, pos4, pad2(w_cmp_k2), pad2(w_cmp_v2),
                                 _upper_zero(k_gain_cmp))

    nba, nbw = _near_tables(rel_bias.astype(F32))
    half = rel_bias.shape[1] // 2
    sinks2 = sinks.astype(F32) * LOG2E
    par_a = _logit_bound(q_gain_a * qscale, [k_gain_a], rel_bias[:, :half] * LOG2E, floor=sinks2)
    par_b = _logit_bound(q_gain_b * qscale, [k_gain_cmp, k_gain_sel, k_gain_win],
                         (rel_bias[:, half:] - rel_bias[N_BUCKETS - 1, half:]) * LOG2E)
    ya = _attn_a_call(par_a, sinks2, qa, ka, vta, sza, nba)
    yb = _attn_b_call(par_b, qb, kcmp, vcmpt, ks, vst, kw, vwt, szb, gb, nbw)
    return _out_call(x, mod3, ya, yb, w_out.astype(MXU_DTYPE))


def kernel(x, c, w_ada, b_ada, norm_gain, w_in, b_nsa_gate, q_gain_a, k_gain_a, sinks, q_gain_b,
           k_gain_cmp, k_gain_sel, k_gain_win, cmp_pos_k, cmp_pos_v, w_cmp_k1, w_cmp_k2,
           w_cmp_v1, w_cmp_v2, w_out, rel_bias):
    for l in range(w_ada.shape[0]):
        x = _layer(x, c, w_ada[l], b_ada[l], norm_gain[l], w_in[l], b_nsa_gate[l], q_gain_a[l],
                   k_gain_a[l], sinks[l], q_gain_b[l], k_gain_cmp[l], k_gain_sel[l],
                   k_gain_win[l], cmp_pos_k[l], cmp_pos_v[l], w_cmp_k1[l], w_cmp_k2[l],
                   w_cmp_v1[l], w_cmp_v2[l], w_out[l], rel_bias)
    return x
```
